```python
import jax
import jax.numpy as jnp
from jax import lax
import numpy as np

D_MODEL = 2048
BATCH = 1
SEQ = 8192
DEPTH = 4

f32 = jnp.float32
GRID_W = 64
CTX_LEN = 256
N_MIXERS = 3
N_RET_LAYERS = (DEPTH + 2) // 3
N_MLSTM_LAYERS = (DEPTH + 1) // 3
N_ATTN_LAYERS = DEPTH // 3
CHUNK = 128
NORM_EPS = 1e-6
ADA_INIT = 0.5

RET_HEADS = 8
RET_DK = D_MODEL // RET_HEADS
RET_DV = 2 * RET_DK
RET_IN = 2 * RET_HEADS * RET_DK + 2 * RET_HEADS * RET_DV
RET_ROPE_BASE = 10000.0

MLSTM_HEADS = 8
MLSTM_DK = D_MODEL // (2 * MLSTM_HEADS)
MLSTM_DV = D_MODEL // MLSTM_HEADS
MLSTM_QK = MLSTM_HEADS * MLSTM_DK
MLSTM_V = MLSTM_HEADS * MLSTM_DV
MLSTM_IN = 2 * MLSTM_QK + 2 * MLSTM_V + 4 * MLSTM_HEADS
MLSTM_CONV = 3
GATE_SOFTCAP = 15.0

ATTN_HEADS = 16
ATTN_KV_HEADS = 8
ATTN_HD = D_MODEL // ATTN_HEADS
ATTN_GROUPS = ATTN_HEADS // ATTN_KV_HEADS
ATTN_IN = ATTN_HEADS * ATTN_HD + 2 * ATTN_KV_HEADS * ATTN_HD
Q_BLOCK = 128
ROPE_BASE = 10000.0

MOE_GROUPS = 4
MOE_PER_GROUP = 8
MOE_EXPERTS = MOE_GROUPS * MOE_PER_GROUP
MOE_TOPK = 2
MOE_FF = 3 * D_MODEL // 8
MOE_BLOCK = 128

kernel_name = 'hybrid_ret_mlstm_gqa_hmoe_dit'


def rmsnorm(x, g):
    x32 = x.astype(f32)
    y = x32 * lax.rsqrt(jnp.mean(x32 * x32, axis=-1, keepdims=True) + NORM_EPS)
    return (y * g.astype(f32)).astype(x.dtype)


def modulate(h, shift, scale):
    return h * (1.0 + scale) + shift


def flip_parts(y, n_ctx):
    return jnp.concatenate([jnp.flip(y[:, :n_ctx], axis=1), jnp.flip(y[:, n_ctx:], axis=1)], axis=1)


def split_ctx_lat(y, n_ctx, need_ctx):
    if need_ctx:
        return y[:, :n_ctx], y[:, n_ctx:]
    return None, y


def rope_tables(pos, dim, base):
    inv = jnp.power(base, -jnp.arange(0, dim, 2, dtype=f32) / dim)
    ang = pos.astype(f32)[:, None] * inv[None, :]
    return jnp.cos(ang), jnp.sin(ang)


def apply_rope(x, cos, sin):
    x32 = x.astype(f32)
    x1, x2 = jnp.split(x32, 2, axis=-1)
    c, s = cos[None, :, None, :], sin[None, :, None, :]
    return jnp.concatenate([x1 * c - x2 * s, x1 * s + x2 * c], axis=-1).astype(x.dtype)


def axial_rope_tables(n_ctx, n_lat):
    rows = n_lat // GRID_W
    row = jnp.repeat(jnp.arange(rows, dtype=jnp.int32), GRID_W)
    col = jnp.tile(jnp.arange(GRID_W, dtype=jnp.int32), rows)
    half = ATTN_HD // 2
    cr, sr = rope_tables(row, half, ROPE_BASE)
    cc, sc = rope_tables(col, half, ROPE_BASE)
    one = jnp.ones((n_ctx, half // 2), f32)
    zero = jnp.zeros((n_ctx, half // 2), f32)
    cat = lambda fill, t: jnp.concatenate([fill, t], axis=0)
    return cat(one, cr), cat(zero, sr), cat(one, cc), cat(zero, sc)


def apply_axial_rope(x, cos_r, sin_r, cos_c, sin_c):
    half = x.shape[-1] // 2
    return jnp.concatenate([apply_rope(x[..., :half], cos_r, sin_r),
                            apply_rope(x[..., half:], cos_c, sin_c)], axis=-1)


def dwconv(x, w):
    k = w.shape[0]
    return lax.conv_general_dilated(x, w[:, None, :], window_strides=(1,), padding=[(k // 2, k // 2)],
                                    dimension_numbers=('NWC', 'WIO', 'NWC'),
                                    feature_group_count=x.shape[-1])


def to_chunks(t):
    bsz, L, H, d = t.shape
    return t.reshape(bsz, L // CHUNK, CHUNK, H, d).transpose(1, 0, 3, 2, 4)


def from_chunks(t):
    nc, bsz, H, C, d = t.shape
    return t.transpose(1, 0, 3, 2, 4).reshape(bsz, nc * C, H, d)


def gate_chunks(t):
    bsz, L, H = t.shape
    return t.reshape(bsz, L // CHUNK, CHUNK, H).transpose(1, 0, 3, 2)


def retention_scan(q, k, v, log_gamma):
    bsz, L, H, DK = q.shape
    DV = v.shape[-1]
    pos = jnp.arange(CHUNK, dtype=f32)
    diff = pos[:, None] - pos[None, :]
    lower = diff >= 0
    lg = log_gamma[:, None, None]
    decay = jnp.where(lower[None], jnp.exp(jnp.where(lower, diff, 0.0)[None] * lg), 0.0)
    q_dec = jnp.exp((pos[None, :] + 1.0) * log_gamma[:, None])[..., None]
    k_dec = jnp.exp((CHUNK - 1.0 - pos[None, :]) * log_gamma[:, None])[..., None]
    c_dec = jnp.exp(CHUNK * log_gamma)[:, None, None]

    def step(S, xs):
        qb, kb, vb = xs
        scores = jnp.einsum('bhid,bhjd->bhij', qb, kb) * decay
        inner = jnp.einsum('bhij,bhjv->bhiv', scores, vb)
        cross = jnp.einsum('bhid,bhdv->bhiv', qb * q_dec, S)
        S_new = c_dec * S + jnp.einsum('bhjd,bhjv->bhdv', kb * k_dec, vb)
        return S_new, inner + cross

    S0 = jnp.zeros((bsz, H, DK, DV), f32)
    _, out = lax.scan(step, S0, (to_chunks(q), to_chunks(k), to_chunks(v)))
    return from_chunks(out)


def mlstm_scan(q, k, v, i_pre, log_f):
    bsz, L, H, DK = q.shape
    DV = v.shape[-1]
    lower = jnp.tril(jnp.ones((CHUNK, CHUNK), bool))

    def step(carry, xs):
        C_prev, n_prev, m_prev = carry
        qb, kb, vb, ib, fb = xs
        b = jnp.cumsum(fb, axis=-1)
        d_log = jnp.where(lower, b[..., :, None] - b[..., None, :] + ib[..., None, :], -jnp.inf)
        m_t = jnp.maximum(b + m_prev[..., None], d_log.max(-1))
        w = jnp.exp(d_log - m_t[..., None])
        s = jnp.einsum('bhid,bhjd->bhij', qb, kb) * w
        inter = jnp.exp(b + m_prev[..., None] - m_t)
        num = jnp.einsum('bhij,bhjv->bhiv', s, vb) + inter[..., None] * jnp.einsum('bhid,bhdv->bhiv', qb, C_prev)
        den = s.sum(-1) + inter * jnp.einsum('bhid,bhd->bhi', qb, n_prev)
        h = num / jnp.maximum(jnp.abs(den), jnp.exp(-m_t))[..., None]
        b_last = b[..., -1:]
        g = b_last - b + ib
        m_new = jnp.maximum(b_last[..., 0] + m_prev, g.max(-1))
        wk = jnp.exp(g - m_new[..., None])[..., None] * kb
        dec = jnp.exp(b_last[..., 0] + m_prev - m_new)
        C_new = dec[..., None, None] * C_prev + jnp.einsum('bhjd,bhjv->bhdv', wk, vb)
        n_new = dec[..., None] * n_prev + wk.sum(-2)
        return (C_new, n_new, m_new), h

    init = (jnp.zeros((bsz, H, DK, DV), f32), jnp.zeros((bsz, H, DK), f32), jnp.zeros((bsz, H), f32))
    _, out = lax.scan(step, init, (to_chunks(q), to_chunks(k), to_chunks(v), gate_chunks(i_pre), gate_chunks(log_f)))
    return from_chunks(out)


def retention_mixer(h_ctx, h_lat, w_in, logit_gamma, gn_w, gn_b, w_out, need_ctx):
    n_ctx = h_ctx.shape[1]
    h = jnp.concatenate([h_ctx, h_lat], axis=1)
    bsz, L, _ = h.shape
    qk_w, v_w = RET_HEADS * RET_DK, RET_HEADS * RET_DV
    q, k, v, g = jnp.split(h @ w_in, [qk_w, 2 * qk_w, 2 * qk_w + v_w], axis=-1)
    q = q.reshape(bsz, L, RET_HEADS, RET_DK) * (RET_DK ** -0.5)
    k = k.reshape(bsz, L, RET_HEADS, RET_DK)
    v = v.reshape(bsz, L, RET_HEADS, RET_DV)
    cos, sin = rope_tables(jnp.arange(L, dtype=jnp.int32), RET_DK, RET_ROPE_BASE)
    log_gamma = jax.nn.log_sigmoid(logit_gamma.astype(f32))
    fp = lambda t: flip_parts(t, n_ctx)
    y_f = retention_scan(apply_rope(q, cos, sin), apply_rope(k, cos, sin), v, log_gamma[0])
    y_b = retention_scan(apply_rope(fp(q), cos, sin), apply_rope(fp(k), cos, sin), fp(v), log_gamma[1])
    y = y_f + fp(y_b)
    if not need_ctx:
        y, g = y[:, n_ctx:], g[:, n_ctx:]
    mu = jnp.mean(y, axis=-1, keepdims=True)
    var = jnp.mean(jnp.square(y - mu), axis=-1, keepdims=True)
    y = ((y - mu) * lax.rsqrt(var + NORM_EPS)).reshape(bsz, y.shape[1], v_w)
    y = (y * gn_w + gn_b) * jax.nn.silu(g.astype(f32))
    out = y.astype(h.dtype) @ w_out
    return split_ctx_lat(out, n_ctx, need_ctx)


def mlstm_mixer(h_ctx, h_lat, w_in, conv_w, b_gate, norm_w, w_out, need_ctx):
    n_ctx = h_ctx.shape[1]
    h = jnp.concatenate([h_ctx, h_lat], axis=1)
    bsz, L, _ = h.shape
    qk, v, o, gates = jnp.split(h @ w_in, [2 * MLSTM_QK, 2 * MLSTM_QK + MLSTM_V, 2 * MLSTM_QK + 2 * MLSTM_V], axis=-1)
    qk = jax.nn.silu(jnp.concatenate([dwconv(qk[:, :n_ctx], conv_w), dwconv(qk[:, n_ctx:], conv_w)], axis=1))
    q, k = jnp.split(qk, 2, axis=-1)
    q = q.reshape(bsz, L, MLSTM_HEADS, MLSTM_DK) * (MLSTM_DK ** -0.5)
    k = k.reshape(bsz, L, MLSTM_HEADS, MLSTM_DK)
    v = v.reshape(bsz, L, MLSTM_HEADS, MLSTM_DV)
    gates = gates.astype(f32).reshape(bsz, L, 2, 2, MLSTM_HEADS) + b_gate.astype(f32)
    i_pre = GATE_SOFTCAP * jnp.tanh(gates[:, :, :, 0] / GATE_SOFTCAP)
    log_f = jax.nn.log_sigmoid(gates[:, :, :, 1])
    fp = lambda t: flip_parts(t, n_ctx)
    y_f = mlstm_scan(q, k, v, i_pre[:, :, 0], log_f[:, :, 0])
    y_b = mlstm_scan(fp(q), fp(k), fp(v), fp(i_pre[:, :, 1]), fp(log_f[:, :, 1]))
    y = y_f + fp(y_b)
    if not need_ctx:
        y, o = y[:, n_ctx:], o[:, n_ctx:]
    y = y * lax.rsqrt(jnp.mean(jnp.square(y), axis=-1, keepdims=True) + NORM_EPS)
    y = y.reshape(bsz, y.shape[1], MLSTM_V) * norm_w * jax.nn.sigmoid(o.astype(f32))
    out = y.astype(h.dtype) @ w_out
    return split_ctx_lat(out, n_ctx, need_ctx)


def attend(qb, kb, vb):
    s = jnp.einsum('bqkgd,bskd->bkgqs', qb, kb).astype(f32) * (ATTN_HD ** -0.5)
    p = jax.nn.softmax(s, axis=-1)
    return jnp.einsum('bkgqs,bskd->bqkgd', p.astype(vb.dtype), vb)


def attention_mixer(h_ctx, h_lat, w_in, q_norm, k_norm, w_out, need_ctx):
    n_ctx, n_lat = h_ctx.shape[1], h_lat.shape[1]
    h = jnp.concatenate([h_ctx, h_lat], axis=1)
    bsz, L, _ = h.shape
    q, k, v = jnp.split(h @ w_in, [ATTN_HEADS * ATTN_HD, (ATTN_HEADS + ATTN_KV_HEADS) * ATTN_HD], axis=-1)
    q = rmsnorm(q.reshape(bsz, L, ATTN_HEADS, ATTN_HD), q_norm)
    k = rmsnorm(k.reshape(bsz, L, ATTN_KV_HEADS, ATTN_HD), k_norm)
    v = v.reshape(bsz, L, ATTN_KV_HEADS, ATTN_HD)
    tabs = axial_rope_tables(n_ctx, n_lat)
    q = apply_axial_rope(q, *tabs).reshape(bsz, L, ATTN_KV_HEADS, ATTN_GROUPS, ATTN_HD)
    k = apply_axial_rope(k, *tabs)
    n_blocks = n_lat // Q_BLOCK
    q_blocks = jnp.moveaxis(q[:, n_ctx:].reshape(bsz, n_blocks, Q_BLOCK, ATTN_KV_HEADS, ATTN_GROUPS, ATTN_HD), 1, 0)
    o_lat = lax.map(lambda qb: attend(qb, k, v), q_blocks)
    o_lat = jnp.moveaxis(o_lat, 0, 1).reshape(bsz, n_lat, D_MODEL)
    y_lat = o_lat @ w_out
    if not need_ctx:
        return None, y_lat
    o_ctx = attend(q[:, :n_ctx], k[:, :n_ctx], v[:, :n_ctx]).reshape(bsz, n_ctx, D_MODEL)
    return o_ctx @ w_out, y_lat


def hier_moe(h, w_rg, b_rg, w_re, b_re, w_gate, w_up, w_down):
    T, D = h.shape
    g_prob = jax.nn.softmax((h @ w_rg).astype(f32) + b_rg.astype(f32), axis=-1)
    g_w, g_idx = lax.top_k(g_prob, 1)
    e_logit = ((h @ w_re).astype(f32) + b_re.astype(f32)).reshape(T, MOE_GROUPS, MOE_PER_GROUP)
    e_logit = jnp.take_along_axis(e_logit, g_idx[:, :, None], axis=1)[:, 0]
    e_w, e_idx = lax.top_k(jax.nn.softmax(e_logit, axis=-1), MOE_TOPK)
    e_w = e_w / jnp.sum(e_w, axis=-1, keepdims=True)
    weight = (g_w * e_w).reshape(-1)
    expert = (g_idx * MOE_PER_GROUP + e_idx).reshape(-1)
    token = jnp.repeat(jnp.arange(T, dtype=jnp.int32), MOE_TOPK)
    n_assign = T * MOE_TOPK
    order = jnp.argsort(expert)
    e_sorted = expert[order]
    counts = jnp.bincount(expert, length=MOE_EXPERTS)
    padded = (counts + MOE_BLOCK - 1) // MOE_BLOCK * MOE_BLOCK
    seg_end = jnp.cumsum(padded)
    dest = (seg_end - padded)[e_sorted] + jnp.arange(n_assign, dtype=jnp.int32) - (jnp.cumsum(counts) - counts)[e_sorted]
    n_blocks = (n_assign + MOE_EXPERTS * (MOE_BLOCK - 1)) // MOE_BLOCK + 1
    slot_tok = jnp.zeros((n_blocks * MOE_BLOCK,), jnp.int32).at[dest].set(token[order])
    slot_w = jnp.zeros((n_blocks * MOE_BLOCK,), f32).at[dest].set(weight[order])
    block_expert = jnp.minimum(jnp.searchsorted(seg_end, jnp.arange(n_blocks, dtype=jnp.int32) * MOE_BLOCK, side='right'),
                               MOE_EXPERTS - 1)

    def expert_block(args):
        tok, w, e = args
        xb = h[tok]
        yb = (jax.nn.silu(xb @ w_gate[e]) * (xb @ w_up[e])) @ w_down[e]
        return yb * w[:, None].astype(yb.dtype)

    ys = lax.map(expert_block, (slot_tok.reshape(n_blocks, MOE_BLOCK), slot_w.reshape(n_blocks, MOE_BLOCK), block_expert))
    return jnp.zeros_like(h).at[slot_tok].add(ys.reshape(-1, D).astype(h.dtype))


def setup_inputs(seed: int = 0) -> dict:
    key = jax.random.key(seed)
    ks = iter(jax.random.split(key, 40))
    D = D_MODEL

    def nrm(shape, scale):
        return jax.random.normal(next(ks), shape, f32) * scale

    x = nrm((BATCH, SEQ, D), 1.0)
    c = nrm((BATCH, D), 1.0)
    ctx = nrm((BATCH, CTX_LEN, D), 1.0)
    c_ctx = nrm((D,), 1.0)
    w_ada = nrm((DEPTH, D, 6 * D), ADA_INIT * D ** -0.5)
    b_ada = nrm((DEPTH, 6 * D), 0.02)
    norm_g = 1.0 + nrm((DEPTH, 2, D), 0.02)
    ret_w_in = nrm((N_RET_LAYERS, D, RET_IN), D ** -0.5)
    base_logit = jnp.log(jnp.power(2.0, 5.0 + jnp.arange(RET_HEADS, dtype=f32)) - 1.0)
    ret_logit_gamma = base_logit + nrm((N_RET_LAYERS, 2, RET_HEADS), 0.05)
    ret_gn_w = 1.0 + nrm((N_RET_LAYERS, RET_HEADS * RET_DV), 0.02)
    ret_gn_b = nrm((N_RET_LAYERS, RET_HEADS * RET_DV), 0.02)
    ret_w_out = nrm((N_RET_LAYERS, RET_HEADS * RET_DV, D), (RET_HEADS * RET_DV) ** -0.5)
    mlstm_w_in = nrm((N_MLSTM_LAYERS, D, MLSTM_IN), D ** -0.5)
    mlstm_conv_w = nrm((N_MLSTM_LAYERS, MLSTM_CONV, 2 * MLSTM_QK), MLSTM_CONV ** -0.5)
    i_bias = nrm((N_MLSTM_LAYERS, 2, 1, MLSTM_HEADS), 0.1)
    f_bias = jnp.linspace(3.0, 6.0, MLSTM_HEADS, dtype=f32) + nrm((N_MLSTM_LAYERS, 2, 1, MLSTM_HEADS), 0.1)
    mlstm_b_gate = jnp.concatenate([i_bias, f_bias], axis=2)
    mlstm_norm_w = 1.0 + nrm((N_MLSTM_LAYERS, MLSTM_V), 0.02)
    mlstm_w_out = nrm((N_MLSTM_LAYERS, MLSTM_V, D), MLSTM_V ** -0.5)
    attn_w_in = nrm((N_ATTN_LAYERS, D, ATTN_IN), D ** -0.5)
    attn_q_norm = 1.0 + nrm((N_ATTN_LAYERS, ATTN_HD), 0.02)
    attn_k_norm = 1.0 + nrm((N_ATTN_LAYERS, ATTN_HD), 0.02)
    attn_w_out = nrm((N_ATTN_LAYERS, ATTN_HEADS * ATTN_HD, D), (ATTN_HEADS * ATTN_HD) ** -0.5)
    moe_w_router_group = nrm((DEPTH, D, MOE_GROUPS), D ** -0.5)
    moe_b_router_group = nrm((DEPTH, MOE_GROUPS), 0.01)
    moe_w_router_expert = nrm((DEPTH, D, MOE_EXPERTS), D ** -0.5)
    moe_b_router_expert = nrm((DEPTH, MOE_EXPERTS), 0.01)
    moe_w_gate = nrm((DEPTH, MOE_EXPERTS, D, MOE_FF), D ** -0.5)
    moe_w_up = nrm((DEPTH, MOE_EXPERTS, D, MOE_FF), D ** -0.5)
    moe_w_down = nrm((DEPTH, MOE_EXPERTS, MOE_FF, D), MOE_FF ** -0.5)
    final_norm_g = 1.0 + nrm((D,), 0.02)
    return {'x': x, 'c': c, 'ctx': ctx, 'c_ctx': c_ctx, 'w_ada': w_ada, 'b_ada': b_ada, 'norm_g': norm_g,
            'ret_w_in': ret_w_in, 'ret_logit_gamma': ret_logit_gamma, 'ret_gn_w': ret_gn_w, 'ret_gn_b': ret_gn_b,
            'ret_w_out': ret_w_out, 'mlstm_w_in': mlstm_w_in, 'mlstm_conv_w': mlstm_conv_w,
            'mlstm_b_gate': mlstm_b_gate, 'mlstm_norm_w': mlstm_norm_w, 'mlstm_w_out': mlstm_w_out,
            'attn_w_in': attn_w_in, 'attn_q_norm': attn_q_norm, 'attn_k_norm': attn_k_norm, 'attn_w_out': attn_w_out,
            'moe_w_router_group': moe_w_router_group, 'moe_b_router_group': moe_b_router_group,
            'moe_w_router_expert': moe_w_router_expert, 'moe_b_router_expert': moe_b_router_expert,
            'moe_w_gate': moe_w_gate, 'moe_w_up': moe_w_up, 'moe_w_down': moe_w_down, 'final_norm_g': final_norm_g}


def reference(x, c, ctx, c_ctx, w_ada, b_ada, norm_g, ret_w_in, ret_logit_gamma, ret_gn_w, ret_gn_b, ret_w_out,
              mlstm_w_in, mlstm_conv_w, mlstm_b_gate, mlstm_norm_w, mlstm_w_out, attn_w_in, attn_q_norm,
              attn_k_norm, attn_w_out, moe_w_router_group, moe_b_router_group, moe_w_router_expert,
              moe_b_router_expert, moe_w_gate, moe_w_up, moe_w_down, final_norm_g):
    n_ctx, n_lat = ctx.shape[1], x.shape[1]
    x_lat, x_ctx = x, ctx
    s_lat = jax.nn.silu(c)
    s_ctx = jax.nn.silu(c_ctx)
    for i in range(DEPTH):
        kind, j = i % N_MIXERS, i // N_MIXERS
        need_ctx = i < DEPTH - 1
        mod_lat = jnp.split((s_lat @ w_ada[i] + b_ada[i])[:, None, :], 6, axis=-1)
        mod_ctx = jnp.split((s_ctx @ w_ada[i] + b_ada[i])[None, None, :], 6, axis=-1)
        h_lat = modulate(rmsnorm(x_lat, norm_g[i, 0]), mod_lat[0], mod_lat[1])
        h_ctx = modulate(rmsnorm(x_ctx, norm_g[i, 0]), mod_ctx[0], mod_ctx[1])
        if kind == 0:
            y_ctx, y_lat = retention_mixer(h_ctx, h_lat, ret_w_in[j], ret_logit_gamma[j], ret_gn_w[j], ret_gn_b[j],
                                           ret_w_out[j], need_ctx)
        elif kind == 1:
            y_ctx, y_lat = mlstm_mixer(h_ctx, h_lat, mlstm_w_in[j], mlstm_conv_w[j], mlstm_b_gate[j], mlstm_norm_w[j],
                                       mlstm_w_out[j], need_ctx)
        else:
            y_ctx, y_lat = attention_mixer(h_ctx, h_lat, attn_w_in[j], attn_q_norm[j], attn_k_norm[j], attn_w_out[j],
                                           need_ctx)
        x_lat = x_lat + (mod_lat[2] * y_lat).astype(x_lat.dtype)
        if need_ctx:
            x_ctx = x_ctx + (mod_ctx[2] * y_ctx).astype(x_ctx.dtype)
        h_lat = modulate(rmsnorm(x_lat, norm_g[i, 1]), mod_lat[3], mod_lat[4])
        if need_ctx:
            h_ctx = modulate(rmsnorm(x_ctx, norm_g[i, 1]), mod_ctx[3], mod_ctx[4])
            h_all = jnp.concatenate([h_ctx, h_lat], axis=1)
        else:
            h_all = h_lat
        y_all = hier_moe(h_all.reshape(-1, D_MODEL), moe_w_router_group[i], moe_b_router_group[i],
                         moe_w_router_expert[i], moe_b_router_expert[i], moe_w_gate[i], moe_w_up[i],
                         moe_w_down[i]).reshape(h_all.shape)
        x_lat = x_lat + (mod_lat[5] * y_all[:, y_all.shape[1] - n_lat:]).astype(x_lat.dtype)
        if need_ctx:
            x_ctx = x_ctx + (mod_ctx[5] * y_all[:, :n_ctx]).astype(x_ctx.dtype)
    return rmsnorm(x_lat, final_norm_g)
```

```python
import functools
import math

import jax
import jax.numpy as jnp
import numpy as np
from jax import lax
from jax.experimental import pallas as pl
from jax.experimental.pallas import tpu as pltpu

F32 = jnp.float32
BF16 = jnp.bfloat16
I32 = jnp.int32
U32 = jnp.uint32

LANES = 128
CHUNK = 128
NORM_EPS = 1e-6
GRID_W = 64
RET_HEADS = 8
RET_ROPE_BASE = 10000.0
MLSTM_HEADS = 8
GATE_SOFTCAP = 15.0
ATTN_HEADS = 16
ATTN_KV_HEADS = 8
ROPE_BASE = 10000.0
MOE_GROUPS = 4
MOE_PER_GROUP = 8
MOE_EXPERTS = MOE_GROUPS * MOE_PER_GROUP
MOE_TOPK = 2
MOE_BLOCK = 256
ROW_TILE = 256
VMEM_LIMIT = 56 * 1024 * 1024
NEG_INF = float("-inf")


def _cparams(sem, vmem=VMEM_LIMIT):
    return pltpu.CompilerParams(dimension_semantics=sem, vmem_limit_bytes=vmem)


def _pick_tile(total, candidates):
    for c in candidates:
        if total % c == 0:
            return c
    raise ValueError(f"no tile for {total} in {candidates}")


def _mod_row(mod_ref, i, tile, n_ctx):
    sel = jnp.where(i < n_ctx // tile, 1, 0)
    return mod_ref[pl.ds(sel, 1), :]


def _ada_kernel(s_ref, w_ref, b_ref, o_ref):
    w = w_ref[0].astype(BF16)
    o_ref[0] = jnp.dot(s_ref[...], w, preferred_element_type=F32) + b_ref[0]


def _ada_modulation(s8, w_ada, b_ada):
    depth, d, n = w_ada.shape
    tn = _pick_tile(n, (1024, 512, 256, 128))
    return pl.pallas_call(
        _ada_kernel,
        out_shape=jax.ShapeDtypeStruct((depth, 8, n), F32),
        grid=(depth, n // tn),
        in_specs=[pl.BlockSpec((8, d), lambda l, j: (0, 0)),
                  pl.BlockSpec((1, d, tn), lambda l, j: (l, 0, j)),
                  pl.BlockSpec((1, 1, tn), lambda l, j: (l, 0, j))],
        out_specs=pl.BlockSpec((1, 8, tn), lambda l, j: (l, 0, j)),
        compiler_params=_cparams(("arbitrary", "arbitrary")),
        name="ada_modulation",
    )(s8, w_ada, b_ada.reshape(depth, 1, n))


def _rms(x):
    return x * lax.rsqrt(jnp.mean(x * x, axis=-1, keepdims=True) + NORM_EPS)


def _norm_mod_kernel(x_ref, g_ref, sh_ref, sc_ref, o_ref, *, n_ctx):
    i = pl.program_id(0)
    tm = x_ref.shape[0]
    y = _rms(x_ref[...]) * g_ref[...]
    y = y * (1.0 + _mod_row(sc_ref, i, tm, n_ctx)) + _mod_row(sh_ref, i, tm, n_ctx)
    o_ref[...] = y.astype(o_ref.dtype)


def _norm_mod(x, g, mod, k_shift, k_scale, n_ctx):
    t, d = x.shape
    tm = ROW_TILE
    return pl.pallas_call(
        functools.partial(_norm_mod_kernel, n_ctx=n_ctx),
        out_shape=jax.ShapeDtypeStruct((t, d), BF16),
        grid=(t // tm,),
        in_specs=[pl.BlockSpec((tm, d), lambda i: (i, 0)),
                  pl.BlockSpec((1, d), lambda i: (0, 0)),
                  pl.BlockSpec((8, d), lambda i: (0, k_shift)),
                  pl.BlockSpec((8, d), lambda i: (0, k_scale))],
        out_specs=pl.BlockSpec((tm, d), lambda i: (i, 0)),
        compiler_params=_cparams(("arbitrary",)),
        name="norm_mod",
    )(x, g.reshape(1, d), mod, mod)


def _final_norm_kernel(x_ref, g_ref, o_ref):
    o_ref[...] = _rms(x_ref[...]) * g_ref[...]


def _final_norm(x, g, n_ctx):
    t, d = x.shape
    tm = ROW_TILE
    off = n_ctx // tm
    return pl.pallas_call(
        _final_norm_kernel,
        out_shape=jax.ShapeDtypeStruct((t - n_ctx, d), F32),
        grid=((t - n_ctx) // tm,),
        in_specs=[pl.BlockSpec((tm, d), lambda i: (i + off, 0)),
                  pl.BlockSpec((1, d), lambda i: (0, 0))],
        out_specs=pl.BlockSpec((tm, d), lambda i: (i, 0)),
        compiler_params=_cparams(("arbitrary",)),
        name="final_norm",
    )(x, g.reshape(1, d))


def _cast_weight_once(w_ref, wbf_ref):
    @pl.when(pl.program_id(1) == 0)
    def _():
        wbf_ref[...] = w_ref[...].astype(BF16)


def _mm_plain_kernel(a_ref, w_ref, o_ref, wbf_ref, *, scale):
    _cast_weight_once(w_ref, wbf_ref)
    acc = jnp.dot(a_ref[...], wbf_ref[...], preferred_element_type=F32)
    if scale != 1.0:
        acc = acc * scale
    o_ref[...] = acc.astype(o_ref.dtype)


def _mm_residual_kernel(a_ref, w_ref, x_ref, mod_ref, o_ref, wbf_ref, *, n_ctx):
    _cast_weight_once(w_ref, wbf_ref)
    tm = a_ref.shape[0]
    acc = jnp.dot(a_ref[...], wbf_ref[...], preferred_element_type=F32)
    row = pl.program_id(1) * tm + lax.broadcasted_iota(I32, (tm, 1), 0)
    gate = jnp.where(row < n_ctx, mod_ref[1:2, :], mod_ref[0:1, :])
    o_ref[...] = x_ref[...] + gate * acc


def _mm_rope_kernel(a_ref, w_ref, cf_ref, sf_ref, cb_ref, sb_ref, o_ref, wbf_ref, *, scale, head_dim):
    _cast_weight_once(w_ref, wbf_ref)
    acc = jnp.dot(a_ref[...], wbf_ref[...], preferred_element_type=F32)
    if scale != 1.0:
        acc = acc * scale
    half = head_dim // 2
    for d, (c_ref, s_ref) in enumerate(((cf_ref, sf_ref), (cb_ref, sb_ref))):
        c, s = c_ref[...], s_ref[...]
        for h in range(acc.shape[1] // head_dim):
            x1 = acc[:, h * head_dim:h * head_dim + half]
            x2 = acc[:, h * head_dim + half:(h + 1) * head_dim]
            o_ref[d, :, h * head_dim:h * head_dim + half] = (x1 * c - x2 * s).astype(o_ref.dtype)
            o_ref[d, :, h * head_dim + half:(h + 1) * head_dim] = (x1 * s + x2 * c).astype(o_ref.dtype)


def _mm_tiles(m, k, n):
    tm = _pick_tile(m, (768, 512, 384, 256, 128))
    tn_cap = 1024 if k <= 2048 else 512
    tn = _pick_tile(n, tuple(c for c in (1024, 512, 256, 128) if c <= tn_cap))
    return tm, tn


def _matmul(a, w, layer, col0, n, out_dtype, scale=1.0):
    m, k = a.shape
    tm, tn = _mm_tiles(m, k, n)
    assert col0 % tn == 0
    c0 = col0 // tn
    return pl.pallas_call(
        functools.partial(_mm_plain_kernel, scale=scale),
        out_shape=jax.ShapeDtypeStruct((m, n), out_dtype),
        grid=(n // tn, m // tm),
        in_specs=[pl.BlockSpec((tm, k), lambda j, i: (i, 0)),
                  pl.BlockSpec((None, k, tn), lambda j, i: (layer, 0, j + c0))],
        out_specs=pl.BlockSpec((tm, tn), lambda j, i: (i, j)),
        scratch_shapes=[pltpu.VMEM((k, tn), BF16)],
        compiler_params=_cparams(("arbitrary", "arbitrary")),
        name="matmul",
    )(a, w)


def _matmul_residual(a, w, layer, x, mod, k_gate, n_ctx):
    m, k = a.shape
    n = w.shape[2]
    tm, tn = _mm_tiles(m, k, n)
    kb = k_gate * (n // tn)
    return pl.pallas_call(
        functools.partial(_mm_residual_kernel, n_ctx=n_ctx),
        out_shape=jax.ShapeDtypeStruct((m, n), F32),
        grid=(n // tn, m // tm),
        in_specs=[pl.BlockSpec((tm, k), lambda j, i: (i, 0)),
                  pl.BlockSpec((None, k, tn), lambda j, i: (layer, 0, j)),
                  pl.BlockSpec((tm, tn), lambda j, i: (i, j)),
                  pl.BlockSpec((8, tn), lambda j, i: (0, kb + j))],
        out_specs=pl.BlockSpec((tm, tn), lambda j, i: (i, j)),
        scratch_shapes=[pltpu.VMEM((k, tn), BF16)],
        compiler_params=_cparams(("arbitrary", "arbitrary")),
        name="matmul_residual",
    )(a, w, x, mod)


def _matmul_rope(a, w, layer, col0, n, tables, scale, head_dim):
    m, k = a.shape
    tm, tn = _mm_tiles(m, k, n)
    assert col0 % tn == 0 and tn % head_dim == 0
    c0 = col0 // tn
    half = head_dim // 2
    tab_spec = pl.BlockSpec((tm, half), lambda j, i: (i, 0))
    return pl.pallas_call(
        functools.partial(_mm_rope_kernel, scale=scale, head_dim=head_dim),
        out_shape=jax.ShapeDtypeStruct((2, m, n), BF16),
        grid=(n // tn, m // tm),
        in_specs=[pl.BlockSpec((tm, k), lambda j, i: (i, 0)),
                  pl.BlockSpec((None, k, tn), lambda j, i: (layer, 0, j + c0)),
                  tab_spec, tab_spec, tab_spec, tab_spec],
        out_specs=pl.BlockSpec((2, tm, tn), lambda j, i: (0, i, j)),
        scratch_shapes=[pltpu.VMEM((k, tn), BF16)],
        compiler_params=_cparams(("arbitrary", "arbitrary")),
        name="matmul_rope",
    )(a, w, *tables)


def _chunk_index(c, n_chunks, ctx_chunks, reverse):
    if not reverse:
        return c
    return jnp.where(c < ctx_chunks, ctx_chunks - 1 - c, n_chunks - 1 - (c - ctx_chunks))


def _dot_nt(a, b):
    return lax.dot_general(a, b, (((1,), (1,)), ((), ())), preferred_element_type=F32)


def _dot_tn(a, b):
    return lax.dot_general(a, b, (((0,), (0,)), ((), ())), preferred_element_type=F32)


def _retention_kernel(lg_ref, q_ref, k_ref, v_ref, o_ref, s_ref, *, reverse, heads, dk, dv):
    c = pl.program_id(0)

    @pl.when(c == 0)
    def _():
        s_ref[...] = jnp.zeros_like(s_ref)

    n = q_ref.shape[0]
    ri = lax.broadcasted_iota(I32, (n, n), 0)
    ci = lax.broadcasted_iota(I32, (n, n), 1)
    diff = (ci - ri) if reverse else (ri - ci)
    dmask = diff >= 0
    dist = jnp.where(dmask, diff, 0).astype(F32)
    r1 = lax.broadcasted_iota(I32, (n, 1), 0)
    pos = ((n - 1 - r1) if reverse else r1).astype(F32)
    for h in range(heads):
        lg = jnp.full((1, 1), lg_ref[h], F32)
        decay = jnp.where(dmask, jnp.exp(dist * lg), 0.0)
        q_dec = jnp.exp((pos + 1.0) * lg)
        k_dec = jnp.exp((n - 1.0 - pos) * lg)
        c_dec = jnp.exp(float(n) * lg)
        q = q_ref[:, h * dk:(h + 1) * dk]
        k = k_ref[:, h * dk:(h + 1) * dk]
        v = v_ref[:, h * dv:(h + 1) * dv]
        s_prev = s_ref[h]
        scores = (_dot_nt(q, k) * decay).astype(BF16)
        inner = jnp.dot(scores, v, preferred_element_type=F32)
        cross = jnp.dot(q, s_prev.astype(BF16), preferred_element_type=F32)
        o_ref[:, h * dv:(h + 1) * dv] = inner + q_dec * cross
        kd = (k.astype(F32) * k_dec).astype(BF16)
        s_ref[h] = c_dec * s_prev + _dot_tn(kd, v)


def _retention_scan(log_gamma, q, k, vg, d_idx, n_ctx, heads, dk, dv, reverse):
    t = q.shape[1]
    nc, cc = t // CHUNK, n_ctx // CHUNK
    cm = lambda c: _chunk_index(c, nc, cc, reverse)
    return pl.pallas_call(
        functools.partial(_retention_kernel, reverse=reverse, heads=heads, dk=dk, dv=dv),
        out_shape=jax.ShapeDtypeStruct((t, heads * dv), F32),
        grid=(nc,),
        in_specs=[pl.BlockSpec(memory_space=pltpu.SMEM),
                  pl.BlockSpec((None, CHUNK, heads * dk), lambda c: (d_idx, cm(c), 0)),
                  pl.BlockSpec((None, CHUNK, heads * dk), lambda c: (d_idx, cm(c), 0)),
                  pl.BlockSpec((CHUNK, heads * dv), lambda c: (cm(c), 0))],
        out_specs=pl.BlockSpec((CHUNK, heads * dv), lambda c: (cm(c), 0)),
        scratch_shapes=[pltpu.VMEM((heads, dk, dv), F32)],
        compiler_params=_cparams(("arbitrary",)),
        name="retention_scan_rev" if reverse else "retention_scan_fwd",
    )(log_gamma, q, k, vg)


def _ret_post_kernel(yf_ref, yb_ref, g_ref, w_ref, b_ref, o_ref, *, heads, dv):
    for h in range(heads):
        sl = slice(h * dv, (h + 1) * dv)
        y = yf_ref[:, sl] + yb_ref[:, sl]
        mu = jnp.mean(y, axis=-1, keepdims=True)
        yc = y - mu
        var = jnp.mean(yc * yc, axis=-1, keepdims=True)
        yn = yc * lax.rsqrt(var + NORM_EPS)
        g = g_ref[:, sl].astype(F32)
        silu = g * (1.0 / (1.0 + jnp.exp(-g)))
        o_ref[:, sl] = ((yn * w_ref[:, sl] + b_ref[:, sl]) * silu).astype(o_ref.dtype)


def _ret_post(yf, yb, vg, gn_w, gn_b, heads, dv):
    t, vw = yf.shape
    tm = ROW_TILE
    return pl.pallas_call(
        functools.partial(_ret_post_kernel, heads=heads, dv=dv),
        out_shape=jax.ShapeDtypeStruct((t, vw), BF16),
        grid=(t // tm,),
        in_specs=[pl.BlockSpec((tm, vw), lambda i: (i, 0)),
                  pl.BlockSpec((tm, vw), lambda i: (i, 0)),
                  pl.BlockSpec((tm, vw), lambda i: (i, 1)),
                  pl.BlockSpec((1, vw), lambda i: (0, 0)),
                  pl.BlockSpec((1, vw), lambda i: (0, 0))],
        out_specs=pl.BlockSpec((tm, vw), lambda i: (i, 0)),
        compiler_params=_cparams(("arbitrary",)),
        name="retention_post",
    )(yf, yb, vg, gn_w.reshape(1, vw), gn_b.reshape(1, vw))


def _conv_silu_kernel(x_ref, prev_ref, next_ref, w_ref, o_ref, *, n_ctx, t_total, q_cols, q_scale):
    i = pl.program_id(0)
    tm = x_ref.shape[0]
    x = x_ref[...]
    row = lax.broadcasted_iota(I32, (tm, 1), 0)
    grow = i * tm + row
    halo = prev_ref.shape[0]
    x_prev = jnp.where(row == 0, prev_ref[halo - 1:halo, :], pltpu.roll(x, 1, axis=0))
    x_prev = jnp.where((grow == 0) | (grow == n_ctx), 0.0, x_prev)
    x_next = jnp.where(row == tm - 1, next_ref[0:1, :], pltpu.roll(x, tm - 1, axis=0))
    x_next = jnp.where((grow == n_ctx - 1) | (grow == t_total - 1), 0.0, x_next)
    y = x_prev * w_ref[0:1, :] + x * w_ref[1:2, :] + x_next * w_ref[2:3, :]
    y = y * (1.0 / (1.0 + jnp.exp(-y)))
    o_ref[:, :q_cols] = (y[:, :q_cols] * q_scale).astype(o_ref.dtype)
    o_ref[:, q_cols:] = y[:, q_cols:].astype(o_ref.dtype)


def _conv_silu(x, conv_w, n_ctx, q_cols, q_scale):
    t, c = x.shape
    tm = ROW_TILE
    halo = 8
    r = tm // halo
    last = t // halo - 1
    w8 = jnp.zeros((8, c), F32).at[:3].set(conv_w)
    return pl.pallas_call(
        functools.partial(_conv_silu_kernel, n_ctx=n_ctx, t_total=t, q_cols=q_cols, q_scale=q_scale),
        out_shape=jax.ShapeDtypeStruct((t, c), BF16),
        grid=(t // tm,),
        in_specs=[pl.BlockSpec((tm, c), lambda i: (i, 0)),
                  pl.BlockSpec((halo, c), lambda i: (jnp.maximum(i * r - 1, 0), 0)),
                  pl.BlockSpec((halo, c), lambda i: (jnp.minimum((i + 1) * r, last), 0)),
                  pl.BlockSpec((8, c), lambda i: (0, 0))],
        out_specs=pl.BlockSpec((tm, c), lambda i: (i, 0)),
        compiler_params=_cparams(("arbitrary",)),
        name="mlstm_conv_silu",
    )(x, x, x, w8)


def _mlstm_kernel(qk_ref, v_ref, gt_ref, bias_ref, o_ref, c_ref, n_ref, m_ref, *, reverse, heads, dk, dv, d_idx):
    c = pl.program_id(0)

    @pl.when(c == 0)
    def _():
        c_ref[...] = jnp.zeros_like(c_ref)
        n_ref[...] = jnp.zeros_like(n_ref)
        m_ref[...] = jnp.zeros_like(m_ref)

    n = qk_ref.shape[0]
    gates = gt_ref[...] + bias_ref[...]
    i_all = GATE_SOFTCAP * jnp.tanh(gates * (1.0 / GATE_SOFTCAP))
    f_all = -(jnp.maximum(-gates, 0.0) + jnp.log1p(jnp.exp(-jnp.abs(gates))))
    ri = lax.broadcasted_iota(I32, (n, n), 0)
    ci = lax.broadcasted_iota(I32, (n, n), 1)
    mask = (ci >= ri) if reverse else (ci <= ri)
    tri = jnp.where(mask, 1.0, 0.0).astype(F32)
    b_all = jnp.dot(tri, f_all, preferred_element_type=F32, precision=lax.Precision.HIGHEST)
    b_all_t = b_all.T
    i_all_t = i_all.T
    last = 0 if reverse else n - 1
    qk_w = heads * dk
    for h in range(heads):
        ic = d_idx * 2 * heads + h
        fc = ic + heads
        b_col = b_all[:, fc:fc + 1]
        b_row = b_all_t[fc:fc + 1, :]
        i_col = i_all[:, ic:ic + 1]
        i_row = i_all_t[ic:ic + 1, :]
        m_prev = m_ref[h][:, 0:1]
        q = qk_ref[:, h * dk:(h + 1) * dk]
        k = qk_ref[:, qk_w + h * dk:qk_w + (h + 1) * dk]
        v = v_ref[:, h * dv:(h + 1) * dv]
        c_prev = c_ref[h]
        n_prev = n_ref[h]
        d_log = jnp.where(mask, b_col - b_row + i_row, NEG_INF)
        m_t = jnp.maximum(b_col + m_prev, jnp.max(d_log, axis=-1, keepdims=True))
        w = jnp.exp(d_log - m_t)
        s = _dot_nt(q, k) * w
        inter = jnp.exp(b_col + m_prev - m_t)
        num = jnp.dot(s.astype(BF16), v, preferred_element_type=F32) + inter * jnp.dot(
            q, c_prev.astype(BF16), preferred_element_type=F32)
        den = jnp.sum(s, axis=-1, keepdims=True) + inter * jnp.sum(q.astype(F32) * n_prev, axis=-1, keepdims=True)
        o_ref[:, h * dv:(h + 1) * dv] = num / jnp.maximum(jnp.abs(den), jnp.exp(-m_t))
        b_last = b_col[last:last + 1, :]
        g = b_last - b_col + i_col
        m_new = jnp.maximum(b_last + m_prev, jnp.max(g, axis=0, keepdims=True))
        wk = jnp.exp(g - m_new) * k.astype(F32)
        dec = jnp.exp(b_last + m_prev - m_new)
        c_ref[h] = dec * c_prev + _dot_tn(wk.astype(BF16), v)
        n_ref[h] = dec * n_prev + jnp.sum(wk, axis=0, keepdims=True)
        m_ref[h] = jnp.broadcast_to(m_new, m_ref.shape[1:])


def _mlstm_scan(qk, vo, gates, bias, n_ctx, heads, dk, dv, reverse):
    t = qk.shape[0]
    nc, cc = t // CHUNK, n_ctx // CHUNK
    cm = lambda c: _chunk_index(c, nc, cc, reverse)
    return pl.pallas_call(
        functools.partial(_mlstm_kernel, reverse=reverse, heads=heads, dk=dk, dv=dv, d_idx=1 if reverse else 0),
        out_shape=jax.ShapeDtypeStruct((t, heads * dv), F32),
        grid=(nc,),
        in_specs=[pl.BlockSpec((CHUNK, 2 * heads * dk), lambda c: (cm(c), 0)),
                  pl.BlockSpec((CHUNK, heads * dv), lambda c: (cm(c), 0)),
                  pl.BlockSpec((CHUNK, LANES), lambda c: (cm(c), 0)),
                  pl.BlockSpec((1, LANES), lambda c: (0, 0))],
        out_specs=pl.BlockSpec((CHUNK, heads * dv), lambda c: (cm(c), 0)),
        scratch_shapes=[pltpu.VMEM((heads, dk, dv), F32),
                        pltpu.VMEM((heads, 1, dk), F32),
                        pltpu.VMEM((heads, 1, LANES), F32)],
        compiler_params=_cparams(("arbitrary",)),
        name="mlstm_scan_rev" if reverse else "mlstm_scan_fwd",
    )(qk, vo, gates, bias)


def _mlstm_post_kernel(yf_ref, yb_ref, o_ref_in, w_ref, o_ref, *, heads, dv):
    for h in range(heads):
        sl = slice(h * dv, (h + 1) * dv)
        y = _rms(yf_ref[:, sl] + yb_ref[:, sl])
        o = o_ref_in[:, sl].astype(F32)
        o_ref[:, sl] = (y * w_ref[:, sl] * (1.0 / (1.0 + jnp.exp(-o)))).astype(o_ref.dtype)


def _mlstm_post(yf, yb, vo, norm_w, heads, dv):
    t, vw = yf.shape
    tm = ROW_TILE
    return pl.pallas_call(
        functools.partial(_mlstm_post_kernel, heads=heads, dv=dv),
        out_shape=jax.ShapeDtypeStruct((t, vw), BF16),
        grid=(t // tm,),
        in_specs=[pl.BlockSpec((tm, vw), lambda i: (i, 0)),
                  pl.BlockSpec((tm, vw), lambda i: (i, 0)),
                  pl.BlockSpec((tm, vw), lambda i: (i, 1)),
                  pl.BlockSpec((1, vw), lambda i: (0, 0))],
        out_specs=pl.BlockSpec((tm, vw), lambda i: (i, 0)),
        compiler_params=_cparams(("arbitrary",)),
        name="mlstm_post",
    )(yf, yb, vo, norm_w.reshape(1, vw))


def _qk_norm_rope_kernel(x_ref, cos_ref, sin_ref, qn_ref, kn_ref, q_ref, k_ref, *, q_heads, k_heads, hd):
    cos, sin = cos_ref[...], sin_ref[...]
    lane = lax.broadcasted_iota(I32, (1, hd), 1)
    first = (lane % (hd // 2)) < (hd // 4)
    for h in range(q_heads + k_heads):
        x = x_ref[:, h * hd:(h + 1) * hd]
        w = qn_ref[...] if h < q_heads else kn_ref[...]
        xh = _rms(x) * w
        partner = jnp.where(first, pltpu.roll(xh, hd - hd // 4, axis=1), pltpu.roll(xh, hd // 4, axis=1))
        y = (xh * cos + partner * sin).astype(q_ref.dtype)
        if h < q_heads:
            q_ref[:, h * hd:(h + 1) * hd] = y
        else:
            k_ref[:, (h - q_heads) * hd:(h - q_heads + 1) * hd] = y


def _qk_norm_rope(qk, cos, sin, q_norm, k_norm, q_heads, k_heads, hd):
    t = qk.shape[0]
    tm = ROW_TILE
    return pl.pallas_call(
        functools.partial(_qk_norm_rope_kernel, q_heads=q_heads, k_heads=k_heads, hd=hd),
        out_shape=(jax.ShapeDtypeStruct((t, q_heads * hd), BF16), jax.ShapeDtypeStruct((t, k_heads * hd), BF16)),
        grid=(t // tm,),
        in_specs=[pl.BlockSpec((tm, (q_heads + k_heads) * hd), lambda i: (i, 0)),
                  pl.BlockSpec((tm, hd), lambda i: (i, 0)),
                  pl.BlockSpec((tm, hd), lambda i: (i, 0)),
                  pl.BlockSpec((1, hd), lambda i: (0, 0)),
                  pl.BlockSpec((1, hd), lambda i: (0, 0))],
        out_specs=(pl.BlockSpec((tm, q_heads * hd), lambda i: (i, 0)),
                   pl.BlockSpec((tm, k_heads * hd), lambda i: (i, 0))),
        compiler_params=_cparams(("arbitrary",)),
        name="attn_qk_norm_rope",
    )(qk, cos, sin, q_norm.reshape(1, hd), k_norm.reshape(1, hd))


def _flash_kernel(q_ref, k_ref, v_ref, o_ref, m_ref, l_ref, acc_ref, *, groups, hd, tk, n_ctx):
    tq = q_ref.shape[0]
    n_kv = k_ref.shape[0]
    q = jnp.concatenate([q_ref[:, g * hd:(g + 1) * hd] for g in range(groups)], axis=0)
    m_ref[...] = jnp.full_like(m_ref, NEG_INF)
    l_ref[...] = jnp.zeros_like(l_ref)
    acc_ref[...] = jnp.zeros_like(acc_ref)

    def chunk(off, size):
        k = k_ref[pl.ds(off, size), :]
        v = v_ref[pl.ds(off, size), :]
        s = _dot_nt(q, k)
        m_prev = m_ref[...]
        m_new = jnp.maximum(m_prev, jnp.max(s, axis=-1, keepdims=True))
        alpha = jnp.exp2(m_prev - m_new)
        p = jnp.exp2(s - jnp.concatenate([m_new] * (size // LANES), axis=1))
        l_ref[...] = alpha * l_ref[...] + jnp.sum(p, axis=-1, keepdims=True)
        acc_ref[...] = alpha * acc_ref[...] + jnp.dot(p.astype(BF16), v, preferred_element_type=F32)
        m_ref[...] = m_new

    is_ctx = pl.program_id(1) < n_ctx // tq

    @pl.when(is_ctx)
    def _():
        chunk(0, n_ctx)

    @pl.when(jnp.logical_not(is_ctx))
    def _():
        def body(j, carry):
            chunk(pl.multiple_of(j * tk, tk), tk)
            return carry
        lax.fori_loop(0, n_kv // tk, body, 0)

    out = acc_ref[...] / l_ref[...]
    for g in range(groups):
        o_ref[:, g * hd:(g + 1) * hd] = out[g * tq:(g + 1) * tq].astype(o_ref.dtype)


def _flash_attention(q, k, v, n_ctx, kv_heads, groups, hd):
    t = q.shape[0]
    assert hd == LANES
    tq = _pick_tile(math.gcd(t, n_ctx), (256, 128))
    tk = _pick_tile(t, (768, 512, 256, 128))
    gw = groups * hd
    return pl.pallas_call(
        functools.partial(_flash_kernel, groups=groups, hd=hd, tk=tk, n_ctx=n_ctx),
        out_shape=jax.ShapeDtypeStruct((t, kv_heads * gw), BF16),
        grid=(kv_heads, t // tq),
        in_specs=[pl.BlockSpec((tq, gw), lambda g, i: (i, g)),
                  pl.BlockSpec((t, hd), lambda g, i: (0, g)),
                  pl.BlockSpec((t, hd), lambda g, i: (0, g))],
        out_specs=pl.BlockSpec((tq, gw), lambda g, i: (i, g)),
        scratch_shapes=[pltpu.VMEM((groups * tq, LANES), F32),
                        pltpu.VMEM((groups * tq, LANES), F32),
                        pltpu.VMEM((groups * tq, hd), F32)],
        compiler_params=_cparams(("arbitrary", "arbitrary")),
        name="flash_attention",
    )(q, k, v)


def _router_kernel(x_ref, g_ref, sh_ref, sc_ref, wr_ref, br_ref, h_ref, ids_ref, wts_ref, cnt_ref, run_ref,
                   *, n_ctx, row0):
    i = pl.program_id(0)
    tm, d = x_ref.shape

    @pl.when(i == 0)
    def _():
        run_ref[...] = jnp.zeros_like(run_ref)

    ib = i + row0 // tm
    h = _rms(x_ref[...]) * g_ref[...]
    h = h * (1.0 + _mod_row(sc_ref, ib, tm, n_ctx)) + _mod_row(sh_ref, ib, tm, n_ctx)
    h_ref[...] = h
    logits = jnp.dot(h, wr_ref[...], preferred_element_type=F32, precision=lax.Precision.HIGHEST) + br_ref[...]
    lane = lax.broadcasted_iota(I32, (tm, LANES), 1)
    big = jnp.int32(LANES)
    gl = jnp.where((lane >= MOE_EXPERTS) & (lane < MOE_EXPERTS + MOE_GROUPS), logits, NEG_INF)
    g_max = jnp.max(gl, axis=-1, keepdims=True)
    g_idx = jnp.min(jnp.where(gl == g_max, lane, big), axis=-1, keepdims=True) - MOE_EXPERTS
    g_w = 1.0 / jnp.sum(jnp.exp(gl - g_max), axis=-1, keepdims=True)
    el = jnp.where((lane < MOE_EXPERTS) & ((lane >> 3) == g_idx), logits, NEG_INF)
    m1 = jnp.max(el, axis=-1, keepdims=True)
    i1 = jnp.min(jnp.where(el == m1, lane, big), axis=-1, keepdims=True)
    el2 = jnp.where(lane == i1, NEG_INF, el)
    m2 = jnp.max(el2, axis=-1, keepdims=True)
    i2 = jnp.min(jnp.where(el2 == m2, lane, big), axis=-1, keepdims=True)
    e_sum = jnp.sum(jnp.exp(el - m1), axis=-1, keepdims=True)
    p1 = 1.0 / e_sum
    p2 = jnp.exp(m2 - m1) / e_sum
    w1 = g_w * (p1 / (p1 + p2))
    w2 = g_w * (p2 / (p1 + p2))
    onehot = jnp.where((lane == i1) | (lane == i2), 1.0, 0.0)
    ri = lax.broadcasted_iota(I32, (tm, tm), 0)
    ci = lax.broadcasted_iota(I32, (tm, tm), 1)
    strict = jnp.where(ci < ri, 1.0, 0.0).astype(BF16)
    before = jnp.dot(strict, onehot.astype(BF16), preferred_element_type=F32) + run_ref[0:1, :]
    r1 = jnp.sum(jnp.where(lane == i1, before, 0.0), axis=-1, keepdims=True).astype(I32)
    r2 = jnp.sum(jnp.where(lane == i2, before, 0.0), axis=-1, keepdims=True).astype(I32)
    ids_ref[...] = jnp.where(lane == 0, i1, jnp.where(lane == 1, i2, jnp.where(lane == 2, r1, jnp.where(lane == 3, r2, 0))))
    wts_ref[...] = jnp.where(lane == 0, w1, jnp.where(lane == 1, w2, 0.0))
    total = run_ref[0:1, :] + jnp.sum(onehot, axis=0, keepdims=True)
    run_ref[...] = jnp.broadcast_to(total, run_ref.shape)
    cnt_ref[...] = jnp.broadcast_to(total, cnt_ref.shape)


def _router(x, g, mod, k_shift, k_scale, w_router, b_router, n_ctx, row0):
    t, d = x.shape
    tm = ROW_TILE
    n = t - row0
    off = row0 // tm
    return pl.pallas_call(
        functools.partial(_router_kernel, n_ctx=n_ctx, row0=row0),
        out_shape=(jax.ShapeDtypeStruct((n, d), F32),
                   jax.ShapeDtypeStruct((n, LANES), I32),
                   jax.ShapeDtypeStruct((n, LANES), F32),
                   jax.ShapeDtypeStruct((8, LANES), F32)),
        grid=(n // tm,),
        in_specs=[pl.BlockSpec((tm, d), lambda i: (i + off, 0)),
                  pl.BlockSpec((1, d), lambda i: (0, 0)),
                  pl.BlockSpec((8, d), lambda i: (0, k_shift)),
                  pl.BlockSpec((8, d), lambda i: (0, k_scale)),
                  pl.BlockSpec((d, LANES), lambda i: (0, 0)),
                  pl.BlockSpec((1, LANES), lambda i: (0, 0))],
        out_specs=(pl.BlockSpec((tm, d), lambda i: (i, 0)),
                   pl.BlockSpec((tm, LANES), lambda i: (i, 0)),
                   pl.BlockSpec((tm, LANES), lambda i: (i, 0)),
                   pl.BlockSpec((8, LANES), lambda i: (0, 0))),
        scratch_shapes=[pltpu.VMEM((8, LANES), F32)],
        compiler_params=_cparams(("arbitrary",)),
        name="moe_router",
    )(x, g.reshape(1, d), mod, mod, w_router, b_router)


def _dispatch_kernel(dest_ref, h_ref, xs_in_ref, xs_ref, sem):
    del xs_in_ref
    i = pl.program_id(0)
    tm = h_ref.shape[0]

    def row_copy(r, k):
        slot = dest_ref[(i * tm + r) * MOE_TOPK + k]
        return pltpu.make_async_copy(h_ref.at[pl.ds(r, 1)], xs_ref.at[pl.ds(slot, 1)], sem)

    def start(r, carry):
        for k in range(MOE_TOPK):
            row_copy(r, k).start()
        return carry

    def wait(r, carry):
        for k in range(MOE_TOPK):
            row_copy(r, k).wait()
        return carry

    lax.fori_loop(0, tm, start, 0)
    lax.fori_loop(0, tm, wait, 0)


def _dispatch(dest, h, xs_init):
    n, w = h.shape
    tm = ROW_TILE
    return pl.pallas_call(
        _dispatch_kernel,
        out_shape=jax.ShapeDtypeStruct(xs_init.shape, xs_init.dtype),
        grid_spec=pltpu.PrefetchScalarGridSpec(
            num_scalar_prefetch=1,
            grid=(n // tm,),
            in_specs=[pl.BlockSpec((tm, w), lambda i, dest: (i, 0)),
                      pl.BlockSpec(memory_space=pl.ANY)],
            out_specs=pl.BlockSpec(memory_space=pl.ANY),
            scratch_shapes=[pltpu.SemaphoreType.DMA(())]),
        input_output_aliases={2: 0},
        compiler_params=_cparams(("arbitrary",)),
        name="moe_dispatch",
    )(dest, h, xs_init)


def _expert_kernel(be_ref, first_ref, next_ref, nb_ref, xs_ref, wg_hbm, wu_hbm, wd_hbm, ys_ref,
                   wg_st, wu_st, wd_st, wg_bf, wu_bf, wd_bf, sem, *, layer):
    b = pl.program_id(0)
    used = b < nb_ref[0]

    def fetch(e):
        return (pltpu.make_async_copy(wg_hbm.at[layer, e], wg_st, sem.at[0]),
                pltpu.make_async_copy(wu_hbm.at[layer, e], wu_st, sem.at[1]),
                pltpu.make_async_copy(wd_hbm.at[layer, e], wd_st, sem.at[2]))

    @pl.when(b == 0)
    def _():
        for cp in fetch(be_ref[0]):
            cp.start()

    @pl.when(used & (first_ref[b] == 1))
    def _():
        for cp in fetch(be_ref[b]):
            cp.wait()
        wg_bf[...] = wg_st[...].astype(BF16)
        wu_bf[...] = wu_st[...].astype(BF16)
        wd_bf[...] = wd_st[...].astype(BF16)

        @pl.when(next_ref[b] >= 0)
        def _():
            for cp in fetch(next_ref[b]):
                cp.start()

    @pl.when(used)
    def _():
        x = xs_ref[...].astype(BF16)
        gate = jnp.dot(x, wg_bf[...], preferred_element_type=F32)
        up = jnp.dot(x, wu_bf[...], preferred_element_type=F32)
        act = (gate * (1.0 / (1.0 + jnp.exp(-gate))) * up).astype(BF16)
        ys_ref[...] = jnp.dot(act, wd_bf[...], preferred_element_type=F32)

    @pl.when(jnp.logical_not(used))
    def _():
        ys_ref[...] = jnp.zeros_like(ys_ref)


def _experts(block_expert, first, next_expert, n_used, xs, w_gate, w_up, w_down, layer):
    n_slots, d = xs.shape
    ff = w_gate.shape[3]
    nb = n_slots // MOE_BLOCK
    row_spec = pl.BlockSpec((MOE_BLOCK, d), lambda b, *_: (b, 0))
    hbm = pl.BlockSpec(memory_space=pl.ANY)
    return pl.pallas_call(
        functools.partial(_expert_kernel, layer=layer),
        out_shape=jax.ShapeDtypeStruct((n_slots, d), F32),
        grid_spec=pltpu.PrefetchScalarGridSpec(
            num_scalar_prefetch=4,
            grid=(nb,),
            in_specs=[row_spec, hbm, hbm, hbm],
            out_specs=row_spec,
            scratch_shapes=[pltpu.VMEM((d, ff), F32), pltpu.VMEM((d, ff), F32), pltpu.VMEM((ff, d), F32),
                            pltpu.VMEM((d, ff), BF16), pltpu.VMEM((d, ff), BF16), pltpu.VMEM((ff, d), BF16),
                            pltpu.SemaphoreType.DMA((3,))]),
        compiler_params=_cparams(("arbitrary",)),
        name="moe_experts",
    )(block_expert, first, next_expert, n_used, xs, w_gate, w_up, w_down)


def _combine_kernel(dest_ref, x_ref, wts_ref, mod_ref, ys_ref, o_ref, buf, sem, *, n_ctx, row0):
    i = pl.program_id(0)
    tm = x_ref.shape[0]

    def row_copy(r, k):
        slot = dest_ref[(i * tm + r) * MOE_TOPK + k]
        return pltpu.make_async_copy(ys_ref.at[pl.ds(slot, 1)], buf.at[k, pl.ds(r, 1)], sem)

    def start(r, carry):
        for k in range(MOE_TOPK):
            row_copy(r, k).start()
        return carry

    def wait(r, carry):
        for k in range(MOE_TOPK):
            row_copy(r, k).wait()
        return carry

    lax.fori_loop(0, tm, start, 0)
    lax.fori_loop(0, tm, wait, 0)
    w = wts_ref[...]
    y = w[:, 0:1] * buf[0] + w[:, 1:2] * buf[1]
    o_ref[...] = x_ref[...] + _mod_row(mod_ref, i + row0 // tm, tm, n_ctx) * y


def _combine(dest, x, wts, mod, k_gate, ys, n_ctx, row0):
    t, d = x.shape
    tm = ROW_TILE
    n = t - row0
    off = row0 // tm
    return pl.pallas_call(
        functools.partial(_combine_kernel, n_ctx=n_ctx, row0=row0),
        out_shape=jax.ShapeDtypeStruct((n, d), F32),
        grid_spec=pltpu.PrefetchScalarGridSpec(
            num_scalar_prefetch=1,
            grid=(n // tm,),
            in_specs=[pl.BlockSpec((tm, d), lambda i, dest: (i + off, 0)),
                      pl.BlockSpec((tm, LANES), lambda i, dest: (i, 0)),
                      pl.BlockSpec((8, d), lambda i, dest: (0, k_gate)),
                      pl.BlockSpec(memory_space=pl.ANY)],
            out_specs=pl.BlockSpec((tm, d), lambda i, dest: (i, 0)),
            scratch_shapes=[pltpu.VMEM((MOE_TOPK, tm, d), F32), pltpu.SemaphoreType.DMA(())]),
        compiler_params=_cparams(("arbitrary",)),
        name="moe_combine",
    )(dest, x, wts, mod, ys)


def _hier_moe(x, g, mod, w_rg, b_rg, w_re, b_re, w_gate, w_up, w_down, layer, n_ctx, row0):
    t, d = x.shape
    n = t - row0
    pad = LANES - MOE_EXPERTS - MOE_GROUPS
    w_router = jnp.concatenate([w_re, w_rg, jnp.zeros((d, pad), F32)], axis=1)
    b_router = jnp.concatenate([b_re, b_rg, jnp.zeros((pad,), F32)]).reshape(1, LANES)
    h, ids, wts, counts = _router(x, g, mod, 3, 4, w_router, b_router, n_ctx, row0)
    counts = counts[0, :MOE_EXPERTS].astype(I32)
    padded = (counts + MOE_BLOCK - 1) // MOE_BLOCK * MOE_BLOCK
    seg_end = jnp.cumsum(padded)
    seg_start = seg_end - padded
    dest = (seg_start[ids[:, :MOE_TOPK]] + ids[:, MOE_TOPK:2 * MOE_TOPK]).reshape(-1)
    n_blocks = (n * MOE_TOPK + MOE_EXPERTS * (MOE_BLOCK - 1)) // MOE_BLOCK + 1
    block_start = jnp.arange(n_blocks, dtype=I32) * MOE_BLOCK
    block_expert = jnp.minimum(jnp.sum((seg_end[None, :] <= block_start[:, None]).astype(I32), axis=1),
                               MOE_EXPERTS - 1).astype(I32)
    n_used = (seg_end[-1:] // MOE_BLOCK).astype(I32)
    block_id = jnp.arange(n_blocks, dtype=I32)
    prev_expert = jnp.concatenate([jnp.full((1,), -1, I32), block_expert[:-1]])
    first = ((block_id < n_used[0]) & (block_expert != prev_expert)).astype(I32)
    eid = jnp.arange(MOE_EXPERTS, dtype=I32)
    owner = jnp.where(padded > 0, eid, MOE_EXPERTS)
    later = jnp.concatenate([lax.cummin(owner[::-1])[::-1][1:], jnp.full((1,), MOE_EXPERTS, I32)])
    next_expert = jnp.where(later < MOE_EXPERTS, later, -1)[block_expert].astype(I32)
    xs = _dispatch(dest, h, jnp.zeros((n_blocks * MOE_BLOCK, d), F32))
    ys = _experts(block_expert, first, next_expert, n_used, xs, w_gate, w_up, w_down, layer)
    return _combine(dest, x, wts, mod, 5, ys, n_ctx, row0)


def _rope_table(pos, dim, base):
    inv = jnp.power(base, -jnp.arange(0, dim, 2, dtype=F32) / dim)
    ang = pos.astype(F32)[:, None] * inv[None, :]
    return jnp.cos(ang), jnp.sin(ang)


def _flipped_positions(t, n_ctx):
    idx = jnp.arange(t, dtype=I32)
    return jnp.where(idx < n_ctx, n_ctx - 1 - idx, n_ctx + (t - 1 - idx))


def _axial_tables(n_ctx, n_lat, hd):
    rows = n_lat // GRID_W
    row = jnp.repeat(jnp.arange(rows, dtype=I32), GRID_W)
    col = jnp.tile(jnp.arange(GRID_W, dtype=I32), rows)
    half = hd // 2
    cr, sr = _rope_table(row, half, ROPE_BASE)
    cc, sc = _rope_table(col, half, ROPE_BASE)
    cos = jnp.concatenate([cr, cr, cc, cc], axis=1)
    sin = jnp.concatenate([-sr, sr, -sc, sc], axis=1)
    cos = jnp.concatenate([jnp.ones((n_ctx, hd), F32), cos], axis=0)
    sin = jnp.concatenate([jnp.zeros((n_ctx, hd), F32), sin], axis=0)
    return cos, sin


def _retention_layer(xa, mod, g, w_in, logit_gamma, gn_w, gn_b, w_out, j, n_ctx):
    t, d = xa.shape
    heads = RET_HEADS
    dk = d // heads
    dv = 2 * dk
    qk_w, v_w = heads * dk, heads * dv
    h = _norm_mod(xa, g, mod, 0, 1, n_ctx)
    pos_f = jnp.arange(t, dtype=I32)
    cf, sf = _rope_table(pos_f, dk, RET_ROPE_BASE)
    cb, sb = _rope_table(_flipped_positions(t, n_ctx), dk, RET_ROPE_BASE)
    tables = (cf, sf, cb, sb)
    q = _matmul_rope(h, w_in, j, 0, qk_w, tables, dk ** -0.5, dk)
    k = _matmul_rope(h, w_in, j, qk_w, qk_w, tables, 1.0, dk)
    vg = _matmul(h, w_in, j, 2 * qk_w, 2 * v_w, BF16)
    log_gamma = jax.nn.log_sigmoid(logit_gamma.astype(F32))
    yf = _retention_scan(log_gamma[0], q, k, vg, 0, n_ctx, heads, dk, dv, reverse=False)
    yb = _retention_scan(log_gamma[1], q, k, vg, 1, n_ctx, heads, dk, dv, reverse=True)
    yn = _ret_post(yf, yb, vg, gn_w, gn_b, heads, dv)
    return _matmul_residual(yn, w_out, j, xa, mod, 2, n_ctx)


def _mlstm_layer(xa, mod, g, w_in, conv_w, b_gate, norm_w, w_out, j, n_ctx):
    t, d = xa.shape
    heads = MLSTM_HEADS
    dk = d // (2 * heads)
    dv = d // heads
    qk_w, v_w = heads * dk, heads * dv
    n_gates = 4 * heads
    h = _norm_mod(xa, g, mod, 0, 1, n_ctx)
    qk_pre = _matmul(h, w_in, j, 0, 2 * qk_w, F32)
    vo = _matmul(h, w_in, j, 2 * qk_w, 2 * v_w, BF16)
    w_gates = jnp.pad(w_in[j, :, 2 * qk_w + 2 * v_w:], ((0, 0), (0, LANES - n_gates)))
    gates = _matmul(h, w_gates[None], 0, 0, LANES, F32)
    bias = jnp.pad(b_gate.astype(F32).reshape(1, n_gates), ((0, 0), (0, LANES - n_gates)))
    qk = _conv_silu(qk_pre, conv_w, n_ctx, qk_w, dk ** -0.5)
    yf = _mlstm_scan(qk, vo, gates, bias, n_ctx, heads, dk, dv, reverse=False)
    yb = _mlstm_scan(qk, vo, gates, bias, n_ctx, heads, dk, dv, reverse=True)
    yn = _mlstm_post(yf, yb, vo, norm_w, heads, dv)
    return _matmul_residual(yn, w_out, j, xa, mod, 2, n_ctx)


def _attention_layer(xa, mod, g, w_in, q_norm, k_norm, w_out, j, n_ctx):
    t, d = xa.shape
    hd = d // ATTN_HEADS
    groups = ATTN_HEADS // ATTN_KV_HEADS
    q_w, kv_w = ATTN_HEADS * hd, ATTN_KV_HEADS * hd
    n_lat = t - n_ctx
    h = _norm_mod(xa, g, mod, 0, 1, n_ctx)
    qk_pre = _matmul(h, w_in, j, 0, q_w + kv_w, F32)
    v = _matmul(h, w_in, j, q_w + kv_w, kv_w, BF16)
    cos, sin = _axial_tables(n_ctx, n_lat, hd)
    q_gain = q_norm * (hd ** -0.5 * math.log2(math.e))
    q, k = _qk_norm_rope(qk_pre, cos, sin, q_gain, k_norm, ATTN_HEADS, ATTN_KV_HEADS, hd)
    o = _flash_attention(q, k, v, n_ctx, ATTN_KV_HEADS, groups, hd)
    return _matmul_residual(o, w_out, j, xa, mod, 2, n_ctx)


def kernel(x, c, ctx, c_ctx, w_ada, b_ada, norm_g, ret_w_in, ret_logit_gamma, ret_gn_w, ret_gn_b, ret_w_out, mlstm_w_in, mlstm_conv_w, mlstm_b_gate, mlstm_norm_w, mlstm_w_out, attn_w_in, attn_q_norm, attn_k_norm, attn_w_out, moe_w_router_group, moe_b_router_group, moe_w_router_expert, moe_b_router_expert, moe_w_gate, moe_w_up, moe_w_down, final_norm_g):
    bsz, n_lat, d = x.shape
    n_ctx = ctx.shape[1]
    depth = w_ada.shape[0]
    assert bsz == 1 and n_ctx % ROW_TILE == 0 and n_lat % ROW_TILE == 0 and n_lat % GRID_W == 0
    xa = jnp.concatenate([ctx[0], x[0]], axis=0)
    s = jnp.stack([jax.nn.silu(c[0]), jax.nn.silu(c_ctx)])
    s8 = jnp.zeros((8, d), F32).at[:2].set(s).astype(BF16)
    mods = _ada_modulation(s8, w_ada, b_ada)
    row0 = 0
    for i in range(depth):
        kind, j = i % 3, i // 3
        mod = mods[i]
        if kind == 0:
            xa = _retention_layer(xa, mod, norm_g[i, 0], ret_w_in, ret_logit_gamma[j], ret_gn_w[j], ret_gn_b[j],
                                  ret_w_out, j, n_ctx)
        elif kind == 1:
            xa = _mlstm_layer(xa, mod, norm_g[i, 0], mlstm_w_in, mlstm_conv_w[j], mlstm_b_gate[j],
                              mlstm_norm_w[j], mlstm_w_out, j, n_ctx)
        else:
            xa = _attention_layer(xa, mod, norm_g[i, 0], attn_w_in, attn_q_norm[j], attn_k_norm[j],
                                  attn_w_out, j, n_ctx)
        xa = _hier_moe(xa, norm_g[i, 1], mod, moe_w_router_group[i], moe_b_router_group[i],
                       moe_w_router_expert[i], moe_b_router_expert[i], moe_w_gate, moe_w_up, moe_w_down, i,
                       n_ctx, row0)
    return _final_norm(xa, final_norm_g, n_ctx)[None]
```

```python
import functools
import math

import jax
import jax.numpy as jnp
import numpy as np
from jax import lax
from jax.experimental import pallas as pl
from jax.experimental.pallas import tpu as pltpu

F32 = jnp.float32
BF16 = jnp.bfloat16
I32 = jnp.int32
U32 = jnp.uint32

LANES = 128
CHUNK = 128
NORM_EPS = 1e-6
GRID_W = 64
RET_HEADS = 8
RET_ROPE_BASE = 10000.0
MLSTM_HEADS = 8
GATE_SOFTCAP = 15.0
ATTN_HEADS = 16
ATTN_KV_HEADS = 8
ROPE_BASE = 10000.0
MOE_GROUPS = 4
MOE_PER_GROUP = 8
MOE_EXPERTS = MOE_GROUPS * MOE_PER_GROUP
MOE_TOPK = 2
MOE_BLOCK = 256
ROW_TILE = 256
VMEM_LIMIT = 56 * 1024 * 1024
NEG_INF = float("-inf")


def _cparams(sem, vmem=VMEM_LIMIT):
    return pltpu.CompilerParams(dimension_semantics=sem, vmem_limit_bytes=vmem)


def _pick_tile(total, candidates):
    for c in candidates:
        if total % c == 0:
            return c
    raise ValueError(f"no tile for {total} in {candidates}")


def _mod_row(mod_ref, i, tile, n_ctx):
    sel = jnp.where(i < n_ctx // tile, 1, 0)
    return mod_ref[pl.ds(sel, 1), :]


def _ada_kernel(s_ref, w_ref, b_ref, o_ref):
    w = w_ref[0].astype(BF16)
    o_ref[0] = jnp.dot(s_ref[...], w, preferred_element_type=F32) + b_ref[0]


def _ada_modulation(s8, w_ada, b_ada):
    depth, d, n = w_ada.shape
    tn = _pick_tile(n, (1024, 512, 256, 128))
    return pl.pallas_call(
        _ada_kernel,
        out_shape=jax.ShapeDtypeStruct((depth, 8, n), F32),
        grid=(depth, n // tn),
        in_specs=[pl.BlockSpec((8, d), lambda l, j: (0, 0)),
                  pl.BlockSpec((1, d, tn), lambda l, j: (l, 0, j)),
                  pl.BlockSpec((1, 1, tn), lambda l, j: (l, 0, j))],
        out_specs=pl.BlockSpec((1, 8, tn), lambda l, j: (l, 0, j)),
        compiler_params=_cparams(("arbitrary", "arbitrary")),
        name="ada_modulation",
    )(s8, w_ada, b_ada.reshape(depth, 1, n))


def _rms(x):
    return x * lax.rsqrt(jnp.mean(x * x, axis=-1, keepdims=True) + NORM_EPS)


def _norm_mod_kernel(x_ref, g_ref, sh_ref, sc_ref, o_ref, *, n_ctx):
    i = pl.program_id(0)
    tm = x_ref.shape[0]
    y = _rms(x_ref[...]) * g_ref[...]
    y = y * (1.0 + _mod_row(sc_ref, i, tm, n_ctx)) + _mod_row(sh_ref, i, tm, n_ctx)
    o_ref[...] = y.astype(o_ref.dtype)


def _norm_mod(x, g, mod, k_shift, k_scale, n_ctx):
    t, d = x.shape
    tm = ROW_TILE
    return pl.pallas_call(
        functools.partial(_norm_mod_kernel, n_ctx=n_ctx),
        out_shape=jax.ShapeDtypeStruct((t, d), BF16),
        grid=(t // tm,),
        in_specs=[pl.BlockSpec((tm, d), lambda i: (i, 0)),
                  pl.BlockSpec((1, d), lambda i: (0, 0)),
                  pl.BlockSpec((8, d), lambda i: (0, k_shift)),
                  pl.BlockSpec((8, d), lambda i: (0, k_scale))],
        out_specs=pl.BlockSpec((tm, d), lambda i: (i, 0)),
        compiler_params=_cparams(("arbitrary",)),
        name="norm_mod",
    )(x, g.reshape(1, d), mod, mod)


def _final_norm_kernel(x_ref, g_ref, o_ref):
    o_ref[...] = _rms(x_ref[...]) * g_ref[...]


def _final_norm(x, g, n_ctx):
    t, d = x.shape
    tm = ROW_TILE
    off = n_ctx // tm
    return pl.pallas_call(
        _final_norm_kernel,
        out_shape=jax.ShapeDtypeStruct((t - n_ctx, d), F32),
        grid=((t - n_ctx) // tm,),
        in_specs=[pl.BlockSpec((tm, d), lambda i: (i + off, 0)),
                  pl.BlockSpec((1, d), lambda i: (0, 0))],
        out_specs=pl.BlockSpec((tm, d), lambda i: (i, 0)),
        compiler_params=_cparams(("arbitrary",)),
        name="final_norm",
    )(x, g.reshape(1, d))


def _cast_weight_once(w_ref, wbf_ref):
    @pl.when(pl.program_id(1) == 0)
    def _():
        wbf_ref[...] = w_ref[...].astype(BF16)


def _mm_plain_kernel(a_ref, w_ref, o_ref, wbf_ref, *, scale):
    _cast_weight_once(w_ref, wbf_ref)
    acc = jnp.dot(a_ref[...], wbf_ref[...], preferred_element_type=F32)
    if scale != 1.0:
        acc = acc * scale
    o_ref[...] = acc.astype(o_ref.dtype)


def _mm_residual_kernel(a_ref, w_ref, x_ref, mod_ref, o_ref, wbf_ref, *, n_ctx):
    _cast_weight_once(w_ref, wbf_ref)
    tm = a_ref.shape[0]
    acc = jnp.dot(a_ref[...], wbf_ref[...], preferred_element_type=F32)
    row = pl.program_id(1) * tm + lax.broadcasted_iota(I32, (tm, 1), 0)
    gate = jnp.where(row < n_ctx, mod_ref[1:2, :], mod_ref[0:1, :])
    o_ref[...] = x_ref[...] + gate * acc


def _mm_rope_kernel(a_ref, w_ref, cf_ref, sf_ref, cb_ref, sb_ref, o_ref, wbf_ref, *, scale, head_dim):
    _cast_weight_once(w_ref, wbf_ref)
    acc = jnp.dot(a_ref[...], wbf_ref[...], preferred_element_type=F32)
    if scale != 1.0:
        acc = acc * scale
    half = head_dim // 2
    for d, (c_ref, s_ref) in enumerate(((cf_ref, sf_ref), (cb_ref, sb_ref))):
        c, s = c_ref[...], s_ref[...]
        for h in range(acc.shape[1] // head_dim):
            x1 = acc[:, h * head_dim:h * head_dim + half]
            x2 = acc[:, h * head_dim + half:(h + 1) * head_dim]
            o_ref[d, :, h * head_dim:h * head_dim + half] = (x1 * c - x2 * s).astype(o_ref.dtype)
            o_ref[d, :, h * head_dim + half:(h + 1) * head_dim] = (x1 * s + x2 * c).astype(o_ref.dtype)


def _mm_tiles(m, k, n):
    tm = _pick_tile(m, (768, 512, 384, 256, 128))
    tn_cap = 1024 if k <= 2048 else 512
    tn = _pick_tile(n, tuple(c for c in (1024, 512, 256, 128) if c <= tn_cap))
    return tm, tn


def _matmul(a, w, layer, col0, n, out_dtype, scale=1.0):
    m, k = a.shape
    tm, tn = _mm_tiles(m, k, n)
    assert col0 % tn == 0
    c0 = col0 // tn
    return pl.pallas_call(
        functools.partial(_mm_plain_kernel, scale=scale),
        out_shape=jax.ShapeDtypeStruct((m, n), out_dtype),
        grid=(n // tn, m // tm),
        in_specs=[pl.BlockSpec((tm, k), lambda j, i: (i, 0)),
                  pl.BlockSpec((None, k, tn), lambda j, i: (layer, 0, j + c0))],
        out_specs=pl.BlockSpec((tm, tn), lambda j, i: (i, j)),
        scratch_shapes=[pltpu.VMEM((k, tn), BF16)],
        compiler_params=_cparams(("arbitrary", "arbitrary")),
        name="matmul",
    )(a, w)


def _matmul_residual(a, w, layer, x, mod, k_gate, n_ctx):
    m, k = a.shape
    n = w.shape[2]
    tm, tn = _mm_tiles(m, k, n)
    kb = k_gate * (n // tn)
    return pl.pallas_call(
        functools.partial(_mm_residual_kernel, n_ctx=n_ctx),
        out_shape=jax.ShapeDtypeStruct((m, n), F32),
        grid=(n // tn, m // tm),
        in_specs=[pl.BlockSpec((tm, k), lambda j, i: (i, 0)),
                  pl.BlockSpec((None, k, tn), lambda j, i: (layer, 0, j)),
                  pl.BlockSpec((tm, tn), lambda j, i: (i, j)),
                  pl.BlockSpec((8, tn), lambda j, i: (0, kb + j))],
        out_specs=pl.BlockSpec((tm, tn), lambda j, i: (i, j)),
        scratch_shapes=[pltpu.VMEM((k, tn), BF16)],
        compiler_params=_cparams(("arbitrary", "arbitrary")),
        name="matmul_residual",
    )(a, w, x, mod)


def _matmul_rope(a, w, layer, col0, n, tables, scale, head_dim):
    m, k = a.shape
    tm, tn = _mm_tiles(m, k, n)
    assert col0 % tn == 0 and tn % head_dim == 0
    c0 = col0 // tn
    half = head_dim // 2
    tab_spec = pl.BlockSpec((tm, half), lambda j, i: (i, 0))
    return pl.pallas_call(
        functools.partial(_mm_rope_kernel, scale=scale, head_dim=head_dim),
        out_shape=jax.ShapeDtypeStruct((2, m, n), BF16),
        grid=(n // tn, m // tm),
        in_specs=[pl.BlockSpec((tm, k), lambda j, i: (i, 0)),
                  pl.BlockSpec((None, k, tn), lambda j, i: (layer, 0, j + c0)),
                  tab_spec, tab_spec, tab_spec, tab_spec],
        out_specs=pl.BlockSpec((2, tm, tn), lambda j, i: (0, i, j)),
        scratch_shapes=[pltpu.VMEM((k, tn), BF16)],
        compiler_params=_cparams(("arbitrary", "arbitrary")),
        name="matmul_rope",
    )(a, w, *tables)


def _chunk_index(c, n_chunks, ctx_chunks, reverse):
    if not reverse:
        return c
    return jnp.where(c < ctx_chunks, ctx_chunks - 1 - c, n_chunks - 1 - (c - ctx_chunks))


def _dot_nt(a, b):
    return lax.dot_general(a, b, (((1,), (1,)), ((), ())), preferred_element_type=F32)


def _dot_tn(a, b):
    return lax.dot_general(a, b, (((0,), (0,)), ((), ())), preferred_element_type=F32)


def _retention_kernel(lg_ref, q_ref, k_ref, v_ref, o_ref, s_ref, *, reverse, heads, dk, dv):
    c = pl.program_id(0)

    @pl.when(c == 0)
    def _():
        s_ref[...] = jnp.zeros_like(s_ref)

    n = q_ref.shape[0]
    ri = lax.broadcasted_iota(I32, (n, n), 0)
    ci = lax.broadcasted_iota(I32, (n, n), 1)
    diff = (ci - ri) if reverse else (ri - ci)
    dmask = diff >= 0
    dist = jnp.where(dmask, diff, 0).astype(F32)
    r1 = lax.broadcasted_iota(I32, (n, 1), 0)
    pos = ((n - 1 - r1) if reverse else r1).astype(F32)
    for h in range(heads):
        lg = jnp.full((1, 1), lg_ref[h], F32)
        decay = jnp.where(dmask, jnp.exp(dist * lg), 0.0)
        q_dec = jnp.exp((pos + 1.0) * lg)
        k_dec = jnp.exp((n - 1.0 - pos) * lg)
        c_dec = jnp.exp(float(n) * lg)
        q = q_ref[:, h * dk:(h + 1) * dk]
        k = k_ref[:, h * dk:(h + 1) * dk]
        v = v_ref[:, h * dv:(h + 1) * dv]
        s_prev = s_ref[h]
        scores = (_dot_nt(q, k) * decay).astype(BF16)
        inner = jnp.dot(scores, v, preferred_element_type=F32)
        cross = jnp.dot(q, s_prev.astype(BF16), preferred_element_type=F32)
        o_ref[:, h * dv:(h + 1) * dv] = inner + q_dec * cross
        kd = (k.astype(F32) * k_dec).astype(BF16)
        s_ref[h] = c_dec * s_prev + _dot_tn(kd, v)


def _retention_scan(log_gamma, q, k, vg, d_idx, n_ctx, heads, dk, dv, reverse):
    t = q.shape[1]
    nc, cc = t // CHUNK, n_ctx // CHUNK
    cm = lambda c: _chunk_index(c, nc, cc, reverse)
    return pl.pallas_call(
        functools.partial(_retention_kernel, reverse=reverse, heads=heads, dk=dk, dv=dv),
        out_shape=jax.ShapeDtypeStruct((t, heads * dv), F32),
        grid=(nc,),
        in_specs=[pl.BlockSpec(memory_space=pltpu.SMEM),
                  pl.BlockSpec((None, CHUNK, heads * dk), lambda c: (d_idx, cm(c), 0)),
                  pl.BlockSpec((None, CHUNK, heads * dk), lambda c: (d_idx, cm(c), 0)),
                  pl.BlockSpec((CHUNK, heads * dv), lambda c: (cm(c), 0))],
        out_specs=pl.BlockSpec((CHUNK, heads * dv), lambda c: (cm(c), 0)),
        scratch_shapes=[pltpu.VMEM((heads, dk, dv), F32)],
        compiler_params=_cparams(("arbitrary",)),
        name="retention_scan_rev" if reverse else "retention_scan_fwd",
    )(log_gamma, q, k, vg)


def _ret_post_kernel(yf_ref, yb_ref, g_ref, w_ref, b_ref, o_ref, *, heads, dv):
    for h in range(heads):
        sl = slice(h * dv, (h + 1) * dv)
        y = yf_ref[:, sl] + yb_ref[:, sl]
        mu = jnp.mean(y, axis=-1, keepdims=True)
        yc = y - mu
        var = jnp.mean(yc * yc, axis=-1, keepdims=True)
        yn = yc * lax.rsqrt(var + NORM_EPS)
        g = g_ref[:, sl].astype(F32)
        silu = g * (1.0 / (1.0 + jnp.exp(-g)))
        o_ref[:, sl] = ((yn * w_ref[:, sl] + b_ref[:, sl]) * silu).astype(o_ref.dtype)


def _ret_post(yf, yb, vg, gn_w, gn_b, heads, dv):
    t, vw = yf.shape
    tm = ROW_TILE
    return pl.pallas_call(
        functools.partial(_ret_post_kernel, heads=heads, dv=dv),
        out_shape=jax.ShapeDtypeStruct((t, vw), BF16),
        grid=(t // tm,),
        in_specs=[pl.BlockSpec((tm, vw), lambda i: (i, 0)),
                  pl.BlockSpec((tm, vw), lambda i: (i, 0)),
                  pl.BlockSpec((tm, vw), lambda i: (i, 1)),
                  pl.BlockSpec((1, vw), lambda i: (0, 0)),
                  pl.BlockSpec((1, vw), lambda i: (0, 0))],
        out_specs=pl.BlockSpec((tm, vw), lambda i: (i, 0)),
        compiler_params=_cparams(("arbitrary",)),
        name="retention_post",
    )(yf, yb, vg, gn_w.reshape(1, vw), gn_b.reshape(1, vw))


def _conv_silu_kernel(x_ref, prev_ref, next_ref, w_ref, o_ref, *, n_ctx, t_total, q_cols, q_scale):
    i = pl.program_id(0)
    tm = x_ref.shape[0]
    x = x_ref[...]
    row = lax.broadcasted_iota(I32, (tm, 1), 0)
    grow = i * tm + row
    halo = prev_ref.shape[0]
    x_prev = jnp.where(row == 0, prev_ref[halo - 1:halo, :], pltpu.roll(x, 1, axis=0))
    x_prev = jnp.where((grow == 0) | (grow == n_ctx), 0.0, x_prev)
    x_next = jnp.where(row == tm - 1, next_ref[0:1, :], pltpu.roll(x, tm - 1, axis=0))
    x_next = jnp.where((grow == n_ctx - 1) | (grow == t_total - 1), 0.0, x_next)
    y = x_prev * w_ref[0:1, :] + x * w_ref[1:2, :] + x_next * w_ref[2:3, :]
    y = y * (1.0 / (1.0 + jnp.exp(-y)))
    o_ref[:, :q_cols] = (y[:, :q_cols] * q_scale).astype(o_ref.dtype)
    o_ref[:, q_cols:] = y[:, q_cols:].astype(o_ref.dtype)


def _conv_silu(x, conv_w, n_ctx, q_cols, q_scale):
    t, c = x.shape
    tm = ROW_TILE
    halo = 8
    r = tm // halo
    last = t // halo - 1
    w8 = jnp.zeros((8, c), F32).at[:3].set(conv_w)
    return pl.pallas_call(
        functools.partial(_conv_silu_kernel, n_ctx=n_ctx, t_total=t, q_cols=q_cols, q_scale=q_scale),
        out_shape=jax.ShapeDtypeStruct((t, c), BF16),
        grid=(t // tm,),
        in_specs=[pl.BlockSpec((tm, c), lambda i: (i, 0)),
                  pl.BlockSpec((halo, c), lambda i: (jnp.maximum(i * r - 1, 0), 0)),
                  pl.BlockSpec((halo, c), lambda i: (jnp.minimum((i + 1) * r, last), 0)),
                  pl.BlockSpec((8, c), lambda i: (0, 0))],
        out_specs=pl.BlockSpec((tm, c), lambda i: (i, 0)),
        compiler_params=_cparams(("arbitrary",)),
        name="mlstm_conv_silu",
    )(x, x, x, w8)


def _mlstm_kernel(qk_ref, v_ref, gt_ref, bias_ref, o_ref, c_ref, n_ref, m_ref, *, reverse, heads, dk, dv, d_idx):
    c = pl.program_id(0)

    @pl.when(c == 0)
    def _():
        c_ref[...] = jnp.zeros_like(c_ref)
        n_ref[...] = jnp.zeros_like(n_ref)
        m_ref[...] = jnp.zeros_like(m_ref)

    n = qk_ref.shape[0]
    gates = gt_ref[...] + bias_ref[...]
    i_all = GATE_SOFTCAP * jnp.tanh(gates * (1.0 / GATE_SOFTCAP))
    f_all = -(jnp.maximum(-gates, 0.0) + jnp.log1p(jnp.exp(-jnp.abs(gates))))
    ri = lax.broadcasted_iota(I32, (n, n), 0)
    ci = lax.broadcasted_iota(I32, (n, n), 1)
    mask = (ci >= ri) if reverse else (ci <= ri)
    tri = jnp.where(mask, 1.0, 0.0).astype(F32)
    b_all = jnp.dot(tri, f_all, preferred_element_type=F32, precision=lax.Precision.HIGHEST)
    b_all_t = b_all.T
    i_all_t = i_all.T
    last = 0 if reverse else n - 1
    qk_w = heads * dk
    for h in range(heads):
        ic = d_idx * 2 * heads + h
        fc = ic + heads
        b_col = b_all[:, fc:fc + 1]
        b_row = b_all_t[fc:fc + 1, :]
        i_col = i_all[:, ic:ic + 1]
        i_row = i_all_t[ic:ic + 1, :]
        m_prev = m_ref[h][:, 0:1]
        q = qk_ref[:, h * dk:(h + 1) * dk]
        k = qk_ref[:, qk_w + h * dk:qk_w + (h + 1) * dk]
        v = v_ref[:, h * dv:(h + 1) * dv]
        c_prev = c_ref[h]
        n_prev = n_ref[h]
        d_log = jnp.where(mask, b_col - b_row + i_row, NEG_INF)
        m_t = jnp.maximum(b_col + m_prev, jnp.max(d_log, axis=-1, keepdims=True))
        w = jnp.exp(d_log - m_t)
        s = _dot_nt(q, k) * w
        inter = jnp.exp(b_col + m_prev - m_t)
        num = jnp.dot(s.astype(BF16), v, preferred_element_type=F32) + inter * jnp.dot(
            q, c_prev.astype(BF16), preferred_element_type=F32)
        den = jnp.sum(s, axis=-1, keepdims=True) + inter * jnp.sum(q.astype(F32) * n_prev, axis=-1, keepdims=True)
        o_ref[:, h * dv:(h + 1) * dv] = num / jnp.maximum(jnp.abs(den), jnp.exp(-m_t))
        b_last = b_col[last:last + 1, :]
        g = b_last - b_col + i_col
        m_new = jnp.maximum(b_last + m_prev, jnp.max(g, axis=0, keepdims=True))
        wk = jnp.exp(g - m_new) * k.astype(F32)
        dec = jnp.exp(b_last + m_prev - m_new)
        c_ref[h] = dec * c_prev + _dot_tn(wk.astype(BF16), v)
        n_ref[h] = dec * n_prev + jnp.sum(wk, axis=0, keepdims=True)
        m_ref[h] = jnp.broadcast_to(m_new, m_ref.shape[1:])


def _mlstm_scan(qk, vo, gates, bias, n_ctx, heads, dk, dv, reverse):
    t = qk.shape[0]
    nc, cc = t // CHUNK, n_ctx // CHUNK
    cm = lambda c: _chunk_index(c, nc, cc, reverse)
    return pl.pallas_call(
        functools.partial(_mlstm_kernel, reverse=reverse, heads=heads, dk=dk, dv=dv, d_idx=1 if reverse else 0),
        out_shape=jax.ShapeDtypeStruct((t, heads * dv), F32),
        grid=(nc,),
        in_specs=[pl.BlockSpec((CHUNK, 2 * heads * dk), lambda c: (cm(c), 0)),
                  pl.BlockSpec((CHUNK, heads * dv), lambda c: (cm(c), 0)),
                  pl.BlockSpec((CHUNK, LANES), lambda c: (cm(c), 0)),
                  pl.BlockSpec((1, LANES), lambda c: (0, 0))],
        out_specs=pl.BlockSpec((CHUNK, heads * dv), lambda c: (cm(c), 0)),
        scratch_shapes=[pltpu.VMEM((heads, dk, dv), F32),
                        pltpu.VMEM((heads, 1, dk), F32),
                        pltpu.VMEM((heads, 1, LANES), F32)],
        compiler_params=_cparams(("arbitrary",)),
        name="mlstm_scan_rev" if reverse else "mlstm_scan_fwd",
    )(qk, vo, gates, bias)


def _mlstm_post_kernel(yf_ref, yb_ref, o_ref_in, w_ref, o_ref, *, heads, dv):
    for h in range(heads):
        sl = slice(h * dv, (h + 1) * dv)
        y = _rms(yf_ref[:, sl] + yb_ref[:, sl])
        o = o_ref_in[:, sl].astype(F32)
        o_ref[:, sl] = (y * w_ref[:, sl] * (1.0 / (1.0 + jnp.exp(-o)))).astype(o_ref.dtype)


def _mlstm_post(yf, yb, vo, norm_w, heads, dv):
    t, vw = yf.shape
    tm = ROW_TILE
    return pl.pallas_call(
        functools.partial(_mlstm_post_kernel, heads=heads, dv=dv),
        out_shape=jax.ShapeDtypeStruct((t, vw), BF16),
        grid=(t // tm,),
        in_specs=[pl.BlockSpec((tm, vw), lambda i: (i, 0)),
                  pl.BlockSpec((tm, vw), lambda i: (i, 0)),
                  pl.BlockSpec((tm, vw), lambda i: (i, 1)),
                  pl.BlockSpec((1, vw), lambda i: (0, 0))],
        out_specs=pl.BlockSpec((tm, vw), lambda i: (i, 0)),
        compiler_params=_cparams(("arbitrary",)),
        name="mlstm_post",
    )(yf, yb, vo, norm_w.reshape(1, vw))


def _qk_norm_rope_kernel(x_ref, cos_ref, sin_ref, qn_ref, kn_ref, q_ref, k_ref, *, q_heads, k_heads, hd):
    cos, sin = cos_ref[...], sin_ref[...]
    lane = lax.broadcasted_iota(I32, (1, hd), 1)
    first = (lane % (hd // 2)) < (hd // 4)
    for h in range(q_heads + k_heads):
        x = x_ref[:, h * hd:(h + 1) * hd]
        w = qn_ref[...] if h < q_heads else kn_ref[...]
        xh = _rms(x) * w
        partner = jnp.where(first, pltpu.roll(xh, hd - hd // 4, axis=1), pltpu.roll(xh, hd // 4, axis=1))
        y = (xh * cos + partner * sin).astype(q_ref.dtype)
        if h < q_heads:
            q_ref[:, h * hd:(h + 1) * hd] = y
        else:
            k_ref[:, (h - q_heads) * hd:(h - q_heads + 1) * hd] = y


def _qk_norm_rope(qk, cos, sin, q_norm, k_norm, q_heads, k_heads, hd):
    t = qk.shape[0]
    tm = ROW_TILE
    return pl.pallas_call(
        functools.partial(_qk_norm_rope_kernel, q_heads=q_heads, k_heads=k_heads, hd=hd),
        out_shape=(jax.ShapeDtypeStruct((t, q_heads * hd), BF16), jax.ShapeDtypeStruct((t, k_heads * hd), BF16)),
        grid=(t // tm,),
        in_specs=[pl.BlockSpec((tm, (q_heads + k_heads) * hd), lambda i: (i, 0)),
                  pl.BlockSpec((tm, hd), lambda i: (i, 0)),
                  pl.BlockSpec((tm, hd), lambda i: (i, 0)),
                  pl.BlockSpec((1, hd), lambda i: (0, 0)),
                  pl.BlockSpec((1, hd), lambda i: (0, 0))],
        out_specs=(pl.BlockSpec((tm, q_heads * hd), lambda i: (i, 0)),
                   pl.BlockSpec((tm, k_heads * hd), lambda i: (i, 0))),
        compiler_params=_cparams(("arbitrary",)),
        name="attn_qk_norm_rope",
    )(qk, cos, sin, q_norm.reshape(1, hd), k_norm.reshape(1, hd))


def _flash_kernel(q_ref, k_ref, v_ref, o_ref, m_ref, l_ref, acc_ref, sa_ref, sb_ref, *, groups, hd, tk, n_ctx):
    tq = q_ref.shape[0]
    n_kv = k_ref.shape[0]
    n = n_kv // tk
    q = jnp.concatenate([q_ref[:, g * hd:(g + 1) * hd] for g in range(groups)], axis=0)
    m_ref[...] = jnp.full_like(m_ref, NEG_INF)
    l_ref[...] = jnp.zeros_like(l_ref)
    acc_ref[...] = jnp.zeros_like(acc_ref)

    def scores(off, size):
        return _dot_nt(q, k_ref[pl.ds(off, size), :])

    def absorb(s, off, size):
        m_prev = m_ref[...]
        m_new = jnp.maximum(m_prev, jnp.max(s, axis=-1, keepdims=True))
        alpha = jnp.exp2(m_prev - m_new)
        p = jnp.exp2(s - jnp.concatenate([m_new] * (size // LANES), axis=1))
        l_ref[...] = alpha * l_ref[...] + jnp.sum(p, axis=-1, keepdims=True)
        acc_ref[...] = alpha * acc_ref[...] + jnp.dot(p.astype(BF16), v_ref[pl.ds(off, size), :],
                                                      preferred_element_type=F32)
        m_ref[...] = m_new

    is_ctx = pl.program_id(1) < n_ctx // tq

    @pl.when(is_ctx)
    def _():
        absorb(scores(0, n_ctx), 0, n_ctx)

    @pl.when(jnp.logical_not(is_ctx))
    def _():
        sa_ref[...] = scores(0, tk)

        def pair(jj, carry):
            off_a = pl.multiple_of(2 * jj * tk, tk)
            sb_ref[...] = scores(off_a + tk, tk)
            absorb(sa_ref[...], off_a, tk)
            sa_ref[...] = scores(off_a + 2 * tk, tk)
            absorb(sb_ref[...], off_a + tk, tk)
            return carry

        n_pairs = (n - 1) // 2
        lax.fori_loop(0, n_pairs, pair, 0)
        if n % 2 == 0:
            sb_ref[...] = scores((n - 1) * tk, tk)
            absorb(sa_ref[...], (n - 2) * tk, tk)
            absorb(sb_ref[...], (n - 1) * tk, tk)
        else:
            absorb(sa_ref[...], (n - 1) * tk, tk)

    out = acc_ref[...] / l_ref[...]
    for g in range(groups):
        o_ref[:, g * hd:(g + 1) * hd] = out[g * tq:(g + 1) * tq].astype(o_ref.dtype)


def _flash_attention(q, k, v, n_ctx, kv_heads, groups, hd):
    t = q.shape[0]
    assert hd == LANES
    tq = _pick_tile(math.gcd(t, n_ctx), (256, 128))
    tk = _pick_tile(t, (768, 512, 256, 128))
    gw = groups * hd
    return pl.pallas_call(
        functools.partial(_flash_kernel, groups=groups, hd=hd, tk=tk, n_ctx=n_ctx),
        out_shape=jax.ShapeDtypeStruct((t, kv_heads * gw), BF16),
        grid=(kv_heads, t // tq),
        in_specs=[pl.BlockSpec((tq, gw), lambda g, i: (i, g)),
                  pl.BlockSpec((t, hd), lambda g, i: (0, g)),
                  pl.BlockSpec((t, hd), lambda g, i: (0, g))],
        out_specs=pl.BlockSpec((tq, gw), lambda g, i: (i, g)),
        scratch_shapes=[pltpu.VMEM((groups * tq, LANES), F32),
                        pltpu.VMEM((groups * tq, LANES), F32),
                        pltpu.VMEM((groups * tq, hd), F32),
                        pltpu.VMEM((groups * tq, tk), F32),
                        pltpu.VMEM((groups * tq, tk), F32)],
        compiler_params=_cparams(("arbitrary", "arbitrary")),
        name="flash_attention",
    )(q, k, v)


def _router_kernel(x_ref, g_ref, sh_ref, sc_ref, wr_ref, br_ref, h_ref, ids_ref, wts_ref, cnt_ref, run_ref,
                   *, n_ctx, row0):
    i = pl.program_id(0)
    tm, d = x_ref.shape

    @pl.when(i == 0)
    def _():
        run_ref[...] = jnp.zeros_like(run_ref)

    ib = i + row0 // tm
    h = _rms(x_ref[...]) * g_ref[...]
    h = h * (1.0 + _mod_row(sc_ref, ib, tm, n_ctx)) + _mod_row(sh_ref, ib, tm, n_ctx)
    h_ref[...] = h
    h_hi = h.astype(BF16)
    h_lo = (h - h_hi.astype(F32)).astype(BF16)
    hw = jnp.dot(h_hi, wr_ref[...], preferred_element_type=F32)
    lw = jnp.dot(h_lo, wr_ref[:, :LANES], preferred_element_type=F32)
    logits = hw[:, :LANES] + (hw[:, LANES:] + lw) + br_ref[...]
    lane = lax.broadcasted_iota(I32, (tm, LANES), 1)
    big = jnp.int32(LANES)
    gl = jnp.where((lane >= MOE_EXPERTS) & (lane < MOE_EXPERTS + MOE_GROUPS), logits, NEG_INF)
    g_max = jnp.max(gl, axis=-1, keepdims=True)
    g_idx = jnp.min(jnp.where(gl == g_max, lane, big), axis=-1, keepdims=True) - MOE_EXPERTS
    g_w = 1.0 / jnp.sum(jnp.exp(gl - g_max), axis=-1, keepdims=True)
    el = jnp.where((lane < MOE_EXPERTS) & ((lane >> 3) == g_idx), logits, NEG_INF)
    m1 = jnp.max(el, axis=-1, keepdims=True)
    i1 = jnp.min(jnp.where(el == m1, lane, big), axis=-1, keepdims=True)
    el2 = jnp.where(lane == i1, NEG_INF, el)
    m2 = jnp.max(el2, axis=-1, keepdims=True)
    i2 = jnp.min(jnp.where(el2 == m2, lane, big), axis=-1, keepdims=True)
    e_sum = jnp.sum(jnp.exp(el - m1), axis=-1, keepdims=True)
    p1 = 1.0 / e_sum
    p2 = jnp.exp(m2 - m1) / e_sum
    w1 = g_w * (p1 / (p1 + p2))
    w2 = g_w * (p2 / (p1 + p2))
    onehot = jnp.where((lane == i1) | (lane == i2), 1.0, 0.0)
    ri = lax.broadcasted_iota(I32, (tm, tm), 0)
    ci = lax.broadcasted_iota(I32, (tm, tm), 1)
    strict = jnp.where(ci < ri, 1.0, 0.0).astype(BF16)
    before = jnp.dot(strict, onehot.astype(BF16), preferred_element_type=F32) + run_ref[0:1, :]
    r1 = jnp.sum(jnp.where(lane == i1, before, 0.0), axis=-1, keepdims=True).astype(I32)
    r2 = jnp.sum(jnp.where(lane == i2, before, 0.0), axis=-1, keepdims=True).astype(I32)
    ids_ref[...] = jnp.where(lane == 0, i1, jnp.where(lane == 1, i2, jnp.where(lane == 2, r1, jnp.where(lane == 3, r2, 0))))
    wts_ref[...] = jnp.where(lane == 0, w1, jnp.where(lane == 1, w2, 0.0))
    total = run_ref[0:1, :] + jnp.sum(onehot, axis=0, keepdims=True)
    run_ref[...] = jnp.broadcast_to(total, run_ref.shape)
    cnt_ref[...] = jnp.broadcast_to(total, cnt_ref.shape)


def _router(x, g, mod, k_shift, k_scale, w_router, b_router, n_ctx, row0):
    t, d = x.shape
    tm = ROW_TILE
    n = t - row0
    off = row0 // tm
    return pl.pallas_call(
        functools.partial(_router_kernel, n_ctx=n_ctx, row0=row0),
        out_shape=(jax.ShapeDtypeStruct((n, d), F32),
                   jax.ShapeDtypeStruct((n, LANES), I32),
                   jax.ShapeDtypeStruct((n, LANES), F32),
                   jax.ShapeDtypeStruct((8, LANES), F32)),
        grid=(n // tm,),
        in_specs=[pl.BlockSpec((tm, d), lambda i: (i + off, 0)),
                  pl.BlockSpec((1, d), lambda i: (0, 0)),
                  pl.BlockSpec((8, d), lambda i: (0, k_shift)),
                  pl.BlockSpec((8, d), lambda i: (0, k_scale)),
                  pl.BlockSpec((d, 2 * LANES), lambda i: (0, 0)),
                  pl.BlockSpec((1, LANES), lambda i: (0, 0))],
        out_specs=(pl.BlockSpec((tm, d), lambda i: (i, 0)),
                   pl.BlockSpec((tm, LANES), lambda i: (i, 0)),
                   pl.BlockSpec((tm, LANES), lambda i: (i, 0)),
                   pl.BlockSpec((8, LANES), lambda i: (0, 0))),
        scratch_shapes=[pltpu.VMEM((8, LANES), F32)],
        compiler_params=_cparams(("arbitrary",)),
        name="moe_router",
    )(x, g.reshape(1, d), mod, mod, w_router, b_router)


def _dispatch_kernel(dest_ref, h_ref, xs_in_ref, xs_ref, sem):
    del xs_in_ref
    i = pl.program_id(0)
    tm = h_ref.shape[0]

    def row_copy(r, k):
        slot = dest_ref[(i * tm + r) * MOE_TOPK + k]
        return pltpu.make_async_copy(h_ref.at[pl.ds(r, 1)], xs_ref.at[pl.ds(slot, 1)], sem)

    def start(r, carry):
        for k in range(MOE_TOPK):
            row_copy(r, k).start()
        return carry

    def wait(r, carry):
        for k in range(MOE_TOPK):
            row_copy(r, k).wait()
        return carry

    lax.fori_loop(0, tm, start, 0, unroll=8)
    lax.fori_loop(0, tm, wait, 0, unroll=8)


def _dispatch(dest, h, xs_init):
    n, w = h.shape
    tm = ROW_TILE
    return pl.pallas_call(
        _dispatch_kernel,
        out_shape=jax.ShapeDtypeStruct(xs_init.shape, xs_init.dtype),
        grid_spec=pltpu.PrefetchScalarGridSpec(
            num_scalar_prefetch=1,
            grid=(n // tm,),
            in_specs=[pl.BlockSpec((tm, w), lambda i, dest: (i, 0)),
                      pl.BlockSpec(memory_space=pl.ANY)],
            out_specs=pl.BlockSpec(memory_space=pl.ANY),
            scratch_shapes=[pltpu.SemaphoreType.DMA(())]),
        input_output_aliases={2: 0},
        compiler_params=_cparams(("arbitrary",)),
        name="moe_dispatch",
    )(dest, h, xs_init)


def _expert_kernel(be_ref, first_ref, next_ref, nb_ref, xs_ref, wg_hbm, wu_hbm, wd_hbm, ys_ref,
                   wg_st, wu_st, wd_st, wg_bf, wu_bf, wd_bf, sem, *, layer):
    b = pl.program_id(0)
    used = b < nb_ref[0]

    def fetch(e):
        return (pltpu.make_async_copy(wg_hbm.at[layer, e], wg_st, sem.at[0]),
                pltpu.make_async_copy(wu_hbm.at[layer, e], wu_st, sem.at[1]),
                pltpu.make_async_copy(wd_hbm.at[layer, e], wd_st, sem.at[2]))

    @pl.when(b == 0)
    def _():
        for cp in fetch(be_ref[0]):
            cp.start()

    @pl.when(used & (first_ref[b] == 1))
    def _():
        for cp in fetch(be_ref[b]):
            cp.wait()
        wg_bf[...] = wg_st[...].astype(BF16)
        wu_bf[...] = wu_st[...].astype(BF16)
        wd_bf[...] = wd_st[...].astype(BF16)

        @pl.when(next_ref[b] >= 0)
        def _():
            for cp in fetch(next_ref[b]):
                cp.start()

    @pl.when(used)
    def _():
        x = xs_ref[...].astype(BF16)
        gate = jnp.dot(x, wg_bf[...], preferred_element_type=F32)
        up = jnp.dot(x, wu_bf[...], preferred_element_type=F32)
        act = (gate * (1.0 / (1.0 + jnp.exp(-gate))) * up).astype(BF16)
        ys_ref[...] = jnp.dot(act, wd_bf[...], preferred_element_type=F32)

    @pl.when(jnp.logical_not(used))
    def _():
        ys_ref[...] = jnp.zeros_like(ys_ref)


def _experts(block_expert, first, next_expert, n_used, xs, w_gate, w_up, w_down, layer):
    n_slots, d = xs.shape
    ff = w_gate.shape[3]
    nb = n_slots // MOE_BLOCK
    row_spec = pl.BlockSpec((MOE_BLOCK, d), lambda b, *_: (b, 0))
    hbm = pl.BlockSpec(memory_space=pl.ANY)
    return pl.pallas_call(
        functools.partial(_expert_kernel, layer=layer),
        out_shape=jax.ShapeDtypeStruct((n_slots, d), F32),
        grid_spec=pltpu.PrefetchScalarGridSpec(
            num_scalar_prefetch=4,
            grid=(nb,),
            in_specs=[row_spec, hbm, hbm, hbm],
            out_specs=row_spec,
            scratch_shapes=[pltpu.VMEM((d, ff), F32), pltpu.VMEM((d, ff), F32), pltpu.VMEM((ff, d), F32),
                            pltpu.VMEM((d, ff), BF16), pltpu.VMEM((d, ff), BF16), pltpu.VMEM((ff, d), BF16),
                            pltpu.SemaphoreType.DMA((3,))]),
        compiler_params=_cparams(("arbitrary",)),
        name="moe_experts",
    )(block_expert, first, next_expert, n_used, xs, w_gate, w_up, w_down)


def _combine_kernel(dest_ref, x_ref, wts_ref, mod_ref, ys_ref, o_ref, buf, sem, *, n_ctx, row0):
    i = pl.program_id(0)
    tm = x_ref.shape[0]

    def row_copy(r, k):
        slot = dest_ref[(i * tm + r) * MOE_TOPK + k]
        return pltpu.make_async_copy(ys_ref.at[pl.ds(slot, 1)], buf.at[k, pl.ds(r, 1)], sem)

    def start(r, carry):
        for k in range(MOE_TOPK):
            row_copy(r, k).start()
        return carry

    def wait(r, carry):
        for k in range(MOE_TOPK):
            row_copy(r, k).wait()
        return carry

    lax.fori_loop(0, tm, start, 0, unroll=8)
    lax.fori_loop(0, tm, wait, 0, unroll=8)
    w = wts_ref[...]
    y = w[:, 0:1] * buf[0] + w[:, 1:2] * buf[1]
    o_ref[...] = x_ref[...] + _mod_row(mod_ref, i + row0 // tm, tm, n_ctx) * y


def _combine(dest, x, wts, mod, k_gate, ys, n_ctx, row0):
    t, d = x.shape
    tm = ROW_TILE
    n = t - row0
    off = row0 // tm
    return pl.pallas_call(
        functools.partial(_combine_kernel, n_ctx=n_ctx, row0=row0),
        out_shape=jax.ShapeDtypeStruct((n, d), F32),
        grid_spec=pltpu.PrefetchScalarGridSpec(
            num_scalar_prefetch=1,
            grid=(n // tm,),
            in_specs=[pl.BlockSpec((tm, d), lambda i, dest: (i + off, 0)),
                      pl.BlockSpec((tm, LANES), lambda i, dest: (i, 0)),
                      pl.BlockSpec((8, d), lambda i, dest: (0, k_gate)),
                      pl.BlockSpec(memory_space=pl.ANY)],
            out_specs=pl.BlockSpec((tm, d), lambda i, dest: (i, 0)),
            scratch_shapes=[pltpu.VMEM((MOE_TOPK, tm, d), F32), pltpu.SemaphoreType.DMA(())]),
        compiler_params=_cparams(("arbitrary",)),
        name="moe_combine",
    )(dest, x, wts, mod, ys)


def _hier_moe(x, g, mod, w_rg, b_rg, w_re, b_re, w_gate, w_up, w_down, layer, n_ctx, row0, xs_buf):
    t, d = x.shape
    n = t - row0
    pad = LANES - MOE_EXPERTS - MOE_GROUPS
    w_router = jnp.concatenate([w_re, w_rg, jnp.zeros((d, pad), F32)], axis=1)
    w_hi = w_router.astype(BF16)
    w_router = jnp.concatenate([w_hi, (w_router - w_hi.astype(F32)).astype(BF16)], axis=1)
    b_router = jnp.concatenate([b_re, b_rg, jnp.zeros((pad,), F32)]).reshape(1, LANES)
    h, ids, wts, counts = _router(x, g, mod, 3, 4, w_router, b_router, n_ctx, row0)
    counts = counts[0, :MOE_EXPERTS].astype(I32)
    padded = (counts + MOE_BLOCK - 1) // MOE_BLOCK * MOE_BLOCK
    seg_end = jnp.cumsum(padded)
    seg_start = seg_end - padded
    dest = (seg_start[ids[:, :MOE_TOPK]] + ids[:, MOE_TOPK:2 * MOE_TOPK]).reshape(-1)
    n_blocks = (n * MOE_TOPK + MOE_EXPERTS * (MOE_BLOCK - 1)) // MOE_BLOCK + 1
    block_start = jnp.arange(n_blocks, dtype=I32) * MOE_BLOCK
    block_expert = jnp.minimum(jnp.sum((seg_end[None, :] <= block_start[:, None]).astype(I32), axis=1),
                               MOE_EXPERTS - 1).astype(I32)
    n_used = (seg_end[-1:] // MOE_BLOCK).astype(I32)
    block_id = jnp.arange(n_blocks, dtype=I32)
    prev_expert = jnp.concatenate([jnp.full((1,), -1, I32), block_expert[:-1]])
    first = ((block_id < n_used[0]) & (block_expert != prev_expert)).astype(I32)
    eid = jnp.arange(MOE_EXPERTS, dtype=I32)
    owner = jnp.where(padded > 0, eid, MOE_EXPERTS)
    later = jnp.concatenate([lax.cummin(owner[::-1])[::-1][1:], jnp.full((1,), MOE_EXPERTS, I32)])
    next_expert = jnp.where(later < MOE_EXPERTS, later, -1)[block_expert].astype(I32)
    if xs_buf is None:
        xs_buf = jnp.zeros((n_blocks * MOE_BLOCK, d), F32)
    xs = _dispatch(dest, h, xs_buf)
    ys = _experts(block_expert, first, next_expert, n_used, xs, w_gate, w_up, w_down, layer)
    return _combine(dest, x, wts, mod, 5, ys, n_ctx, row0), xs


def _rope_table(pos, dim, base):
    inv = jnp.power(base, -jnp.arange(0, dim, 2, dtype=F32) / dim)
    ang = pos.astype(F32)[:, None] * inv[None, :]
    return jnp.cos(ang), jnp.sin(ang)


def _flipped_positions(t, n_ctx):
    idx = jnp.arange(t, dtype=I32)
    return jnp.where(idx < n_ctx, n_ctx - 1 - idx, n_ctx + (t - 1 - idx))


def _axial_tables(n_ctx, n_lat, hd):
    rows = n_lat // GRID_W
    row = jnp.repeat(jnp.arange(rows, dtype=I32), GRID_W)
    col = jnp.tile(jnp.arange(GRID_W, dtype=I32), rows)
    half = hd // 2
    cr, sr = _rope_table(row, half, ROPE_BASE)
    cc, sc = _rope_table(col, half, ROPE_BASE)
    cos = jnp.concatenate([cr, cr, cc, cc], axis=1)
    sin = jnp.concatenate([-sr, sr, -sc, sc], axis=1)
    cos = jnp.concatenate([jnp.ones((n_ctx, hd), F32), cos], axis=0)
    sin = jnp.concatenate([jnp.zeros((n_ctx, hd), F32), sin], axis=0)
    return cos, sin


def _retention_layer(xa, mod, g, w_in, logit_gamma, gn_w, gn_b, w_out, j, n_ctx):
    t, d = xa.shape
    heads = RET_HEADS
    dk = d // heads
    dv = 2 * dk
    qk_w, v_w = heads * dk, heads * dv
    h = _norm_mod(xa, g, mod, 0, 1, n_ctx)
    pos_f = jnp.arange(t, dtype=I32)
    cf, sf = _rope_table(pos_f, dk, RET_ROPE_BASE)
    cb, sb = _rope_table(_flipped_positions(t, n_ctx), dk, RET_ROPE_BASE)
    tables = (cf, sf, cb, sb)
    q = _matmul_rope(h, w_in, j, 0, qk_w, tables, dk ** -0.5, dk)
    k = _matmul_rope(h, w_in, j, qk_w, qk_w, tables, 1.0, dk)
    vg = _matmul(h, w_in, j, 2 * qk_w, 2 * v_w, BF16)
    log_gamma = jax.nn.log_sigmoid(logit_gamma.astype(F32))
    yf = _retention_scan(log_gamma[0], q, k, vg, 0, n_ctx, heads, dk, dv, reverse=False)
    yb = _retention_scan(log_gamma[1], q, k, vg, 1, n_ctx, heads, dk, dv, reverse=True)
    yn = _ret_post(yf, yb, vg, gn_w, gn_b, heads, dv)
    return _matmul_residual(yn, w_out, j, xa, mod, 2, n_ctx)


def _mlstm_layer(xa, mod, g, w_in, conv_w, b_gate, norm_w, w_out, j, n_ctx):
    t, d = xa.shape
    heads = MLSTM_HEADS
    dk = d // (2 * heads)
    dv = d // heads
    qk_w, v_w = heads * dk, heads * dv
    n_gates = 4 * heads
    h = _norm_mod(xa, g, mod, 0, 1, n_ctx)
    qk_pre = _matmul(h, w_in, j, 0, 2 * qk_w, F32)
    vo = _matmul(h, w_in, j, 2 * qk_w, 2 * v_w, BF16)
    w_gates = jnp.pad(w_in[j, :, 2 * qk_w + 2 * v_w:], ((0, 0), (0, LANES - n_gates)))
    gates = _matmul(h, w_gates[None], 0, 0, LANES, F32)
    bias = jnp.pad(b_gate.astype(F32).reshape(1, n_gates), ((0, 0), (0, LANES - n_gates)))
    qk = _conv_silu(qk_pre, conv_w, n_ctx, qk_w, dk ** -0.5)
    yf = _mlstm_scan(qk, vo, gates, bias, n_ctx, heads, dk, dv, reverse=False)
    yb = _mlstm_scan(qk, vo, gates, bias, n_ctx, heads, dk, dv, reverse=True)
    yn = _mlstm_post(yf, yb, vo, norm_w, heads, dv)
    return _matmul_residual(yn, w_out, j, xa, mod, 2, n_ctx)


def _attention_layer(xa, mod, g, w_in, q_norm, k_norm, w_out, j, n_ctx):
    t, d = xa.shape
    hd = d // ATTN_HEADS
    groups = ATTN_HEADS // ATTN_KV_HEADS
    q_w, kv_w = ATTN_HEADS * hd, ATTN_KV_HEADS * hd
    n_lat = t - n_ctx
    h = _norm_mod(xa, g, mod, 0, 1, n_ctx)
    qk_pre = _matmul(h, w_in, j, 0, q_w + kv_w, F32)
    v = _matmul(h, w_in, j, q_w + kv_w, kv_w, BF16)
    cos, sin = _axial_tables(n_ctx, n_lat, hd)
    q_gain = q_norm * (hd ** -0.5 * math.log2(math.e))
    q, k = _qk_norm_rope(qk_pre, cos, sin, q_gain, k_norm, ATTN_HEADS, ATTN_KV_HEADS, hd)
    o = _flash_attention(q, k, v, n_ctx, ATTN_KV_HEADS, groups, hd)
    return _matmul_residual(o, w_out, j, xa, mod, 2, n_ctx)


def kernel(x, c, ctx, c_ctx, w_ada, b_ada, norm_g, ret_w_in, ret_logit_gamma, ret_gn_w, ret_gn_b, ret_w_out, mlstm_w_in, mlstm_conv_w, mlstm_b_gate, mlstm_norm_w, mlstm_w_out, attn_w_in, attn_q_norm, attn_k_norm, attn_w_out, moe_w_router_group, moe_b_router_group, moe_w_router_expert, moe_b_router_expert, moe_w_gate, moe_w_up, moe_w_down, final_norm_g):
    bsz, n_lat, d = x.shape
    n_ctx = ctx.shape[1]
    depth = w_ada.shape[0]
    assert bsz == 1 and n_ctx % ROW_TILE == 0 and n_lat % ROW_TILE == 0 and n_lat % GRID_W == 0
    xa = jnp.concatenate([ctx[0], x[0]], axis=0)
    s = jnp.stack([jax.nn.silu(c[0]), jax.nn.silu(c_ctx)])
    s8 = jnp.zeros((8, d), F32).at[:2].set(s).astype(BF16)
    mods = _ada_modulation(s8, w_ada, b_ada)
    row0 = 0
    xs_buf = None
    for i in range(depth):
        kind, j = i % 3, i // 3
        mod = mods[i]
        if kind == 0:
            xa = _retention_layer(xa, mod, norm_g[i, 0], ret_w_in, ret_logit_gamma[j], ret_gn_w[j], ret_gn_b[j],
                                  ret_w_out, j, n_ctx)
        elif kind == 1:
            xa = _mlstm_layer(xa, mod, norm_g[i, 0], mlstm_w_in, mlstm_conv_w[j], mlstm_b_gate[j],
                              mlstm_norm_w[j], mlstm_w_out, j, n_ctx)
        else:
            xa = _attention_layer(xa, mod, norm_g[i, 0], attn_w_in, attn_q_norm[j], attn_k_norm[j],
                                  attn_w_out, j, n_ctx)
        xa, xs_buf = _hier_moe(xa, norm_g[i, 1], mod, moe_w_router_group[i], moe_b_router_group[i],
                       moe_w_router_expert[i], moe_b_router_expert[i], moe_w_gate, moe_w_up, moe_w_down, i,
                       n_ctx, row0, xs_buf)
    return _final_norm(xa, final_norm_g, n_ctx)[None]
```

```python
import functools
import math

import jax
import jax.numpy as jnp
import numpy as np
from jax import lax
from jax.experimental import pallas as pl
from jax.experimental.pallas import tpu as pltpu

F32 = jnp.float32
BF16 = jnp.bfloat16
I32 = jnp.int32
U32 = jnp.uint32

LANES = 128
CHUNK = 128
NORM_EPS = 1e-6
GRID_W = 64
RET_HEADS = 8
RET_ROPE_BASE = 10000.0
MLSTM_HEADS = 8
GATE_SOFTCAP = 15.0
ATTN_HEADS = 16
ATTN_KV_HEADS = 8
ROPE_BASE = 10000.0
MOE_GROUPS = 4
MOE_PER_GROUP = 8
MOE_EXPERTS = MOE_GROUPS * MOE_PER_GROUP
MOE_TOPK = 2
MOE_BLOCK = 256
ROW_TILE = 256
VMEM_LIMIT = 56 * 1024 * 1024
NEG_INF = float("-inf")


def _cparams(sem, vmem=VMEM_LIMIT):
    return pltpu.CompilerParams(dimension_semantics=sem, vmem_limit_bytes=vmem)


def _pick_tile(total, candidates):
    for c in candidates:
        if total % c == 0:
            return c
    raise ValueError(f"no tile for {total} in {candidates}")


def _mod_row(mod_ref, i, tile, n_ctx):
    sel = jnp.where(i < n_ctx // tile, 1, 0)
    return mod_ref[pl.ds(sel, 1), :]


def _ada_kernel(s_ref, w_ref, b_ref, o_ref):
    w = w_ref[0].astype(BF16)
    o_ref[0] = jnp.dot(s_ref[...], w, preferred_element_type=F32) + b_ref[0]


def _ada_modulation(s8, w_ada, b_ada):
    depth, d, n = w_ada.shape
    tn = _pick_tile(n, (1024, 512, 256, 128))
    return pl.pallas_call(
        _ada_kernel,
        out_shape=jax.ShapeDtypeStruct((depth, 8, n), F32),
        grid=(depth, n // tn),
        in_specs=[pl.BlockSpec((8, d), lambda l, j: (0, 0)),
                  pl.BlockSpec((1, d, tn), lambda l, j: (l, 0, j)),
                  pl.BlockSpec((1, 1, tn), lambda l, j: (l, 0, j))],
        out_specs=pl.BlockSpec((1, 8, tn), lambda l, j: (l, 0, j)),
        compiler_params=_cparams(("arbitrary", "arbitrary")),
        name="ada_modulation",
    )(s8, w_ada, b_ada.reshape(depth, 1, n))


def _rms(x):
    return x * lax.rsqrt(jnp.mean(x * x, axis=-1, keepdims=True) + NORM_EPS)


def _norm_mod_kernel(x_ref, g_ref, sh_ref, sc_ref, o_ref, *, n_ctx):
    i = pl.program_id(0)
    tm = x_ref.shape[0]
    y = _rms(x_ref[...]) * g_ref[...]
    y = y * (1.0 + _mod_row(sc_ref, i, tm, n_ctx)) + _mod_row(sh_ref, i, tm, n_ctx)
    o_ref[...] = y.astype(o_ref.dtype)


def _norm_mod(x, g, mod, k_shift, k_scale, n_ctx):
    t, d = x.shape
    tm = ROW_TILE
    return pl.pallas_call(
        functools.partial(_norm_mod_kernel, n_ctx=n_ctx),
        out_shape=jax.ShapeDtypeStruct((t, d), BF16),
        grid=(t // tm,),
        in_specs=[pl.BlockSpec((tm, d), lambda i: (i, 0)),
                  pl.BlockSpec((1, d), lambda i: (0, 0)),
                  pl.BlockSpec((8, d), lambda i: (0, k_shift)),
                  pl.BlockSpec((8, d), lambda i: (0, k_scale))],
        out_specs=pl.BlockSpec((tm, d), lambda i: (i, 0)),
        compiler_params=_cparams(("arbitrary",)),
        name="norm_mod",
    )(x, g.reshape(1, d), mod, mod)


def _cast_weight_once(w_ref, wbf_ref):
    @pl.when(pl.program_id(1) == 0)
    def _():
        wbf_ref[...] = w_ref[...].astype(BF16)


def _mm_plain_kernel(a_ref, w_ref, o_ref, wbf_ref, *, scale):
    _cast_weight_once(w_ref, wbf_ref)
    acc = jnp.dot(a_ref[...], wbf_ref[...], preferred_element_type=F32)
    if scale != 1.0:
        acc = acc * scale
    o_ref[...] = acc.astype(o_ref.dtype)


def _mm_residual_kernel(a_ref, w_ref, x_ref, mod_ref, o_ref, wbf_ref, *, n_ctx):
    _cast_weight_once(w_ref, wbf_ref)
    tm = a_ref.shape[0]
    acc = jnp.dot(a_ref[...], wbf_ref[...], preferred_element_type=F32)
    row = pl.program_id(1) * tm + lax.broadcasted_iota(I32, (tm, 1), 0)
    gate = jnp.where(row < n_ctx, mod_ref[1:2, :], mod_ref[0:1, :])
    o_ref[...] = x_ref[...] + gate * acc


def _mm_rope_kernel(a_ref, w_ref, cf_ref, sf_ref, cb_ref, sb_ref, o_ref, wbf_ref, *, scale, head_dim):
    _cast_weight_once(w_ref, wbf_ref)
    acc = jnp.dot(a_ref[...], wbf_ref[...], preferred_element_type=F32)
    if scale != 1.0:
        acc = acc * scale
    half = head_dim // 2
    for d, (c_ref, s_ref) in enumerate(((cf_ref, sf_ref), (cb_ref, sb_ref))):
        c, s = c_ref[...], s_ref[...]
        for h in range(acc.shape[1] // head_dim):
            x1 = acc[:, h * head_dim:h * head_dim + half]
            x2 = acc[:, h * head_dim + half:(h + 1) * head_dim]
            o_ref[d, :, h * head_dim:h * head_dim + half] = (x1 * c - x2 * s).astype(o_ref.dtype)
            o_ref[d, :, h * head_dim + half:(h + 1) * head_dim] = (x1 * s + x2 * c).astype(o_ref.dtype)


def _mm_tiles(m, k, n):
    tm = _pick_tile(m, (768, 512, 384, 256, 128))
    tn_cap = 1024 if k <= 2048 else 512
    tn = _pick_tile(n, tuple(c for c in (1024, 512, 256, 128) if c <= tn_cap))
    return tm, tn


def _matmul(a, w, layer, col0, n, out_dtype, scale=1.0):
    m, k = a.shape
    tm, tn = _mm_tiles(m, k, n)
    assert col0 % tn == 0
    c0 = col0 // tn
    return pl.pallas_call(
        functools.partial(_mm_plain_kernel, scale=scale),
        out_shape=jax.ShapeDtypeStruct((m, n), out_dtype),
        grid=(n // tn, m // tm),
        in_specs=[pl.BlockSpec((tm, k), lambda j, i: (i, 0)),
                  pl.BlockSpec((None, k, tn), lambda j, i: (layer, 0, j + c0))],
        out_specs=pl.BlockSpec((tm, tn), lambda j, i: (i, j)),
        scratch_shapes=[pltpu.VMEM((k, tn), BF16)],
        compiler_params=_cparams(("arbitrary", "arbitrary")),
        name="matmul",
    )(a, w)


def _matmul_residual(a, w, layer, x, mod, k_gate, n_ctx):
    m, k = a.shape
    n = w.shape[2]
    tm, tn = _mm_tiles(m, k, n)
    kb = k_gate * (n // tn)
    return pl.pallas_call(
        functools.partial(_mm_residual_kernel, n_ctx=n_ctx),
        out_shape=jax.ShapeDtypeStruct((m, n), F32),
        grid=(n // tn, m // tm),
        in_specs=[pl.BlockSpec((tm, k), lambda j, i: (i, 0)),
                  pl.BlockSpec((None, k, tn), lambda j, i: (layer, 0, j)),
                  pl.BlockSpec((tm, tn), lambda j, i: (i, j)),
                  pl.BlockSpec((8, tn), lambda j, i: (0, kb + j))],
        out_specs=pl.BlockSpec((tm, tn), lambda j, i: (i, j)),
        scratch_shapes=[pltpu.VMEM((k, tn), BF16)],
        compiler_params=_cparams(("arbitrary", "arbitrary")),
        name="matmul_residual",
    )(a, w, x, mod)


def _matmul_rope(a, w, layer, col0, n, tables, scale, head_dim):
    m, k = a.shape
    tm, tn = _mm_tiles(m, k, n)
    assert col0 % tn == 0 and tn % head_dim == 0
    c0 = col0 // tn
    half = head_dim // 2
    tab_spec = pl.BlockSpec((tm, half), lambda j, i: (i, 0))
    return pl.pallas_call(
        functools.partial(_mm_rope_kernel, scale=scale, head_dim=head_dim),
        out_shape=jax.ShapeDtypeStruct((2, m, n), BF16),
        grid=(n // tn, m // tm),
        in_specs=[pl.BlockSpec((tm, k), lambda j, i: (i, 0)),
                  pl.BlockSpec((None, k, tn), lambda j, i: (layer, 0, j + c0)),
                  tab_spec, tab_spec, tab_spec, tab_spec],
        out_specs=pl.BlockSpec((2, tm, tn), lambda j, i: (0, i, j)),
        scratch_shapes=[pltpu.VMEM((k, tn), BF16)],
        compiler_params=_cparams(("arbitrary", "arbitrary")),
        name="matmul_rope",
    )(a, w, *tables)


def _chunk_index(c, n_chunks, ctx_chunks, reverse):
    if not reverse:
        return c
    return jnp.where(c < ctx_chunks, ctx_chunks - 1 - c, n_chunks - 1 - (c - ctx_chunks))


def _dot_nt(a, b):
    return lax.dot_general(a, b, (((1,), (1,)), ((), ())), preferred_element_type=F32)


def _dot_tn(a, b):
    return lax.dot_general(a, b, (((0,), (0,)), ((), ())), preferred_element_type=F32)


def _retention_kernel(lg_ref, q_ref, k_ref, v_ref, o_ref, s_ref, *, reverse, heads, dk, dv):
    c = pl.program_id(0)

    @pl.when(c == 0)
    def _():
        s_ref[...] = jnp.zeros_like(s_ref)

    n = q_ref.shape[0]
    ri = lax.broadcasted_iota(I32, (n, n), 0)
    ci = lax.broadcasted_iota(I32, (n, n), 1)
    diff = (ci - ri) if reverse else (ri - ci)
    dmask = diff >= 0
    dist = jnp.where(dmask, diff, 0).astype(F32)
    r1 = lax.broadcasted_iota(I32, (n, 1), 0)
    pos = ((n - 1 - r1) if reverse else r1).astype(F32)
    for h in range(heads):
        lg = jnp.full((1, 1), lg_ref[h], F32)
        decay = jnp.where(dmask, jnp.exp(dist * lg), 0.0)
        q_dec = jnp.exp((pos + 1.0) * lg)
        k_dec = jnp.exp((n - 1.0 - pos) * lg)
        c_dec = jnp.exp(float(n) * lg)
        q = q_ref[:, h * dk:(h + 1) * dk]
        k = k_ref[:, h * dk:(h + 1) * dk]
        v = v_ref[:, h * dv:(h + 1) * dv]
        s_prev = s_ref[h]
        scores = (_dot_nt(q, k) * decay).astype(BF16)
        inner = jnp.dot(scores, v, preferred_element_type=F32)
        cross = jnp.dot(q, s_prev.astype(BF16), preferred_element_type=F32)
        o_ref[:, h * dv:(h + 1) * dv] = (inner + q_dec * cross).astype(o_ref.dtype)
        kd = (k.astype(F32) * k_dec).astype(BF16)
        s_ref[h] = c_dec * s_prev + _dot_tn(kd, v)


def _retention_scan(log_gamma, q, k, vg, d_idx, n_ctx, heads, dk, dv, reverse):
    t = q.shape[1]
    nc, cc = t // CHUNK, n_ctx // CHUNK
    cm = lambda c: _chunk_index(c, nc, cc, reverse)
    return pl.pallas_call(
        functools.partial(_retention_kernel, reverse=reverse, heads=heads, dk=dk, dv=dv),
        out_shape=jax.ShapeDtypeStruct((t, heads * dv), BF16),
        grid=(nc,),
        in_specs=[pl.BlockSpec(memory_space=pltpu.SMEM),
                  pl.BlockSpec((None, CHUNK, heads * dk), lambda c: (d_idx, cm(c), 0)),
                  pl.BlockSpec((None, CHUNK, heads * dk), lambda c: (d_idx, cm(c), 0)),
                  pl.BlockSpec((CHUNK, heads * dv), lambda c: (cm(c), 0))],
        out_specs=pl.BlockSpec((CHUNK, heads * dv), lambda c: (cm(c), 0)),
        scratch_shapes=[pltpu.VMEM((heads, dk, dv), F32)],
        compiler_params=_cparams(("arbitrary",)),
        name="retention_scan_rev" if reverse else "retention_scan_fwd",
    )(log_gamma, q, k, vg)


def _ret_post_kernel(yf_ref, yb_ref, g_ref, w_ref, b_ref, o_ref, *, heads, dv):
    for h in range(heads):
        sl = slice(h * dv, (h + 1) * dv)
        y = yf_ref[:, sl].astype(F32) + yb_ref[:, sl].astype(F32)
        mu = jnp.mean(y, axis=-1, keepdims=True)
        yc = y - mu
        var = jnp.mean(yc * yc, axis=-1, keepdims=True)
        yn = yc * lax.rsqrt(var + NORM_EPS)
        g = g_ref[:, sl].astype(F32)
        silu = g * (1.0 / (1.0 + jnp.exp(-g)))
        o_ref[:, sl] = ((yn * w_ref[:, sl] + b_ref[:, sl]) * silu).astype(o_ref.dtype)


def _ret_post(yf, yb, vg, gn_w, gn_b, heads, dv):
    t, vw = yf.shape
    tm = ROW_TILE
    return pl.pallas_call(
        functools.partial(_ret_post_kernel, heads=heads, dv=dv),
        out_shape=jax.ShapeDtypeStruct((t, vw), BF16),
        grid=(t // tm,),
        in_specs=[pl.BlockSpec((tm, vw), lambda i: (i, 0)),
                  pl.BlockSpec((tm, vw), lambda i: (i, 0)),
                  pl.BlockSpec((tm, vw), lambda i: (i, 1)),
                  pl.BlockSpec((1, vw), lambda i: (0, 0)),
                  pl.BlockSpec((1, vw), lambda i: (0, 0))],
        out_specs=pl.BlockSpec((tm, vw), lambda i: (i, 0)),
        compiler_params=_cparams(("arbitrary",)),
        name="retention_post",
    )(yf, yb, vg, gn_w.reshape(1, vw), gn_b.reshape(1, vw))


def _conv_silu_kernel(x_ref, prev_ref, next_ref, w_ref, o_ref, *, n_ctx, t_total, q_cols, q_scale):
    i = pl.program_id(0)
    tm = x_ref.shape[0]
    x = x_ref[...]
    row = lax.broadcasted_iota(I32, (tm, 1), 0)
    grow = i * tm + row
    halo = prev_ref.shape[0]
    x_prev = jnp.where(row == 0, prev_ref[halo - 1:halo, :], pltpu.roll(x, 1, axis=0))
    x_prev = jnp.where((grow == 0) | (grow == n_ctx), 0.0, x_prev)
    x_next = jnp.where(row == tm - 1, next_ref[0:1, :], pltpu.roll(x, tm - 1, axis=0))
    x_next = jnp.where((grow == n_ctx - 1) | (grow == t_total - 1), 0.0, x_next)
    y = x_prev * w_ref[0:1, :] + x * w_ref[1:2, :] + x_next * w_ref[2:3, :]
    y = y * (1.0 / (1.0 + jnp.exp(-y)))
    o_ref[:, :q_cols] = (y[:, :q_cols] * q_scale).astype(o_ref.dtype)
    o_ref[:, q_cols:] = y[:, q_cols:].astype(o_ref.dtype)


def _conv_silu(x, conv_w, n_ctx, q_cols, q_scale):
    t, c = x.shape
    tm = ROW_TILE
    halo = 8
    r = tm // halo
    last = t // halo - 1
    w8 = jnp.zeros((8, c), F32).at[:3].set(conv_w)
    return pl.pallas_call(
        functools.partial(_conv_silu_kernel, n_ctx=n_ctx, t_total=t, q_cols=q_cols, q_scale=q_scale),
        out_shape=jax.ShapeDtypeStruct((t, c), BF16),
        grid=(t // tm,),
        in_specs=[pl.BlockSpec((tm, c), lambda i: (i, 0)),
                  pl.BlockSpec((halo, c), lambda i: (jnp.maximum(i * r - 1, 0), 0)),
                  pl.BlockSpec((halo, c), lambda i: (jnp.minimum((i + 1) * r, last), 0)),
                  pl.BlockSpec((8, c), lambda i: (0, 0))],
        out_specs=pl.BlockSpec((tm, c), lambda i: (i, 0)),
        compiler_params=_cparams(("arbitrary",)),
        name="mlstm_conv_silu",
    )(x, x, x, w8)


def _mlstm_kernel(qk_ref, v_ref, gt_ref, bias_ref, o_ref, c_ref, n_ref, m_ref, *, reverse, heads, dk, dv, d_idx):
    c = pl.program_id(0)

    @pl.when(c == 0)
    def _():
        c_ref[...] = jnp.zeros_like(c_ref)
        n_ref[...] = jnp.zeros_like(n_ref)
        m_ref[...] = jnp.zeros_like(m_ref)

    n = qk_ref.shape[0]
    gates = gt_ref[...] + bias_ref[...]
    i_all = GATE_SOFTCAP * jnp.tanh(gates * (1.0 / GATE_SOFTCAP))
    f_all = -(jnp.maximum(-gates, 0.0) + jnp.log1p(jnp.exp(-jnp.abs(gates))))
    ri = lax.broadcasted_iota(I32, (n, n), 0)
    ci = lax.broadcasted_iota(I32, (n, n), 1)
    mask = (ci >= ri) if reverse else (ci <= ri)
    tri = jnp.where(mask, 1.0, 0.0).astype(F32)
    b_all = jnp.dot(tri, f_all, preferred_element_type=F32, precision=lax.Precision.HIGHEST)
    b_all_t = b_all.T
    i_all_t = i_all.T
    last = 0 if reverse else n - 1
    qk_w = heads * dk
    for h in range(heads):
        ic = d_idx * 2 * heads + h
        fc = ic + heads
        b_col = b_all[:, fc:fc + 1]
        b_row = b_all_t[fc:fc + 1, :]
        i_col = i_all[:, ic:ic + 1]
        i_row = i_all_t[ic:ic + 1, :]
        m_prev = m_ref[h][:, 0:1]
        q = qk_ref[:, h * dk:(h + 1) * dk]
        k = qk_ref[:, qk_w + h * dk:qk_w + (h + 1) * dk]
        v = v_ref[:, h * dv:(h + 1) * dv]
        c_prev = c_ref[h]
        n_prev = n_ref[h]
        d_log = jnp.where(mask, b_col - b_row + i_row, NEG_INF)
        m_t = jnp.maximum(b_col + m_prev, jnp.max(d_log, axis=-1, keepdims=True))
        w = jnp.exp(d_log - m_t)
        s = _dot_nt(q, k) * w
        inter = jnp.exp(b_col + m_prev - m_t)
        num = jnp.dot(s.astype(BF16), v, preferred_element_type=F32) + inter * jnp.dot(
            q, c_prev.astype(BF16), preferred_element_type=F32)
        den = jnp.sum(s, axis=-1, keepdims=True) + inter * jnp.sum(q.astype(F32) * n_prev, axis=-1, keepdims=True)
        o_ref[:, h * dv:(h + 1) * dv] = (num / jnp.maximum(jnp.abs(den), jnp.exp(-m_t))).astype(o_ref.dtype)
        b_last = b_col[last:last + 1, :]
        g = b_last - b_col + i_col
        m_new = jnp.maximum(b_last + m_prev, jnp.max(g, axis=0, keepdims=True))
        wk = jnp.exp(g - m_new) * k.astype(F32)
        dec = jnp.exp(b_last + m_prev - m_new)
        c_ref[h] = dec * c_prev + _dot_tn(wk.astype(BF16), v)
        n_ref[h] = dec * n_prev + jnp.sum(wk, axis=0, keepdims=True)
        m_ref[h] = jnp.broadcast_to(m_new, m_ref.shape[1:])


def _mlstm_scan(qk, vo, gates, bias, n_ctx, heads, dk, dv, reverse):
    t = qk.shape[0]
    nc, cc = t // CHUNK, n_ctx // CHUNK
    cm = lambda c: _chunk_index(c, nc, cc, reverse)
    return pl.pallas_call(
        functools.partial(_mlstm_kernel, reverse=reverse, heads=heads, dk=dk, dv=dv, d_idx=1 if reverse else 0),
        out_shape=jax.ShapeDtypeStruct((t, heads * dv), BF16),
        grid=(nc,),
        in_specs=[pl.BlockSpec((CHUNK, 2 * heads * dk), lambda c: (cm(c), 0)),
                  pl.BlockSpec((CHUNK, heads * dv), lambda c: (cm(c), 0)),
                  pl.BlockSpec((CHUNK, LANES), lambda c: (cm(c), 0)),
                  pl.BlockSpec((1, LANES), lambda c: (0, 0))],
        out_specs=pl.BlockSpec((CHUNK, heads * dv), lambda c: (cm(c), 0)),
        scratch_shapes=[pltpu.VMEM((heads, dk, dv), F32),
                        pltpu.VMEM((heads, 1, dk), F32),
                        pltpu.VMEM((heads, 1, LANES), F32)],
        compiler_params=_cparams(("arbitrary",)),
        name="mlstm_scan_rev" if reverse else "mlstm_scan_fwd",
    )(qk, vo, gates, bias)


def _mlstm_post_kernel(yf_ref, yb_ref, o_ref_in, w_ref, o_ref, *, heads, dv):
    for h in range(heads):
        sl = slice(h * dv, (h + 1) * dv)
        y = _rms(yf_ref[:, sl].astype(F32) + yb_ref[:, sl].astype(F32))
        o = o_ref_in[:, sl].astype(F32)
        o_ref[:, sl] = (y * w_ref[:, sl] * (1.0 / (1.0 + jnp.exp(-o)))).astype(o_ref.dtype)


def _mlstm_post(yf, yb, vo, norm_w, heads, dv):
    t, vw = yf.shape
    tm = ROW_TILE
    return pl.pallas_call(
        functools.partial(_mlstm_post_kernel, heads=heads, dv=dv),
        out_shape=jax.ShapeDtypeStruct((t, vw), BF16),
        grid=(t // tm,),
        in_specs=[pl.BlockSpec((tm, vw), lambda i: (i, 0)),
                  pl.BlockSpec((tm, vw), lambda i: (i, 0)),
                  pl.BlockSpec((tm, vw), lambda i: (i, 1)),
                  pl.BlockSpec((1, vw), lambda i: (0, 0))],
        out_specs=pl.BlockSpec((tm, vw), lambda i: (i, 0)),
        compiler_params=_cparams(("arbitrary",)),
        name="mlstm_post",
    )(yf, yb, vo, norm_w.reshape(1, vw))


def _qk_norm_rope_kernel(x_ref, cos_ref, sin_ref, qn_ref, kn_ref, q_ref, k_ref, *, q_heads, k_heads, hd):
    cos, sin = cos_ref[...], sin_ref[...]
    lane = lax.broadcasted_iota(I32, (1, hd), 1)
    first = (lane % (hd // 2)) < (hd // 4)
    for h in range(q_heads + k_heads):
        x = x_ref[:, h * hd:(h + 1) * hd]
        w = qn_ref[...] if h < q_heads else kn_ref[...]
        xh = _rms(x) * w
        partner = jnp.where(first, pltpu.roll(xh, hd - hd // 4, axis=1), pltpu.roll(xh, hd // 4, axis=1))
        y = (xh * cos + partner * sin).astype(q_ref.dtype)
        if h < q_heads:
            q_ref[:, h * hd:(h + 1) * hd] = y
        else:
            k_ref[:, (h - q_heads) * hd:(h - q_heads + 1) * hd] = y


def _qk_norm_rope(qk, cos, sin, q_norm, k_norm, q_heads, k_heads, hd):
    t = qk.shape[0]
    tm = ROW_TILE
    return pl.pallas_call(
        functools.partial(_qk_norm_rope_kernel, q_heads=q_heads, k_heads=k_heads, hd=hd),
        out_shape=(jax.ShapeDtypeStruct((t, q_heads * hd), BF16), jax.ShapeDtypeStruct((t, k_heads * hd), BF16)),
        grid=(t // tm,),
        in_specs=[pl.BlockSpec((tm, (q_heads + k_heads) * hd), lambda i: (i, 0)),
                  pl.BlockSpec((tm, hd), lambda i: (i, 0)),
                  pl.BlockSpec((tm, hd), lambda i: (i, 0)),
                  pl.BlockSpec((1, hd), lambda i: (0, 0)),
                  pl.BlockSpec((1, hd), lambda i: (0, 0))],
        out_specs=(pl.BlockSpec((tm, q_heads * hd), lambda i: (i, 0)),
                   pl.BlockSpec((tm, k_heads * hd), lambda i: (i, 0))),
        compiler_params=_cparams(("arbitrary",)),
        name="attn_qk_norm_rope",
    )(qk, cos, sin, q_norm.reshape(1, hd), k_norm.reshape(1, hd))


def _flash_kernel(q_ref, k_ref, v_ref, o_ref, m_ref, l_ref, acc_ref, sa_ref, sb_ref, *, groups, hd, tk, n_ctx):
    tq = q_ref.shape[0]
    n_kv = k_ref.shape[0]
    n = n_kv // tk
    q = jnp.concatenate([q_ref[:, g * hd:(g + 1) * hd] for g in range(groups)], axis=0)
    m_ref[...] = jnp.full_like(m_ref, NEG_INF)
    l_ref[...] = jnp.zeros_like(l_ref)
    acc_ref[...] = jnp.zeros_like(acc_ref)

    def scores(off, size):
        return _dot_nt(q, k_ref[pl.ds(off, size), :])

    def absorb(s, off, size):
        m_prev = m_ref[...]
        m_new = jnp.maximum(m_prev, jnp.max(s, axis=-1, keepdims=True))
        alpha = jnp.exp2(m_prev - m_new)
        p = jnp.exp2(s - jnp.concatenate([m_new] * (size // LANES), axis=1))
        l_ref[...] = alpha * l_ref[...] + jnp.sum(p, axis=-1, keepdims=True)
        acc_ref[...] = alpha * acc_ref[...] + jnp.dot(p.astype(BF16), v_ref[pl.ds(off, size), :],
                                                      preferred_element_type=F32)
        m_ref[...] = m_new

    is_ctx = pl.program_id(1) < n_ctx // tq

    @pl.when(is_ctx)
    def _():
        absorb(scores(0, n_ctx), 0, n_ctx)

    @pl.when(jnp.logical_not(is_ctx))
    def _():
        sa_ref[...] = scores(0, tk)

        def pair(jj, carry):
            off_a = pl.multiple_of(2 * jj * tk, tk)
            sb_ref[...] = scores(off_a + tk, tk)
            absorb(sa_ref[...], off_a, tk)
            sa_ref[...] = scores(off_a + 2 * tk, tk)
            absorb(sb_ref[...], off_a + tk, tk)
            return carry

        n_pairs = (n - 1) // 2
        lax.fori_loop(0, n_pairs, pair, 0)
        if n % 2 == 0:
            sb_ref[...] = scores((n - 1) * tk, tk)
            absorb(sa_ref[...], (n - 2) * tk, tk)
            absorb(sb_ref[...], (n - 1) * tk, tk)
        else:
            absorb(sa_ref[...], (n - 1) * tk, tk)

    out = acc_ref[...] / l_ref[...]
    for g in range(groups):
        o_ref[:, g * hd:(g + 1) * hd] = out[g * tq:(g + 1) * tq].astype(o_ref.dtype)


def _flash_attention(q, k, v, n_ctx, kv_heads, groups, hd):
    t = q.shape[0]
    assert hd == LANES
    tq = _pick_tile(math.gcd(t, n_ctx), (256, 128))
    tk = _pick_tile(t, (768, 512, 256, 128))
    gw = groups * hd
    return pl.pallas_call(
        functools.partial(_flash_kernel, groups=groups, hd=hd, tk=tk, n_ctx=n_ctx),
        out_shape=jax.ShapeDtypeStruct((t, kv_heads * gw), BF16),
        grid=(kv_heads, t // tq),
        in_specs=[pl.BlockSpec((tq, gw), lambda g, i: (i, g)),
                  pl.BlockSpec((t, hd), lambda g, i: (0, g)),
                  pl.BlockSpec((t, hd), lambda g, i: (0, g))],
        out_specs=pl.BlockSpec((tq, gw), lambda g, i: (i, g)),
        scratch_shapes=[pltpu.VMEM((groups * tq, LANES), F32),
                        pltpu.VMEM((groups * tq, LANES), F32),
                        pltpu.VMEM((groups * tq, hd), F32),
                        pltpu.VMEM((groups * tq, tk), F32),
                        pltpu.VMEM((groups * tq, tk), F32)],
        compiler_params=_cparams(("arbitrary", "arbitrary")),
        name="flash_attention",
    )(q, k, v)


def _router_kernel(x_ref, g_ref, sh_ref, sc_ref, wr_ref, br_ref, h_ref, ids_ref, wts_ref, cnt_ref, run_ref,
                   *, n_ctx, row0):
    i = pl.program_id(0)
    tm, d = x_ref.shape

    @pl.when(i == 0)
    def _():
        run_ref[...] = jnp.zeros_like(run_ref)

    ib = i + row0 // tm
    h = _rms(x_ref[...]) * g_ref[...]
    h = h * (1.0 + _mod_row(sc_ref, ib, tm, n_ctx)) + _mod_row(sh_ref, ib, tm, n_ctx)
    h_ref[...] = h
    h_hi = h.astype(BF16)
    h_lo = (h - h_hi.astype(F32)).astype(BF16)
    hw = jnp.dot(h_hi, wr_ref[...], preferred_element_type=F32)
    lw = jnp.dot(h_lo, wr_ref[:, :LANES], preferred_element_type=F32)
    logits = hw[:, :LANES] + (hw[:, LANES:] + lw) + br_ref[...]
    lane = lax.broadcasted_iota(I32, (tm, LANES), 1)
    big = jnp.int32(LANES)
    gl = jnp.where((lane >= MOE_EXPERTS) & (lane < MOE_EXPERTS + MOE_GROUPS), logits, NEG_INF)
    g_max = jnp.max(gl, axis=-1, keepdims=True)
    g_idx = jnp.min(jnp.where(gl == g_max, lane, big), axis=-1, keepdims=True) - MOE_EXPERTS
    g_w = 1.0 / jnp.sum(jnp.exp(gl - g_max), axis=-1, keepdims=True)
    el = jnp.where((lane < MOE_EXPERTS) & ((lane >> 3) == g_idx), logits, NEG_INF)
    m1 = jnp.max(el, axis=-1, keepdims=True)
    i1 = jnp.min(jnp.where(el == m1, lane, big), axis=-1, keepdims=True)
    el2 = jnp.where(lane == i1, NEG_INF, el)
    m2 = jnp.max(el2, axis=-1, keepdims=True)
    i2 = jnp.min(jnp.where(el2 == m2, lane, big), axis=-1, keepdims=True)
    e_sum = jnp.sum(jnp.exp(el - m1), axis=-1, keepdims=True)
    p1 = 1.0 / e_sum
    p2 = jnp.exp(m2 - m1) / e_sum
    w1 = g_w * (p1 / (p1 + p2))
    w2 = g_w * (p2 / (p1 + p2))
    onehot = jnp.where((lane == i1) | (lane == i2), 1.0, 0.0)
    ri = lax.broadcasted_iota(I32, (tm, tm), 0)
    ci = lax.broadcasted_iota(I32, (tm, tm), 1)
    strict = jnp.where(ci < ri, 1.0, 0.0).astype(BF16)
    before = jnp.dot(strict, onehot.astype(BF16), preferred_element_type=F32) + run_ref[0:1, :]
    r1 = jnp.sum(jnp.where(lane == i1, before, 0.0), axis=-1, keepdims=True).astype(I32)
    r2 = jnp.sum(jnp.where(lane == i2, before, 0.0), axis=-1, keepdims=True).astype(I32)
    ids_ref[...] = jnp.where(lane == 0, i1, jnp.where(lane == 1, i2, jnp.where(lane == 2, r1, jnp.where(lane == 3, r2, 0))))
    wts_ref[...] = jnp.where(lane == 0, w1, jnp.where(lane == 1, w2, 0.0))
    total = run_ref[0:1, :] + jnp.sum(onehot, axis=0, keepdims=True)
    run_ref[...] = jnp.broadcast_to(total, run_ref.shape)
    cnt_ref[...] = jnp.broadcast_to(total, cnt_ref.shape)


def _router(x, g, mod, k_shift, k_scale, w_router, b_router, n_ctx, row0):
    t, d = x.shape
    tm = ROW_TILE
    n = t - row0
    off = row0 // tm
    return pl.pallas_call(
        functools.partial(_router_kernel, n_ctx=n_ctx, row0=row0),
        out_shape=(jax.ShapeDtypeStruct((n, d), F32),
                   jax.ShapeDtypeStruct((n, LANES), I32),
                   jax.ShapeDtypeStruct((n, LANES), F32),
                   jax.ShapeDtypeStruct((8, LANES), F32)),
        grid=(n // tm,),
        in_specs=[pl.BlockSpec((tm, d), lambda i: (i + off, 0)),
                  pl.BlockSpec((1, d), lambda i: (0, 0)),
                  pl.BlockSpec((8, d), lambda i: (0, k_shift)),
                  pl.BlockSpec((8, d), lambda i: (0, k_scale)),
                  pl.BlockSpec((d, 2 * LANES), lambda i: (0, 0)),
                  pl.BlockSpec((1, LANES), lambda i: (0, 0))],
        out_specs=(pl.BlockSpec((tm, d), lambda i: (i, 0)),
                   pl.BlockSpec((tm, LANES), lambda i: (i, 0)),
                   pl.BlockSpec((tm, LANES), lambda i: (i, 0)),
                   pl.BlockSpec((8, LANES), lambda i: (0, 0))),
        scratch_shapes=[pltpu.VMEM((8, LANES), F32)],
        compiler_params=_cparams(("arbitrary",)),
        name="moe_router",
    )(x, g.reshape(1, d), mod, mod, w_router, b_router)


def _dispatch_kernel(dest_ref, h_ref, xs_in_ref, xs_ref, sem):
    del xs_in_ref
    i = pl.program_id(0)
    tm = h_ref.shape[0]

    def row_copy(r, k):
        slot = dest_ref[(i * tm + r) * MOE_TOPK + k]
        return pltpu.make_async_copy(h_ref.at[pl.ds(r, 1)], xs_ref.at[pl.ds(slot, 1)], sem)

    def start(r, carry):
        for k in range(MOE_TOPK):
            row_copy(r, k).start()
        return carry

    def wait(r, carry):
        for k in range(MOE_TOPK):
            row_copy(r, k).wait()
        return carry

    lax.fori_loop(0, tm, start, 0, unroll=8)
    lax.fori_loop(0, tm, wait, 0, unroll=8)


def _dispatch(dest, h, xs_init):
    n, w = h.shape
    tm = ROW_TILE
    return pl.pallas_call(
        _dispatch_kernel,
        out_shape=jax.ShapeDtypeStruct(xs_init.shape, xs_init.dtype),
        grid_spec=pltpu.PrefetchScalarGridSpec(
            num_scalar_prefetch=1,
            grid=(n // tm,),
            in_specs=[pl.BlockSpec((tm, w), lambda i, dest: (i, 0)),
                      pl.BlockSpec(memory_space=pl.ANY)],
            out_specs=pl.BlockSpec(memory_space=pl.ANY),
            scratch_shapes=[pltpu.SemaphoreType.DMA(())]),
        input_output_aliases={2: 0},
        compiler_params=_cparams(("arbitrary",)),
        name="moe_dispatch",
    )(dest, h, xs_init)


def _expert_kernel(be_ref, first_ref, next_ref, nb_ref, xs_ref, wg_hbm, wu_hbm, wd_hbm, ys_ref,
                   wg_st, wu_st, wd_st, wg_bf, wu_bf, wd_bf, sem, *, layer):
    b = pl.program_id(0)
    used = b < nb_ref[0]

    weights = ((wg_hbm, wg_st, wg_bf), (wu_hbm, wu_st, wu_bf), (wd_hbm, wd_st, wd_bf))

    def fetch(e, idx):
        hbm, st, _ = weights[idx]
        return pltpu.make_async_copy(hbm.at[layer, e], st, sem.at[idx])

    @pl.when(b == 0)
    def _():
        for idx in range(len(weights)):
            fetch(be_ref[0], idx).start()

    @pl.when(used & (first_ref[b] == 1))
    def _():
        has_next = next_ref[b] >= 0
        for idx, (_, st, bf) in enumerate(weights):
            fetch(be_ref[b], idx).wait()
            bf[...] = st[...].astype(BF16)

            @pl.when(has_next)
            def _():
                fetch(next_ref[b], idx).start()

    @pl.when(used)
    def _():
        x = xs_ref[...].astype(BF16)
        gate = jnp.dot(x, wg_bf[...], preferred_element_type=F32)
        up = jnp.dot(x, wu_bf[...], preferred_element_type=F32)
        act = (gate * (1.0 / (1.0 + jnp.exp(-gate))) * up).astype(BF16)
        ys_ref[...] = jnp.dot(act, wd_bf[...], preferred_element_type=F32)

    @pl.when(jnp.logical_not(used))
    def _():
        ys_ref[...] = jnp.zeros_like(ys_ref)


def _experts(block_expert, first, next_expert, n_used, xs, w_gate, w_up, w_down, layer):
    n_slots, d = xs.shape
    ff = w_gate.shape[3]
    nb = n_slots // MOE_BLOCK
    row_spec = pl.BlockSpec((MOE_BLOCK, d), lambda b, *_: (b, 0))
    hbm = pl.BlockSpec(memory_space=pl.ANY)
    return pl.pallas_call(
        functools.partial(_expert_kernel, layer=layer),
        out_shape=jax.ShapeDtypeStruct((n_slots, d), F32),
        grid_spec=pltpu.PrefetchScalarGridSpec(
            num_scalar_prefetch=4,
            grid=(nb,),
            in_specs=[row_spec, hbm, hbm, hbm],
            out_specs=row_spec,
            scratch_shapes=[pltpu.VMEM((d, ff), F32), pltpu.VMEM((d, ff), F32), pltpu.VMEM((ff, d), F32),
                            pltpu.VMEM((d, ff), BF16), pltpu.VMEM((d, ff), BF16), pltpu.VMEM((ff, d), BF16),
                            pltpu.SemaphoreType.DMA((3,))]),
        compiler_params=_cparams(("arbitrary",)),
        name="moe_experts",
    )(block_expert, first, next_expert, n_used, xs, w_gate, w_up, w_down)


def _combine_kernel(dest_ref, x_ref, wts_ref, mod_ref, ys_ref, g_ref, sh_ref, sc_ref, *rest, n_ctx, row0, last):
    if last:
        o_ref, buf, sem = rest
    else:
        o_ref, h_ref, buf, sem = rest
    i = pl.program_id(0)
    tm = x_ref.shape[0]

    def row_copy(r, k):
        slot = dest_ref[(i * tm + r) * MOE_TOPK + k]
        return pltpu.make_async_copy(ys_ref.at[pl.ds(slot, 1)], buf.at[k, pl.ds(r, 1)], sem)

    def start(r, carry):
        for k in range(MOE_TOPK):
            row_copy(r, k).start()
        return carry

    def wait(r, carry):
        for k in range(MOE_TOPK):
            row_copy(r, k).wait()
        return carry

    lax.fori_loop(0, tm, start, 0, unroll=8)
    lax.fori_loop(0, tm, wait, 0, unroll=8)
    w = wts_ref[...]
    y = w[:, 0:1] * buf[0] + w[:, 1:2] * buf[1]
    ib = i + row0 // tm
    x_new = x_ref[...] + _mod_row(mod_ref, ib, tm, n_ctx) * y
    hn = _rms(x_new) * g_ref[...]
    if last:
        o_ref[...] = hn
    else:
        o_ref[...] = x_new
        hn = hn * (1.0 + _mod_row(sc_ref, ib, tm, n_ctx)) + _mod_row(sh_ref, ib, tm, n_ctx)
        h_ref[...] = hn.astype(h_ref.dtype)


def _combine(dest, x, wts, mod, k_gate, ys, n_ctx, row0, g_next, mod_next, last):
    t, d = x.shape
    tm = ROW_TILE
    n = t - row0
    off = row0 // tm
    row_spec = pl.BlockSpec((tm, d), lambda i, dest: (i, 0))
    out_shape = jax.ShapeDtypeStruct((n, d), F32)
    out_specs = row_spec
    if not last:
        out_shape = (out_shape, jax.ShapeDtypeStruct((n, d), BF16))
        out_specs = (row_spec, row_spec)
    return pl.pallas_call(
        functools.partial(_combine_kernel, n_ctx=n_ctx, row0=row0, last=last),
        out_shape=out_shape,
        grid_spec=pltpu.PrefetchScalarGridSpec(
            num_scalar_prefetch=1,
            grid=(n // tm,),
            in_specs=[pl.BlockSpec((tm, d), lambda i, dest: (i + off, 0)),
                      pl.BlockSpec((tm, LANES), lambda i, dest: (i, 0)),
                      pl.BlockSpec((8, d), lambda i, dest: (0, k_gate)),
                      pl.BlockSpec(memory_space=pl.ANY),
                      pl.BlockSpec((1, d), lambda i, dest: (0, 0)),
                      pl.BlockSpec((8, d), lambda i, dest: (0, 0)),
                      pl.BlockSpec((8, d), lambda i, dest: (0, 1))],
            out_specs=out_specs,
            scratch_shapes=[pltpu.VMEM((MOE_TOPK, tm, d), F32), pltpu.SemaphoreType.DMA(())]),
        compiler_params=_cparams(("arbitrary",)),
        name="moe_combine",
    )(dest, x, wts, mod, ys, g_next.reshape(1, d), mod_next, mod_next)


def _hier_moe(x, g, mod, w_rg, b_rg, w_re, b_re, w_gate, w_up, w_down, layer, n_ctx, row0, xs_buf,
              g_next, mod_next, last):
    t, d = x.shape
    n = t - row0
    pad = LANES - MOE_EXPERTS - MOE_GROUPS
    w_router = jnp.concatenate([w_re, w_rg, jnp.zeros((d, pad), F32)], axis=1)
    w_hi = w_router.astype(BF16)
    w_router = jnp.concatenate([w_hi, (w_router - w_hi.astype(F32)).astype(BF16)], axis=1)
    b_router = jnp.concatenate([b_re, b_rg, jnp.zeros((pad,), F32)]).reshape(1, LANES)
    h, ids, wts, counts = _router(x, g, mod, 3, 4, w_router, b_router, n_ctx, row0)
    counts = counts[0, :MOE_EXPERTS].astype(I32)
    padded = (counts + MOE_BLOCK - 1) // MOE_BLOCK * MOE_BLOCK
    seg_end = jnp.cumsum(padded)
    seg_start = seg_end - padded
    dest = (seg_start[ids[:, :MOE_TOPK]] + ids[:, MOE_TOPK:2 * MOE_TOPK]).reshape(-1)
    n_blocks = (t * MOE_TOPK + MOE_EXPERTS * (MOE_BLOCK - 1)) // MOE_BLOCK + 1
    block_start = jnp.arange(n_blocks, dtype=I32) * MOE_BLOCK
    block_expert = jnp.minimum(jnp.sum((seg_end[None, :] <= block_start[:, None]).astype(I32), axis=1),
                               MOE_EXPERTS - 1).astype(I32)
    n_used = (seg_end[-1:] // MOE_BLOCK).astype(I32)
    block_id = jnp.arange(n_blocks, dtype=I32)
    prev_expert = jnp.concatenate([jnp.full((1,), -1, I32), block_expert[:-1]])
    first = ((block_id < n_used[0]) & (block_expert != prev_expert)).astype(I32)
    eid = jnp.arange(MOE_EXPERTS, dtype=I32)
    owner = jnp.where(padded > 0, eid, MOE_EXPERTS)
    later = jnp.concatenate([lax.cummin(owner[::-1])[::-1][1:], jnp.full((1,), MOE_EXPERTS, I32)])
    next_expert = jnp.where(later < MOE_EXPERTS, later, -1)[block_expert].astype(I32)
    if xs_buf is None:
        xs_buf = jnp.zeros((n_blocks * MOE_BLOCK, d), F32)
    xs = _dispatch(dest, h, xs_buf)
    ys = _experts(block_expert, first, next_expert, n_used, xs, w_gate, w_up, w_down, layer)
    return _combine(dest, x, wts, mod, 5, ys, n_ctx, row0, g_next, mod_next, last), xs


def _rope_table(pos, dim, base):
    inv = jnp.power(base, -jnp.arange(0, dim, 2, dtype=F32) / dim)
    ang = pos.astype(F32)[:, None] * inv[None, :]
    return jnp.cos(ang), jnp.sin(ang)


def _flipped_positions(t, n_ctx):
    idx = jnp.arange(t, dtype=I32)
    return jnp.where(idx < n_ctx, n_ctx - 1 - idx, n_ctx + (t - 1 - idx))


def _axial_tables(n_ctx, n_lat, hd):
    rows = n_lat // GRID_W
    row = jnp.repeat(jnp.arange(rows, dtype=I32), GRID_W)
    col = jnp.tile(jnp.arange(GRID_W, dtype=I32), rows)
    half = hd // 2
    cr, sr = _rope_table(row, half, ROPE_BASE)
    cc, sc = _rope_table(col, half, ROPE_BASE)
    cos = jnp.concatenate([cr, cr, cc, cc], axis=1)
    sin = jnp.concatenate([-sr, sr, -sc, sc], axis=1)
    cos = jnp.concatenate([jnp.ones((n_ctx, hd), F32), cos], axis=0)
    sin = jnp.concatenate([jnp.zeros((n_ctx, hd), F32), sin], axis=0)
    return cos, sin


def _retention_layer(xa, h, mod, w_in, logit_gamma, gn_w, gn_b, w_out, j, n_ctx):
    t, d = xa.shape
    heads = RET_HEADS
    dk = d // heads
    dv = 2 * dk
    qk_w, v_w = heads * dk, heads * dv
    pos_f = jnp.arange(t, dtype=I32)
    cf, sf = _rope_table(pos_f, dk, RET_ROPE_BASE)
    cb, sb = _rope_table(_flipped_positions(t, n_ctx), dk, RET_ROPE_BASE)
    tables = (cf, sf, cb, sb)
    q = _matmul_rope(h, w_in, j, 0, qk_w, tables, dk ** -0.5, dk)
    k = _matmul_rope(h, w_in, j, qk_w, qk_w, tables, 1.0, dk)
    vg = _matmul(h, w_in, j, 2 * qk_w, 2 * v_w, BF16)
    log_gamma = jax.nn.log_sigmoid(logit_gamma.astype(F32))
    yf = _retention_scan(log_gamma[0], q, k, vg, 0, n_ctx, heads, dk, dv, reverse=False)
    yb = _retention_scan(log_gamma[1], q, k, vg, 1, n_ctx, heads, dk, dv, reverse=True)
    yn = _ret_post(yf, yb, vg, gn_w, gn_b, heads, dv)
    return _matmul_residual(yn, w_out, j, xa, mod, 2, n_ctx)


def _mlstm_layer(xa, h, mod, w_in, conv_w, b_gate, norm_w, w_out, j, n_ctx):
    t, d = xa.shape
    heads = MLSTM_HEADS
    dk = d // (2 * heads)
    dv = d // heads
    qk_w, v_w = heads * dk, heads * dv
    n_gates = 4 * heads
    qk_pre = _matmul(h, w_in, j, 0, 2 * qk_w, F32)
    vo = _matmul(h, w_in, j, 2 * qk_w, 2 * v_w, BF16)
    w_gates = jnp.pad(w_in[j, :, 2 * qk_w + 2 * v_w:], ((0, 0), (0, LANES - n_gates)))
    gates = _matmul(h, w_gates[None], 0, 0, LANES, F32)
    bias = jnp.pad(b_gate.astype(F32).reshape(1, n_gates), ((0, 0), (0, LANES - n_gates)))
    qk = _conv_silu(qk_pre, conv_w, n_ctx, qk_w, dk ** -0.5)
    yf = _mlstm_scan(qk, vo, gates, bias, n_ctx, heads, dk, dv, reverse=False)
    yb = _mlstm_scan(qk, vo, gates, bias, n_ctx, heads, dk, dv, reverse=True)
    yn = _mlstm_post(yf, yb, vo, norm_w, heads, dv)
    return _matmul_residual(yn, w_out, j, xa, mod, 2, n_ctx)


def _attention_layer(xa, h, mod, w_in, q_norm, k_norm, w_out, j, n_ctx):
    t, d = xa.shape
    hd = d // ATTN_HEADS
    groups = ATTN_HEADS // ATTN_KV_HEADS
    q_w, kv_w = ATTN_HEADS * hd, ATTN_KV_HEADS * hd
    n_lat = t - n_ctx
    qk_pre = _matmul(h, w_in, j, 0, q_w + kv_w, F32)
    v = _matmul(h, w_in, j, q_w + kv_w, kv_w, BF16)
    cos, sin = _axial_tables(n_ctx, n_lat, hd)
    q_gain = q_norm * (hd ** -0.5 * math.log2(math.e))
    q, k = _qk_norm_rope(qk_pre, cos, sin, q_gain, k_norm, ATTN_HEADS, ATTN_KV_HEADS, hd)
    o = _flash_attention(q, k, v, n_ctx, ATTN_KV_HEADS, groups, hd)
    return _matmul_residual(o, w_out, j, xa, mod, 2, n_ctx)


def kernel(x, c, ctx, c_ctx, w_ada, b_ada, norm_g, ret_w_in, ret_logit_gamma, ret_gn_w, ret_gn_b, ret_w_out, mlstm_w_in, mlstm_conv_w, mlstm_b_gate, mlstm_norm_w, mlstm_w_out, attn_w_in, attn_q_norm, attn_k_norm, attn_w_out, moe_w_router_group, moe_b_router_group, moe_w_router_expert, moe_b_router_expert, moe_w_gate, moe_w_up, moe_w_down, final_norm_g):
    bsz, n_lat, d = x.shape
    n_ctx = ctx.shape[1]
    depth = w_ada.shape[0]
    assert bsz == 1 and n_ctx % ROW_TILE == 0 and n_lat % ROW_TILE == 0 and n_lat % GRID_W == 0
    xa = jnp.concatenate([ctx[0], x[0]], axis=0)
    s = jnp.stack([jax.nn.silu(c[0]), jax.nn.silu(c_ctx)])
    s8 = jnp.zeros((8, d), F32).at[:2].set(s).astype(BF16)
    mods = _ada_modulation(s8, w_ada, b_ada)
    xs_buf = None
    h = _norm_mod(xa, norm_g[0, 0], mods[0], 0, 1, n_ctx)
    for i in range(depth):
        kind, j = i % 3, i // 3
        mod = mods[i]
        last = i == depth - 1
        if kind == 0:
            xa = _retention_layer(xa, h, mod, ret_w_in, ret_logit_gamma[j], ret_gn_w[j], ret_gn_b[j],
                                  ret_w_out, j, n_ctx)
        elif kind == 1:
            xa = _mlstm_layer(xa, h, mod, mlstm_w_in, mlstm_conv_w[j], mlstm_b_gate[j],
                              mlstm_norm_w[j], mlstm_w_out, j, n_ctx)
        else:
            xa = _attention_layer(xa, h, mod, attn_w_in, attn_q_norm[j], attn_k_norm[j],
                                  attn_w_out, j, n_ctx)
        row0 = n_ctx if last else 0
        g_next = final_norm_g if last else norm_g[i + 1, 0]
        mod_next = mod if last else mods[i + 1]
        out, xs_buf = _hier_moe(xa, norm_g[i, 1], mod, moe_w_router_group[i], moe_b_router_group[i],
                                moe_w_router_expert[i], moe_b_router_expert[i], moe_w_gate, moe_w_up, moe_w_down, i,
                                n_ctx, row0, xs_buf, g_next, mod_next, last)
        if last:
            return out[None]
        xa, h = out
```

```python
import functools
import math

import jax
import jax.numpy as jnp
import numpy as np
from jax import lax
from jax.experimental import pallas as pl
from jax.experimental.pallas import tpu as pltpu

F32 = jnp.float32
BF16 = jnp.bfloat16
I32 = jnp.int32
U32 = jnp.uint32

LANES = 128
CHUNK = 128
NORM_EPS = 1e-6
GRID_W = 64
RET_HEADS = 8
RET_ROPE_BASE = 10000.0
MLSTM_HEADS = 8
GATE_SOFTCAP = 15.0
ATTN_HEADS = 16
ATTN_KV_HEADS = 8
ROPE_BASE = 10000.0
MOE_GROUPS = 4
MOE_PER_GROUP = 8
MOE_EXPERTS = MOE_GROUPS * MOE_PER_GROUP
MOE_TOPK = 2
MOE_BLOCK = 256
ROW_TILE = 256
FLASH_BAND = 32
VMEM_LIMIT = 56 * 1024 * 1024
NEG_INF = float("-inf")


def _cparams(sem, vmem=VMEM_LIMIT):
    return pltpu.CompilerParams(dimension_semantics=sem, vmem_limit_bytes=vmem)


def _pick_tile(total, candidates):
    for c in candidates:
        if total % c == 0:
            return c
    raise ValueError(f"no tile for {total} in {candidates}")


def _mod_row(mod_ref, i, tile, n_ctx):
    sel = jnp.where(i < n_ctx // tile, 1, 0)
    return mod_ref[pl.ds(sel, 1), :]


def _ada_kernel(s_ref, w_ref, b_ref, o_ref):
    w = w_ref[0].astype(BF16)
    o_ref[0] = jnp.dot(s_ref[...], w, preferred_element_type=F32) + b_ref[0]


def _ada_modulation(s8, w_ada, b_ada):
    depth, d, n = w_ada.shape
    tn = _pick_tile(n, (1024, 512, 256, 128))
    return pl.pallas_call(
        _ada_kernel,
        out_shape=jax.ShapeDtypeStruct((depth, 8, n), F32),
        grid=(depth, n // tn),
        in_specs=[pl.BlockSpec((8, d), lambda l, j: (0, 0)),
                  pl.BlockSpec((1, d, tn), lambda l, j: (l, 0, j)),
                  pl.BlockSpec((1, 1, tn), lambda l, j: (l, 0, j))],
        out_specs=pl.BlockSpec((1, 8, tn), lambda l, j: (l, 0, j)),
        compiler_params=_cparams(("arbitrary", "arbitrary")),
        name="ada_modulation",
    )(s8, w_ada, b_ada.reshape(depth, 1, n))


def _rms(x):
    return x * lax.rsqrt(jnp.mean(x * x, axis=-1, keepdims=True) + NORM_EPS)


def _norm_mod_kernel(x_ref, g_ref, sh_ref, sc_ref, o_ref, *, n_ctx):
    i = pl.program_id(0)
    tm = x_ref.shape[0]
    y = _rms(x_ref[...]) * g_ref[...]
    y = y * (1.0 + _mod_row(sc_ref, i, tm, n_ctx)) + _mod_row(sh_ref, i, tm, n_ctx)
    o_ref[...] = y.astype(o_ref.dtype)


def _norm_mod(x, g, mod, k_shift, k_scale, n_ctx):
    t, d = x.shape
    tm = ROW_TILE
    return pl.pallas_call(
        functools.partial(_norm_mod_kernel, n_ctx=n_ctx),
        out_shape=jax.ShapeDtypeStruct((t, d), BF16),
        grid=(t // tm,),
        in_specs=[pl.BlockSpec((tm, d), lambda i: (i, 0)),
                  pl.BlockSpec((1, d), lambda i: (0, 0)),
                  pl.BlockSpec((8, d), lambda i: (0, k_shift)),
                  pl.BlockSpec((8, d), lambda i: (0, k_scale))],
        out_specs=pl.BlockSpec((tm, d), lambda i: (i, 0)),
        compiler_params=_cparams(("arbitrary",)),
        name="norm_mod",
    )(x, g.reshape(1, d), mod, mod)


def _cast_weight_once(w_ref, wbf_ref):
    @pl.when(pl.program_id(1) == 0)
    def _():
        wbf_ref[...] = w_ref[...].astype(BF16)


def _mm_plain_kernel(a_ref, w_ref, o_ref, wbf_ref, *, scale):
    _cast_weight_once(w_ref, wbf_ref)
    acc = jnp.dot(a_ref[...], wbf_ref[...], preferred_element_type=F32)
    if scale != 1.0:
        acc = acc * scale
    o_ref[...] = acc.astype(o_ref.dtype)


def _mm_residual_kernel(a_ref, w_ref, x_ref, mod_ref, o_ref, wbf_ref, *, n_ctx):
    _cast_weight_once(w_ref, wbf_ref)
    tm = a_ref.shape[0]
    acc = jnp.dot(a_ref[...], wbf_ref[...], preferred_element_type=F32)
    row = pl.program_id(1) * tm + lax.broadcasted_iota(I32, (tm, 1), 0)
    gate = jnp.where(row < n_ctx, mod_ref[1:2, :], mod_ref[0:1, :])
    o_ref[...] = x_ref[...] + gate * acc


def _mm_rope_kernel(a_ref, w_ref, cf_ref, sf_ref, cb_ref, sb_ref, o_ref, wbf_ref, *, scale, head_dim):
    _cast_weight_once(w_ref, wbf_ref)
    acc = jnp.dot(a_ref[...], wbf_ref[...], preferred_element_type=F32)
    if scale != 1.0:
        acc = acc * scale
    half = head_dim // 2
    for d, (c_ref, s_ref) in enumerate(((cf_ref, sf_ref), (cb_ref, sb_ref))):
        c, s = c_ref[...], s_ref[...]
        for h in range(acc.shape[1] // head_dim):
            x1 = acc[:, h * head_dim:h * head_dim + half]
            x2 = acc[:, h * head_dim + half:(h + 1) * head_dim]
            o_ref[d, :, h * head_dim:h * head_dim + half] = (x1 * c - x2 * s).astype(o_ref.dtype)
            o_ref[d, :, h * head_dim + half:(h + 1) * head_dim] = (x1 * s + x2 * c).astype(o_ref.dtype)


def _mm_tiles(m, k, n):
    tm = _pick_tile(m, (768, 512, 384, 256, 128))
    tn_cap = 1024 if k <= 2048 else 512
    tn = _pick_tile(n, tuple(c for c in (1024, 512, 256, 128) if c <= tn_cap))
    return tm, tn


def _matmul(a, w, layer, col0, n, out_dtype, scale=1.0):
    m, k = a.shape
    tm, tn = _mm_tiles(m, k, n)
    assert col0 % tn == 0
    c0 = col0 // tn
    return pl.pallas_call(
        functools.partial(_mm_plain_kernel, scale=scale),
        out_shape=jax.ShapeDtypeStruct((m, n), out_dtype),
        grid=(n // tn, m // tm),
        in_specs=[pl.BlockSpec((tm, k), lambda j, i: (i, 0)),
                  pl.BlockSpec((None, k, tn), lambda j, i: (layer, 0, j + c0))],
        out_specs=pl.BlockSpec((tm, tn), lambda j, i: (i, j)),
        scratch_shapes=[pltpu.VMEM((k, tn), BF16)],
        compiler_params=_cparams(("arbitrary", "arbitrary")),
        name="matmul",
    )(a, w)


def _matmul_residual(a, w, layer, x, mod, k_gate, n_ctx):
    m, k = a.shape
    n = w.shape[2]
    tm, tn = _mm_tiles(m, k, n)
    kb = k_gate * (n // tn)
    return pl.pallas_call(
        functools.partial(_mm_residual_kernel, n_ctx=n_ctx),
        out_shape=jax.ShapeDtypeStruct((m, n), F32),
        grid=(n // tn, m // tm),
        in_specs=[pl.BlockSpec((tm, k), lambda j, i: (i, 0)),
                  pl.BlockSpec((None, k, tn), lambda j, i: (layer, 0, j)),
                  pl.BlockSpec((tm, tn), lambda j, i: (i, j)),
                  pl.BlockSpec((8, tn), lambda j, i: (0, kb + j))],
        out_specs=pl.BlockSpec((tm, tn), lambda j, i: (i, j)),
        scratch_shapes=[pltpu.VMEM((k, tn), BF16)],
        compiler_params=_cparams(("arbitrary", "arbitrary")),
        name="matmul_residual",
    )(a, w, x, mod)


def _matmul_rope(a, w, layer, col0, n, tables, scale, head_dim):
    m, k = a.shape
    tm, tn = _mm_tiles(m, k, n)
    assert col0 % tn == 0 and tn % head_dim == 0
    c0 = col0 // tn
    half = head_dim // 2
    tab_spec = pl.BlockSpec((tm, half), lambda j, i: (i, 0))
    return pl.pallas_call(
        functools.partial(_mm_rope_kernel, scale=scale, head_dim=head_dim),
        out_shape=jax.ShapeDtypeStruct((2, m, n), BF16),
        grid=(n // tn, m // tm),
        in_specs=[pl.BlockSpec((tm, k), lambda j, i: (i, 0)),
                  pl.BlockSpec((None, k, tn), lambda j, i: (layer, 0, j + c0)),
                  tab_spec, tab_spec, tab_spec, tab_spec],
        out_specs=pl.BlockSpec((2, tm, tn), lambda j, i: (0, i, j)),
        scratch_shapes=[pltpu.VMEM((k, tn), BF16)],
        compiler_params=_cparams(("arbitrary", "arbitrary")),
        name="matmul_rope",
    )(a, w, *tables)


def _chunk_index(c, n_chunks, ctx_chunks, reverse):
    if not reverse:
        return c
    return jnp.where(c < ctx_chunks, ctx_chunks - 1 - c, n_chunks - 1 - (c - ctx_chunks))


def _dot_nt(a, b):
    return lax.dot_general(a, b, (((1,), (1,)), ((), ())), preferred_element_type=F32)


def _dot_tn(a, b):
    return lax.dot_general(a, b, (((0,), (0,)), ((), ())), preferred_element_type=F32)


def _retention_kernel(lg_ref, q_ref, k_ref, v_ref, o_ref, s_ref, *, reverse, heads, dk, dv):
    c = pl.program_id(0)

    @pl.when(c == 0)
    def _():
        s_ref[...] = jnp.zeros_like(s_ref)

    n = q_ref.shape[0]
    ri = lax.broadcasted_iota(I32, (n, n), 0)
    ci = lax.broadcasted_iota(I32, (n, n), 1)
    diff = (ci - ri) if reverse else (ri - ci)
    dmask = diff >= 0
    dist = jnp.where(dmask, diff, 0).astype(F32)
    r1 = lax.broadcasted_iota(I32, (n, 1), 0)
    pos = ((n - 1 - r1) if reverse else r1).astype(F32)
    for h in range(heads):
        lg = jnp.full((1, 1), lg_ref[h], F32)
        decay = jnp.where(dmask, jnp.exp(dist * lg), 0.0)
        q_dec = jnp.exp((pos + 1.0) * lg)
        k_dec = jnp.exp((n - 1.0 - pos) * lg)
        c_dec = jnp.exp(float(n) * lg)
        q = q_ref[:, h * dk:(h + 1) * dk]
        k = k_ref[:, h * dk:(h + 1) * dk]
        v = v_ref[:, h * dv:(h + 1) * dv]
        s_prev = s_ref[h]
        scores = (_dot_nt(q, k) * decay).astype(BF16)
        inner = jnp.dot(scores, v, preferred_element_type=F32)
        cross = jnp.dot(q, s_prev.astype(BF16), preferred_element_type=F32)
        o_ref[:, h * dv:(h + 1) * dv] = (inner + q_dec * cross).astype(o_ref.dtype)
        kd = (k.astype(F32) * k_dec).astype(BF16)
        s_ref[h] = c_dec * s_prev + _dot_tn(kd, v)


def _retention_scan(log_gamma, q, k, vg, d_idx, n_ctx, heads, dk, dv, reverse):
    t = q.shape[1]
    nc, cc = t // CHUNK, n_ctx // CHUNK
    cm = lambda c: _chunk_index(c, nc, cc, reverse)
    return pl.pallas_call(
        functools.partial(_retention_kernel, reverse=reverse, heads=heads, dk=dk, dv=dv),
        out_shape=jax.ShapeDtypeStruct((t, heads * dv), BF16),
        grid=(nc,),
        in_specs=[pl.BlockSpec(memory_space=pltpu.SMEM),
                  pl.BlockSpec((None, CHUNK, heads * dk), lambda c: (d_idx, cm(c), 0)),
                  pl.BlockSpec((None, CHUNK, heads * dk), lambda c: (d_idx, cm(c), 0)),
                  pl.BlockSpec((CHUNK, heads * dv), lambda c: (cm(c), 0))],
        out_specs=pl.BlockSpec((CHUNK, heads * dv), lambda c: (cm(c), 0)),
        scratch_shapes=[pltpu.VMEM((heads, dk, dv), F32)],
        compiler_params=_cparams(("arbitrary",)),
        name="retention_scan_rev" if reverse else "retention_scan_fwd",
    )(log_gamma, q, k, vg)


def _ret_post_kernel(yf_ref, yb_ref, g_ref, w_ref, b_ref, o_ref, *, heads, dv):
    for h in range(heads):
        sl = slice(h * dv, (h + 1) * dv)
        y = yf_ref[:, sl].astype(F32) + yb_ref[:, sl].astype(F32)
        mu = jnp.mean(y, axis=-1, keepdims=True)
        yc = y - mu
        var = jnp.mean(yc * yc, axis=-1, keepdims=True)
        yn = yc * lax.rsqrt(var + NORM_EPS)
        g = g_ref[:, sl].astype(F32)
        silu = g * (1.0 / (1.0 + jnp.exp(-g)))
        o_ref[:, sl] = ((yn * w_ref[:, sl] + b_ref[:, sl]) * silu).astype(o_ref.dtype)


def _ret_post(yf, yb, vg, gn_w, gn_b, heads, dv):
    t, vw = yf.shape
    tm = ROW_TILE
    return pl.pallas_call(
        functools.partial(_ret_post_kernel, heads=heads, dv=dv),
        out_shape=jax.ShapeDtypeStruct((t, vw), BF16),
        grid=(t // tm,),
        in_specs=[pl.BlockSpec((tm, vw), lambda i: (i, 0)),
                  pl.BlockSpec((tm, vw), lambda i: (i, 0)),
                  pl.BlockSpec((tm, vw), lambda i: (i, 1)),
                  pl.BlockSpec((1, vw), lambda i: (0, 0)),
                  pl.BlockSpec((1, vw), lambda i: (0, 0))],
        out_specs=pl.BlockSpec((tm, vw), lambda i: (i, 0)),
        compiler_params=_cparams(("arbitrary",)),
        name="retention_post",
    )(yf, yb, vg, gn_w.reshape(1, vw), gn_b.reshape(1, vw))


def _conv_silu_kernel(x_ref, prev_ref, next_ref, w_ref, o_ref, *, n_ctx, t_total, q_cols, q_scale):
    i = pl.program_id(0)
    tm = x_ref.shape[0]
    x = x_ref[...]
    row = lax.broadcasted_iota(I32, (tm, 1), 0)
    grow = i * tm + row
    halo = prev_ref.shape[0]
    x_prev = jnp.where(row == 0, prev_ref[halo - 1:halo, :], pltpu.roll(x, 1, axis=0))
    x_prev = jnp.where((grow == 0) | (grow == n_ctx), 0.0, x_prev)
    x_next = jnp.where(row == tm - 1, next_ref[0:1, :], pltpu.roll(x, tm - 1, axis=0))
    x_next = jnp.where((grow == n_ctx - 1) | (grow == t_total - 1), 0.0, x_next)
    y = x_prev * w_ref[0:1, :] + x * w_ref[1:2, :] + x_next * w_ref[2:3, :]
    y = y * (1.0 / (1.0 + jnp.exp(-y)))
    o_ref[:, :q_cols] = (y[:, :q_cols] * q_scale).astype(o_ref.dtype)
    o_ref[:, q_cols:] = y[:, q_cols:].astype(o_ref.dtype)


def _conv_silu(x, conv_w, n_ctx, q_cols, q_scale):
    t, c = x.shape
    tm = ROW_TILE
    halo = 8
    r = tm // halo
    last = t // halo - 1
    w8 = jnp.zeros((8, c), F32).at[:3].set(conv_w)
    return pl.pallas_call(
        functools.partial(_conv_silu_kernel, n_ctx=n_ctx, t_total=t, q_cols=q_cols, q_scale=q_scale),
        out_shape=jax.ShapeDtypeStruct((t, c), BF16),
        grid=(t // tm,),
        in_specs=[pl.BlockSpec((tm, c), lambda i: (i, 0)),
                  pl.BlockSpec((halo, c), lambda i: (jnp.maximum(i * r - 1, 0), 0)),
                  pl.BlockSpec((halo, c), lambda i: (jnp.minimum((i + 1) * r, last), 0)),
                  pl.BlockSpec((8, c), lambda i: (0, 0))],
        out_specs=pl.BlockSpec((tm, c), lambda i: (i, 0)),
        compiler_params=_cparams(("arbitrary",)),
        name="mlstm_conv_silu",
    )(x, x, x, w8)


def _mlstm_kernel(qk_ref, v_ref, gt_ref, bias_ref, o_ref, c_ref, n_ref, m_ref, *, reverse, heads, dk, dv, d_idx):
    c = pl.program_id(0)

    @pl.when(c == 0)
    def _():
        c_ref[...] = jnp.zeros_like(c_ref)
        n_ref[...] = jnp.zeros_like(n_ref)
        m_ref[...] = jnp.zeros_like(m_ref)

    n = qk_ref.shape[0]
    gates = gt_ref[...] + bias_ref[...]
    i_all = GATE_SOFTCAP * jnp.tanh(gates * (1.0 / GATE_SOFTCAP))
    f_all = -(jnp.maximum(-gates, 0.0) + jnp.log1p(jnp.exp(-jnp.abs(gates))))
    ri = lax.broadcasted_iota(I32, (n, n), 0)
    ci = lax.broadcasted_iota(I32, (n, n), 1)
    mask = (ci >= ri) if reverse else (ci <= ri)
    tri = jnp.where(mask, 1.0, 0.0).astype(F32)
    b_all = jnp.dot(tri, f_all, preferred_element_type=F32, precision=lax.Precision.HIGHEST)
    b_all_t = b_all.T
    i_all_t = i_all.T
    last = 0 if reverse else n - 1
    qk_w = heads * dk
    for h in range(heads):
        ic = d_idx * 2 * heads + h
        fc = ic + heads
        b_col = b_all[:, fc:fc + 1]
        b_row = b_all_t[fc:fc + 1, :]
        i_col = i_all[:, ic:ic + 1]
        i_row = i_all_t[ic:ic + 1, :]
        m_prev = m_ref[h][:, 0:1]
        q = qk_ref[:, h * dk:(h + 1) * dk]
        k = qk_ref[:, qk_w + h * dk:qk_w + (h + 1) * dk]
        v = v_ref[:, h * dv:(h + 1) * dv]
        c_prev = c_ref[h]
        n_prev = n_ref[h]
        d_log = jnp.where(mask, b_col - b_row + i_row, NEG_INF)
        m_t = jnp.maximum(b_col + m_prev, jnp.max(d_log, axis=-1, keepdims=True))
        w = jnp.exp(d_log - m_t)
        s = _dot_nt(q, k) * w
        inter = jnp.exp(b_col + m_prev - m_t)
        num = jnp.dot(s.astype(BF16), v, preferred_element_type=F32) + inter * jnp.dot(
            q, c_prev.astype(BF16), preferred_element_type=F32)
        den = jnp.sum(s, axis=-1, keepdims=True) + inter * jnp.sum(q.astype(F32) * n_prev, axis=-1, keepdims=True)
        o_ref[:, h * dv:(h + 1) * dv] = (num / jnp.maximum(jnp.abs(den), jnp.exp(-m_t))).astype(o_ref.dtype)
        b_last = b_col[last:last + 1, :]
        g = b_last - b_col + i_col
        m_new = jnp.maximum(b_last + m_prev, jnp.max(g, axis=0, keepdims=True))
        wk = jnp.exp(g - m_new) * k.astype(F32)
        dec = jnp.exp(b_last + m_prev - m_new)
        c_ref[h] = dec * c_prev + _dot_tn(wk.astype(BF16), v)
        n_ref[h] = dec * n_prev + jnp.sum(wk, axis=0, keepdims=True)
        m_ref[h] = jnp.broadcast_to(m_new, m_ref.shape[1:])


def _mlstm_scan(qk, vo, gates, bias, n_ctx, heads, dk, dv, reverse):
    t = qk.shape[0]
    nc, cc = t // CHUNK, n_ctx // CHUNK
    cm = lambda c: _chunk_index(c, nc, cc, reverse)
    return pl.pallas_call(
        functools.partial(_mlstm_kernel, reverse=reverse, heads=heads, dk=dk, dv=dv, d_idx=1 if reverse else 0),
        out_shape=jax.ShapeDtypeStruct((t, heads * dv), BF16),
        grid=(nc,),
        in_specs=[pl.BlockSpec((CHUNK, 2 * heads * dk), lambda c: (cm(c), 0)),
                  pl.BlockSpec((CHUNK, heads * dv), lambda c: (cm(c), 0)),
                  pl.BlockSpec((CHUNK, LANES), lambda c: (cm(c), 0)),
                  pl.BlockSpec((1, LANES), lambda c: (0, 0))],
        out_specs=pl.BlockSpec((CHUNK, heads * dv), lambda c: (cm(c), 0)),
        scratch_shapes=[pltpu.VMEM((heads, dk, dv), F32),
                        pltpu.VMEM((heads, 1, dk), F32),
                        pltpu.VMEM((heads, 1, LANES), F32)],
        compiler_params=_cparams(("arbitrary",)),
        name="mlstm_scan_rev" if reverse else "mlstm_scan_fwd",
    )(qk, vo, gates, bias)


def _mlstm_post_kernel(yf_ref, yb_ref, o_ref_in, w_ref, o_ref, *, heads, dv):
    for h in range(heads):
        sl = slice(h * dv, (h + 1) * dv)
        y = _rms(yf_ref[:, sl].astype(F32) + yb_ref[:, sl].astype(F32))
        o = o_ref_in[:, sl].astype(F32)
        o_ref[:, sl] = (y * w_ref[:, sl] * (1.0 / (1.0 + jnp.exp(-o)))).astype(o_ref.dtype)


def _mlstm_post(yf, yb, vo, norm_w, heads, dv):
    t, vw = yf.shape
    tm = ROW_TILE
    return pl.pallas_call(
        functools.partial(_mlstm_post_kernel, heads=heads, dv=dv),
        out_shape=jax.ShapeDtypeStruct((t, vw), BF16),
        grid=(t // tm,),
        in_specs=[pl.BlockSpec((tm, vw), lambda i: (i, 0)),
                  pl.BlockSpec((tm, vw), lambda i: (i, 0)),
                  pl.BlockSpec((tm, vw), lambda i: (i, 1)),
                  pl.BlockSpec((1, vw), lambda i: (0, 0))],
        out_specs=pl.BlockSpec((tm, vw), lambda i: (i, 0)),
        compiler_params=_cparams(("arbitrary",)),
        name="mlstm_post",
    )(yf, yb, vo, norm_w.reshape(1, vw))


def _qk_norm_rope_kernel(x_ref, cos_ref, sin_ref, qn_ref, kn_ref, q_ref, k_ref, *, q_heads, k_heads, hd):
    cos, sin = cos_ref[...], sin_ref[...]
    lane = lax.broadcasted_iota(I32, (1, hd), 1)
    first = (lane % (hd // 2)) < (hd // 4)
    for h in range(q_heads + k_heads):
        x = x_ref[:, h * hd:(h + 1) * hd]
        w = qn_ref[...] if h < q_heads else kn_ref[...]
        xh = _rms(x) * w
        partner = jnp.where(first, pltpu.roll(xh, hd - hd // 4, axis=1), pltpu.roll(xh, hd // 4, axis=1))
        y = (xh * cos + partner * sin).astype(q_ref.dtype)
        if h < q_heads:
            q_ref[:, h * hd:(h + 1) * hd] = y
        else:
            k_ref[:, (h - q_heads) * hd:(h - q_heads + 1) * hd] = y


def _qk_norm_rope(qk, cos, sin, q_norm, k_norm, q_heads, k_heads, hd):
    t = qk.shape[0]
    tm = ROW_TILE
    return pl.pallas_call(
        functools.partial(_qk_norm_rope_kernel, q_heads=q_heads, k_heads=k_heads, hd=hd),
        out_shape=(jax.ShapeDtypeStruct((t, q_heads * hd), BF16), jax.ShapeDtypeStruct((t, k_heads * hd), BF16)),
        grid=(t // tm,),
        in_specs=[pl.BlockSpec((tm, (q_heads + k_heads) * hd), lambda i: (i, 0)),
                  pl.BlockSpec((tm, hd), lambda i: (i, 0)),
                  pl.BlockSpec((tm, hd), lambda i: (i, 0)),
                  pl.BlockSpec((1, hd), lambda i: (0, 0)),
                  pl.BlockSpec((1, hd), lambda i: (0, 0))],
        out_specs=(pl.BlockSpec((tm, q_heads * hd), lambda i: (i, 0)),
                   pl.BlockSpec((tm, k_heads * hd), lambda i: (i, 0))),
        compiler_params=_cparams(("arbitrary",)),
        name="attn_qk_norm_rope",
    )(qk, cos, sin, q_norm.reshape(1, hd), k_norm.reshape(1, hd))


def _flash_kernel(q_ref, k_ref, v_ref, o_ref, m_ref, l_ref, acc_ref, alpha_ref, sa_ref, sb_ref, p_ref,
                  *, groups, hd, tk, n_ctx):
    tq = q_ref.shape[0]
    n_kv = k_ref.shape[0]
    n = n_kv // tk
    q = jnp.concatenate([q_ref[:, g * hd:(g + 1) * hd] for g in range(groups)], axis=0)
    m_ref[...] = jnp.full_like(m_ref, NEG_INF)
    l_ref[...] = jnp.zeros_like(l_ref)
    acc_ref[...] = jnp.zeros_like(acc_ref)

    def scores(off, size):
        return _dot_nt(q, k_ref[pl.ds(off, size), :])

    def absorb(s_ref, off, size):
        for band in range(groups * tq // FLASH_BAND):
            rows = slice(band * FLASH_BAND, (band + 1) * FLASH_BAND)
            s = s_ref[rows, :size]
            m_prev = m_ref[rows, :]
            m_new = jnp.maximum(m_prev, jnp.max(s, axis=-1, keepdims=True))
            alpha = jnp.exp2(m_prev - m_new)
            p = jnp.exp2(s - jnp.concatenate([m_new] * (size // LANES), axis=1))
            l_ref[rows, :] = alpha * l_ref[rows, :] + jnp.sum(p, axis=-1, keepdims=True)
            m_ref[rows, :] = m_new
            alpha_ref[rows, :] = alpha
            p_ref[rows, :size] = p.astype(BF16)
        acc_ref[...] = alpha_ref[...] * acc_ref[...] + jnp.dot(p_ref[:, :size], v_ref[pl.ds(off, size), :],
                                                                preferred_element_type=F32)

    is_ctx = pl.program_id(1) < n_ctx // tq

    @pl.when(is_ctx)
    def _():
        sa_ref[:, :n_ctx] = scores(0, n_ctx)
        absorb(sa_ref, 0, n_ctx)

    @pl.when(jnp.logical_not(is_ctx))
    def _():
        bufs = (sa_ref, sb_ref)
        bufs[0][...] = scores(0, tk)
        for j in range(n):
            if j + 1 < n:
                bufs[(j + 1) % 2][...] = scores((j + 1) * tk, tk)
            absorb(bufs[j % 2], j * tk, tk)

    out = acc_ref[...] / l_ref[...]
    for g in range(groups):
        o_ref[:, g * hd:(g + 1) * hd] = out[g * tq:(g + 1) * tq].astype(o_ref.dtype)


def _flash_attention(q, k, v, n_ctx, kv_heads, groups, hd):
    t = q.shape[0]
    tq = _pick_tile(math.gcd(t, n_ctx), (256, 128))
    tk = _pick_tile(t, (768, 512, 256, 128))
    assert hd == LANES and n_ctx <= tk and (groups * tq) % FLASH_BAND == 0
    gw = groups * hd
    return pl.pallas_call(
        functools.partial(_flash_kernel, groups=groups, hd=hd, tk=tk, n_ctx=n_ctx),
        out_shape=jax.ShapeDtypeStruct((t, kv_heads * gw), BF16),
        grid=(kv_heads, t // tq),
        in_specs=[pl.BlockSpec((tq, gw), lambda g, i: (i, g)),
                  pl.BlockSpec((t, hd), lambda g, i: (0, g)),
                  pl.BlockSpec((t, hd), lambda g, i: (0, g))],
        out_specs=pl.BlockSpec((tq, gw), lambda g, i: (i, g)),
        scratch_shapes=[pltpu.VMEM((groups * tq, LANES), F32),
                        pltpu.VMEM((groups * tq, LANES), F32),
                        pltpu.VMEM((groups * tq, hd), F32),
                        pltpu.VMEM((groups * tq, LANES), F32),
                        pltpu.VMEM((groups * tq, tk), F32),
                        pltpu.VMEM((groups * tq, tk), F32),
                        pltpu.VMEM((groups * tq, tk), BF16)],
        compiler_params=_cparams(("arbitrary", "arbitrary")),
        name="flash_attention",
    )(q, k, v)


def _router_kernel(x_ref, g_ref, sh_ref, sc_ref, wr_ref, br_ref, h_ref, ids_ref, wts_ref, cnt_ref, run_ref,
                   *, n_ctx, row0):
    i = pl.program_id(0)
    tm, d = x_ref.shape

    @pl.when(i == 0)
    def _():
        run_ref[...] = jnp.zeros_like(run_ref)

    ib = i + row0 // tm
    h = _rms(x_ref[...]) * g_ref[...]
    h = h * (1.0 + _mod_row(sc_ref, ib, tm, n_ctx)) + _mod_row(sh_ref, ib, tm, n_ctx)
    h_ref[...] = h
    h_hi = h.astype(BF16)
    h_lo = (h - h_hi.astype(F32)).astype(BF16)
    hw = jnp.dot(h_hi, wr_ref[...], preferred_element_type=F32)
    lw = jnp.dot(h_lo, wr_ref[:, :LANES], preferred_element_type=F32)
    logits = hw[:, :LANES] + (hw[:, LANES:] + lw) + br_ref[...]
    lane = lax.broadcasted_iota(I32, (tm, LANES), 1)
    big = jnp.int32(LANES)
    gl = jnp.where((lane >= MOE_EXPERTS) & (lane < MOE_EXPERTS + MOE_GROUPS), logits, NEG_INF)
    g_max = jnp.max(gl, axis=-1, keepdims=True)
    g_idx = jnp.min(jnp.where(gl == g_max, lane, big), axis=-1, keepdims=True) - MOE_EXPERTS
    g_w = 1.0 / jnp.sum(jnp.exp(gl - g_max), axis=-1, keepdims=True)
    el = jnp.where((lane < MOE_EXPERTS) & ((lane >> 3) == g_idx), logits, NEG_INF)
    m1 = jnp.max(el, axis=-1, keepdims=True)
    i1 = jnp.min(jnp.where(el == m1, lane, big), axis=-1, keepdims=True)
    el2 = jnp.where(lane == i1, NEG_INF, el)
    m2 = jnp.max(el2, axis=-1, keepdims=True)
    i2 = jnp.min(jnp.where(el2 == m2, lane, big), axis=-1, keepdims=True)
    e_sum = jnp.sum(jnp.exp(el - m1), axis=-1, keepdims=True)
    p1 = 1.0 / e_sum
    p2 = jnp.exp(m2 - m1) / e_sum
    w1 = g_w * (p1 / (p1 + p2))
    w2 = g_w * (p2 / (p1 + p2))
    onehot = jnp.where((lane == i1) | (lane == i2), 1.0, 0.0)
    ri = lax.broadcasted_iota(I32, (tm, tm), 0)
    ci = lax.broadcasted_iota(I32, (tm, tm), 1)
    strict = jnp.where(ci < ri, 1.0, 0.0).astype(BF16)
    before = jnp.dot(strict, onehot.astype(BF16), preferred_element_type=F32) + run_ref[0:1, :]
    r1 = jnp.sum(jnp.where(lane == i1, before, 0.0), axis=-1, keepdims=True).astype(I32)
    r2 = jnp.sum(jnp.where(lane == i2, before, 0.0), axis=-1, keepdims=True).astype(I32)
    ids_ref[...] = jnp.where(lane == 0, i1, jnp.where(lane == 1, i2, jnp.where(lane == 2, r1, jnp.where(lane == 3, r2, 0))))
    wts_ref[...] = jnp.where(lane == 0, w1, jnp.where(lane == 1, w2, 0.0))
    total = run_ref[0:1, :] + jnp.sum(onehot, axis=0, keepdims=True)
    run_ref[...] = jnp.broadcast_to(total, run_ref.shape)
    cnt_ref[...] = jnp.broadcast_to(total, cnt_ref.shape)


def _router(x, g, mod, k_shift, k_scale, w_router, b_router, n_ctx, row0):
    t, d = x.shape
    tm = ROW_TILE
    n = t - row0
    off = row0 // tm
    return pl.pallas_call(
        functools.partial(_router_kernel, n_ctx=n_ctx, row0=row0),
        out_shape=(jax.ShapeDtypeStruct((n, d), F32),
                   jax.ShapeDtypeStruct((n, LANES), I32),
                   jax.ShapeDtypeStruct((n, LANES), F32),
                   jax.ShapeDtypeStruct((8, LANES), F32)),
        grid=(n // tm,),
        in_specs=[pl.BlockSpec((tm, d), lambda i: (i + off, 0)),
                  pl.BlockSpec((1, d), lambda i: (0, 0)),
                  pl.BlockSpec((8, d), lambda i: (0, k_shift)),
                  pl.BlockSpec((8, d), lambda i: (0, k_scale)),
                  pl.BlockSpec((d, 2 * LANES), lambda i: (0, 0)),
                  pl.BlockSpec((1, LANES), lambda i: (0, 0))],
        out_specs=(pl.BlockSpec((tm, d), lambda i: (i, 0)),
                   pl.BlockSpec((tm, LANES), lambda i: (i, 0)),
                   pl.BlockSpec((tm, LANES), lambda i: (i, 0)),
                   pl.BlockSpec((8, LANES), lambda i: (0, 0))),
        scratch_shapes=[pltpu.VMEM((8, LANES), F32)],
        compiler_params=_cparams(("arbitrary",)),
        name="moe_router",
    )(x, g.reshape(1, d), mod, mod, w_router, b_router)


def _dispatch_kernel(dest_ref, h_ref, xs_in_ref, xs_ref, sem):
    del xs_in_ref
    i = pl.program_id(0)
    tm = h_ref.shape[0]

    def row_copy(r, k):
        slot = dest_ref[(i * tm + r) * MOE_TOPK + k]
        return pltpu.make_async_copy(h_ref.at[pl.ds(r, 1)], xs_ref.at[pl.ds(slot, 1)], sem)

    def start(r, carry):
        for k in range(MOE_TOPK):
            row_copy(r, k).start()
        return carry

    def wait(r, carry):
        for k in range(MOE_TOPK):
            row_copy(r, k).wait()
        return carry

    lax.fori_loop(0, tm, start, 0, unroll=8)
    lax.fori_loop(0, tm, wait, 0, unroll=8)


def _dispatch(dest, h, xs_init):
    n, w = h.shape
    tm = ROW_TILE
    return pl.pallas_call(
        _dispatch_kernel,
        out_shape=jax.ShapeDtypeStruct(xs_init.shape, xs_init.dtype),
        grid_spec=pltpu.PrefetchScalarGridSpec(
            num_scalar_prefetch=1,
            grid=(n // tm,),
            in_specs=[pl.BlockSpec((tm, w), lambda i, dest: (i, 0)),
                      pl.BlockSpec(memory_space=pl.ANY)],
            out_specs=pl.BlockSpec(memory_space=pl.ANY),
            scratch_shapes=[pltpu.SemaphoreType.DMA(())]),
        input_output_aliases={2: 0},
        compiler_params=_cparams(("arbitrary",)),
        name="moe_dispatch",
    )(dest, h, xs_init)


def _expert_kernel(be_ref, first_ref, next_ref, nb_ref, xs_ref, wg_hbm, wu_hbm, wd_hbm, ys_ref,
                   wg_st, wu_st, wd_st, wg_bf, wu_bf, wd_bf, sem, *, layer):
    b = pl.program_id(0)
    used = b < nb_ref[0]

    weights = ((wg_hbm, wg_st, wg_bf), (wu_hbm, wu_st, wu_bf), (wd_hbm, wd_st, wd_bf))

    def fetch(e, idx):
        hbm, st, _ = weights[idx]
        return pltpu.make_async_copy(hbm.at[layer, e], st, sem.at[idx])

    @pl.when(b == 0)
    def _():
        for idx in range(len(weights)):
            fetch(be_ref[0], idx).start()

    @pl.when(used & (first_ref[b] == 1))
    def _():
        has_next = next_ref[b] >= 0
        for idx, (_, st, bf) in enumerate(weights):
            fetch(be_ref[b], idx).wait()
            bf[...] = st[...].astype(BF16)

            @pl.when(has_next)
            def _():
                fetch(next_ref[b], idx).start()

    @pl.when(used)
    def _():
        x = xs_ref[...].astype(BF16)
        gate = jnp.dot(x, wg_bf[...], preferred_element_type=F32)
        up = jnp.dot(x, wu_bf[...], preferred_element_type=F32)
        act = (gate * (1.0 / (1.0 + jnp.exp(-gate))) * up).astype(BF16)
        ys_ref[...] = jnp.dot(act, wd_bf[...], preferred_element_type=F32)

    @pl.when(jnp.logical_not(used))
    def _():
        ys_ref[...] = jnp.zeros_like(ys_ref)


def _experts(block_expert, first, next_expert, n_used, xs, w_gate, w_up, w_down, layer):
    n_slots, d = xs.shape
    ff = w_gate.shape[3]
    nb = n_slots // MOE_BLOCK
    row_spec = pl.BlockSpec((MOE_BLOCK, d), lambda b, *_: (b, 0))
    hbm = pl.BlockSpec(memory_space=pl.ANY)
    return pl.pallas_call(
        functools.partial(_expert_kernel, layer=layer),
        out_shape=jax.ShapeDtypeStruct((n_slots, d), F32),
        grid_spec=pltpu.PrefetchScalarGridSpec(
            num_scalar_prefetch=4,
            grid=(nb,),
            in_specs=[row_spec, hbm, hbm, hbm],
            out_specs=row_spec,
            scratch_shapes=[pltpu.VMEM((d, ff), F32), pltpu.VMEM((d, ff), F32), pltpu.VMEM((ff, d), F32),
                            pltpu.VMEM((d, ff), BF16), pltpu.VMEM((d, ff), BF16), pltpu.VMEM((ff, d), BF16),
                            pltpu.SemaphoreType.DMA((3,))]),
        compiler_params=_cparams(("arbitrary",)),
        name="moe_experts",
    )(block_expert, first, next_expert, n_used, xs, w_gate, w_up, w_down)


def _combine_kernel(dest_ref, x_ref, wts_ref, mod_ref, ys_ref, g_ref, sh_ref, sc_ref, *rest, n_ctx, row0, last):
    if last:
        o_ref, buf, sem = rest
    else:
        o_ref, h_ref, buf, sem = rest
    i = pl.program_id(0)
    tm = x_ref.shape[0]

    def row_copy(r, k):
        slot = dest_ref[(i * tm + r) * MOE_TOPK + k]
        return pltpu.make_async_copy(ys_ref.at[pl.ds(slot, 1)], buf.at[k, pl.ds(r, 1)], sem)

    def start(r, carry):
        for k in range(MOE_TOPK):
            row_copy(r, k).start()
        return carry

    def wait(r, carry):
        for k in range(MOE_TOPK):
            row_copy(r, k).wait()
        return carry

    lax.fori_loop(0, tm, start, 0, unroll=8)
    lax.fori_loop(0, tm, wait, 0, unroll=8)
    w = wts_ref[...]
    y = w[:, 0:1] * buf[0] + w[:, 1:2] * buf[1]
    ib = i + row0 // tm
    x_new = x_ref[...] + _mod_row(mod_ref, ib, tm, n_ctx) * y
    hn = _rms(x_new) * g_ref[...]
    if last:
        o_ref[...] = hn
    else:
        o_ref[...] = x_new
        hn = hn * (1.0 + _mod_row(sc_ref, ib, tm, n_ctx)) + _mod_row(sh_ref, ib, tm, n_ctx)
        h_ref[...] = hn.astype(h_ref.dtype)


def _combine(dest, x, wts, mod, k_gate, ys, n_ctx, row0, g_next, mod_next, last):
    t, d = x.shape
    tm = ROW_TILE
    n = t - row0
    off = row0 // tm
    row_spec = pl.BlockSpec((tm, d), lambda i, dest: (i, 0))
    out_shape = jax.ShapeDtypeStruct((n, d), F32)
    out_specs = row_spec
    if not last:
        out_shape = (out_shape, jax.ShapeDtypeStruct((n, d), BF16))
        out_specs = (row_spec, row_spec)
    return pl.pallas_call(
        functools.partial(_combine_kernel, n_ctx=n_ctx, row0=row0, last=last),
        out_shape=out_shape,
        grid_spec=pltpu.PrefetchScalarGridSpec(
            num_scalar_prefetch=1,
            grid=(n // tm,),
            in_specs=[pl.BlockSpec((tm, d), lambda i, dest: (i + off, 0)),
                      pl.BlockSpec((tm, LANES), lambda i, dest: (i, 0)),
                      pl.BlockSpec((8, d), lambda i, dest: (0, k_gate)),
                      pl.BlockSpec(memory_space=pl.ANY),
                      pl.BlockSpec((1, d), lambda i, dest: (0, 0)),
                      pl.BlockSpec((8, d), lambda i, dest: (0, 0)),
                      pl.BlockSpec((8, d), lambda i, dest: (0, 1))],
            out_specs=out_specs,
            scratch_shapes=[pltpu.VMEM((MOE_TOPK, tm, d), F32), pltpu.SemaphoreType.DMA(())]),
        compiler_params=_cparams(("arbitrary",)),
        name="moe_combine",
    )(dest, x, wts, mod, ys, g_next.reshape(1, d), mod_next, mod_next)


def _hier_moe(x, g, mod, w_rg, b_rg, w_re, b_re, w_gate, w_up, w_down, layer, n_ctx, row0, xs_buf,
              g_next, mod_next, last):
    t, d = x.shape
    n = t - row0
    pad = LANES - MOE_EXPERTS - MOE_GROUPS
    w_router = jnp.concatenate([w_re, w_rg, jnp.zeros((d, pad), F32)], axis=1)
    w_hi = w_router.astype(BF16)
    w_router = jnp.concatenate([w_hi, (w_router - w_hi.astype(F32)).astype(BF16)], axis=1)
    b_router = jnp.concatenate([b_re, b_rg, jnp.zeros((pad,), F32)]).reshape(1, LANES)
    h, ids, wts, counts = _router(x, g, mod, 3, 4, w_router, b_router, n_ctx, row0)
    counts = counts[0, :MOE_EXPERTS].astype(I32)
    padded = (counts + MOE_BLOCK - 1) // MOE_BLOCK * MOE_BLOCK
    seg_end = jnp.cumsum(padded)
    seg_start = seg_end - padded
    dest = (seg_start[ids[:, :MOE_TOPK]] + ids[:, MOE_TOPK:2 * MOE_TOPK]).reshape(-1)
    n_blocks = (t * MOE_TOPK + MOE_EXPERTS * (MOE_BLOCK - 1)) // MOE_BLOCK + 1
    block_start = jnp.arange(n_blocks, dtype=I32) * MOE_BLOCK
    block_expert = jnp.minimum(jnp.sum((seg_end[None, :] <= block_start[:, None]).astype(I32), axis=1),
                               MOE_EXPERTS - 1).astype(I32)
    n_used = (seg_end[-1:] // MOE_BLOCK).astype(I32)
    block_id = jnp.arange(n_blocks, dtype=I32)
    prev_expert = jnp.concatenate([jnp.full((1,), -1, I32), block_expert[:-1]])
    first = ((block_id < n_used[0]) & (block_expert != prev_expert)).astype(I32)
    eid = jnp.arange(MOE_EXPERTS, dtype=I32)
    owner = jnp.where(padded > 0, eid, MOE_EXPERTS)
    later = jnp.concatenate([lax.cummin(owner[::-1])[::-1][1:], jnp.full((1,), MOE_EXPERTS, I32)])
    next_expert = jnp.where(later < MOE_EXPERTS, later, -1)[block_expert].astype(I32)
    if xs_buf is None:
        xs_buf = jnp.zeros((n_blocks * MOE_BLOCK, d), F32)
    xs = _dispatch(dest, h, xs_buf)
    ys = _experts(block_expert, first, next_expert, n_used, xs, w_gate, w_up, w_down, layer)
    return _combine(dest, x, wts, mod, 5, ys, n_ctx, row0, g_next, mod_next, last), xs


def _rope_table(pos, dim, base):
    inv = jnp.power(base, -jnp.arange(0, dim, 2, dtype=F32) / dim)
    ang = pos.astype(F32)[:, None] * inv[None, :]
    return jnp.cos(ang), jnp.sin(ang)


def _flipped_positions(t, n_ctx):
    idx = jnp.arange(t, dtype=I32)
    return jnp.where(idx < n_ctx, n_ctx - 1 - idx, n_ctx + (t - 1 - idx))


def _axial_tables(n_ctx, n_lat, hd):
    rows = n_lat // GRID_W
    row = jnp.repeat(jnp.arange(rows, dtype=I32), GRID_W)
    col = jnp.tile(jnp.arange(GRID_W, dtype=I32), rows)
    half = hd // 2
    cr, sr = _rope_table(row, half, ROPE_BASE)
    cc, sc = _rope_table(col, half, ROPE_BASE)
    cos = jnp.concatenate([cr, cr, cc, cc], axis=1)
    sin = jnp.concatenate([-sr, sr, -sc, sc], axis=1)
    cos = jnp.concatenate([jnp.ones((n_ctx, hd), F32), cos], axis=0)
    sin = jnp.concatenate([jnp.zeros((n_ctx, hd), F32), sin], axis=0)
    return cos, sin


def _retention_layer(xa, h, mod, w_in, logit_gamma, gn_w, gn_b, w_out, j, n_ctx):
    t, d = xa.shape
    heads = RET_HEADS
    dk = d // heads
    dv = 2 * dk
    qk_w, v_w = heads * dk, heads * dv
    pos_f = jnp.arange(t, dtype=I32)
    cf, sf = _rope_table(pos_f, dk, RET_ROPE_BASE)
    cb, sb = _rope_table(_flipped_positions(t, n_ctx), dk, RET_ROPE_BASE)
    tables = (cf, sf, cb, sb)
    q = _matmul_rope(h, w_in, j, 0, qk_w, tables, dk ** -0.5, dk)
    k = _matmul_rope(h, w_in, j, qk_w, qk_w, tables, 1.0, dk)
    vg = _matmul(h, w_in, j, 2 * qk_w, 2 * v_w, BF16)
    log_gamma = jax.nn.log_sigmoid(logit_gamma.astype(F32))
    yf = _retention_scan(log_gamma[0], q, k, vg, 0, n_ctx, heads, dk, dv, reverse=False)
    yb = _retention_scan(log_gamma[1], q, k, vg, 1, n_ctx, heads, dk, dv, reverse=True)
    yn = _ret_post(yf, yb, vg, gn_w, gn_b, heads, dv)
    return _matmul_residual(yn, w_out, j, xa, mod, 2, n_ctx)


def _mlstm_layer(xa, h, mod, w_in, conv_w, b_gate, norm_w, w_out, j, n_ctx):
    t, d = xa.shape
    heads = MLSTM_HEADS
    dk = d // (2 * heads)
    dv = d // heads
    qk_w, v_w = heads * dk, heads * dv
    n_gates = 4 * heads
    qk_pre = _matmul(h, w_in, j, 0, 2 * qk_w, F32)
    vo = _matmul(h, w_in, j, 2 * qk_w, 2 * v_w, BF16)
    w_gates = jnp.pad(w_in[j, :, 2 * qk_w + 2 * v_w:], ((0, 0), (0, LANES - n_gates)))
    gates = _matmul(h, w_gates[None], 0, 0, LANES, F32)
    bias = jnp.pad(b_gate.astype(F32).reshape(1, n_gates), ((0, 0), (0, LANES - n_gates)))
    qk = _conv_silu(qk_pre, conv_w, n_ctx, qk_w, dk ** -0.5)
    yf = _mlstm_scan(qk, vo, gates, bias, n_ctx, heads, dk, dv, reverse=False)
    yb = _mlstm_scan(qk, vo, gates, bias, n_ctx, heads, dk, dv, reverse=True)
    yn = _mlstm_post(yf, yb, vo, norm_w, heads, dv)
    return _matmul_residual(yn, w_out, j, xa, mod, 2, n_ctx)


def _attention_layer(xa, h, mod, w_in, q_norm, k_norm, w_out, j, n_ctx):
    t, d = xa.shape
    hd = d // ATTN_HEADS
    groups = ATTN_HEADS // ATTN_KV_HEADS
    q_w, kv_w = ATTN_HEADS * hd, ATTN_KV_HEADS * hd
    n_lat = t - n_ctx
    qk_pre = _matmul(h, w_in, j, 0, q_w + kv_w, F32)
    v = _matmul(h, w_in, j, q_w + kv_w, kv_w, BF16)
    cos, sin = _axial_tables(n_ctx, n_lat, hd)
    q_gain = q_norm * (hd ** -0.5 * math.log2(math.e))
    q, k = _qk_norm_rope(qk_pre, cos, sin, q_gain, k_norm, ATTN_HEADS, ATTN_KV_HEADS, hd)
    o = _flash_attention(q, k, v, n_ctx, ATTN_KV_HEADS, groups, hd)
    return _matmul_residual(o, w_out, j, xa, mod, 2, n_ctx)


def kernel(x, c, ctx, c_ctx, w_ada, b_ada, norm_g, ret_w_in, ret_logit_gamma, ret_gn_w, ret_gn_b, ret_w_out, mlstm_w_in, mlstm_conv_w, mlstm_b_gate, mlstm_norm_w, mlstm_w_out, attn_w_in, attn_q_norm, attn_k_norm, attn_w_out, moe_w_router_group, moe_b_router_group, moe_w_router_expert, moe_b_router_expert, moe_w_gate, moe_w_up, moe_w_down, final_norm_g):
    bsz, n_lat, d = x.shape
    n_ctx = ctx.shape[1]
    depth = w_ada.shape[0]
    assert bsz == 1 and n_ctx % ROW_TILE == 0 and n_lat % ROW_TILE == 0 and n_lat % GRID_W == 0
    xa = jnp.concatenate([ctx[0], x[0]], axis=0)
    s = jnp.stack([jax.nn.silu(c[0]), jax.nn.silu(c_ctx)])
    s8 = jnp.zeros((8, d), F32).at[:2].set(s).astype(BF16)
    mods = _ada_modulation(s8, w_ada, b_ada)
    xs_buf = None
    h = _norm_mod(xa, norm_g[0, 0], mods[0], 0, 1, n_ctx)
    for i in range(depth):
        kind, j = i % 3, i // 3
        mod = mods[i]
        last = i == depth - 1
        if kind == 0:
            xa = _retention_layer(xa, h, mod, ret_w_in, ret_logit_gamma[j], ret_gn_w[j], ret_gn_b[j],
                                  ret_w_out, j, n_ctx)
        elif kind == 1:
            xa = _mlstm_layer(xa, h, mod, mlstm_w_in, mlstm_conv_w[j], mlstm_b_gate[j],
                              mlstm_norm_w[j], mlstm_w_out, j, n_ctx)
        else:
            xa = _attention_layer(xa, h, mod, attn_w_in, attn_q_norm[j], attn_k_norm[j],
                                  attn_w_out, j, n_ctx)
        row0 = n_ctx if last else 0
        g_next = final_norm_g if last else norm_g[i + 1, 0]
        mod_next = mod if last else mods[i + 1]
        out, xs_buf = _hier_moe(xa, norm_g[i, 1], mod, moe_w_router_group[i], moe_b_router_group[i],
                                moe_w_router_expert[i], moe_b_router_expert[i], moe_w_gate, moe_w_up, moe_w_down, i,
                                n_ctx, row0, xs_buf, g_next, mod_next, last)
        if last:
            return out[None]
        xa, h = out
```

```python
import functools
import math

import jax
import jax.numpy as jnp
import numpy as np
from jax import lax
from jax.experimental import pallas as pl
from jax.experimental.pallas import tpu as pltpu

F32 = jnp.float32
BF16 = jnp.bfloat16
I32 = jnp.int32
U32 = jnp.uint32

LANES = 128
CHUNK = 128
NORM_EPS = 1e-6
GRID_W = 64
RET_HEADS = 8
RET_ROPE_BASE = 10000.0
MLSTM_HEADS = 8
GATE_SOFTCAP = 15.0
ATTN_HEADS = 16
ATTN_KV_HEADS = 8
ROPE_BASE = 10000.0
MOE_GROUPS = 4
MOE_PER_GROUP = 8
MOE_EXPERTS = MOE_GROUPS * MOE_PER_GROUP
MOE_TOPK = 2
MOE_BLOCK = 256
ROW_TILE = 256
FLASH_BAND = 32
VMEM_LIMIT = 56 * 1024 * 1024
NEG_INF = float("-inf")


def _cparams(sem, vmem=VMEM_LIMIT):
    return pltpu.CompilerParams(dimension_semantics=sem, vmem_limit_bytes=vmem)


def _pick_tile(total, candidates):
    for c in candidates:
        if total % c == 0:
            return c
    raise ValueError(f"no tile for {total} in {candidates}")


def _mod_row(mod_ref, i, tile, n_ctx):
    sel = jnp.where(i < n_ctx // tile, 1, 0)
    return mod_ref[pl.ds(sel, 1), :]


def _ada_kernel(s_ref, w_ref, b_ref, o_ref):
    w = w_ref[0].astype(BF16)
    o_ref[0] = jnp.dot(s_ref[...], w, preferred_element_type=F32) + b_ref[0]


def _ada_modulation(s8, w_ada, b_ada):
    depth, d, n = w_ada.shape
    tn = _pick_tile(n, (1024, 512, 256, 128))
    return pl.pallas_call(
        _ada_kernel,
        out_shape=jax.ShapeDtypeStruct((depth, 8, n), F32),
        grid=(depth, n // tn),
        in_specs=[pl.BlockSpec((8, d), lambda l, j: (0, 0)),
                  pl.BlockSpec((1, d, tn), lambda l, j: (l, 0, j)),
                  pl.BlockSpec((1, 1, tn), lambda l, j: (l, 0, j))],
        out_specs=pl.BlockSpec((1, 8, tn), lambda l, j: (l, 0, j)),
        compiler_params=_cparams(("arbitrary", "arbitrary")),
        name="ada_modulation",
    )(s8, w_ada, b_ada.reshape(depth, 1, n))


def _rms(x):
    return x * lax.rsqrt(jnp.mean(x * x, axis=-1, keepdims=True) + NORM_EPS)


def _norm_mod_kernel(x_ref, g_ref, sh_ref, sc_ref, o_ref, *, n_ctx):
    i = pl.program_id(0)
    tm = x_ref.shape[0]
    y = _rms(x_ref[...]) * g_ref[...]
    y = y * (1.0 + _mod_row(sc_ref, i, tm, n_ctx)) + _mod_row(sh_ref, i, tm, n_ctx)
    o_ref[...] = y.astype(o_ref.dtype)


def _norm_mod(x, g, mod, k_shift, k_scale, n_ctx):
    t, d = x.shape
    tm = ROW_TILE
    return pl.pallas_call(
        functools.partial(_norm_mod_kernel, n_ctx=n_ctx),
        out_shape=jax.ShapeDtypeStruct((t, d), BF16),
        grid=(t // tm,),
        in_specs=[pl.BlockSpec((tm, d), lambda i: (i, 0)),
                  pl.BlockSpec((1, d), lambda i: (0, 0)),
                  pl.BlockSpec((8, d), lambda i: (0, k_shift)),
                  pl.BlockSpec((8, d), lambda i: (0, k_scale))],
        out_specs=pl.BlockSpec((tm, d), lambda i: (i, 0)),
        compiler_params=_cparams(("arbitrary",)),
        name="norm_mod",
    )(x, g.reshape(1, d), mod, mod)


def _cast_weight_once(w_ref, wbf_ref):
    @pl.when(pl.program_id(1) == 0)
    def _():
        wbf_ref[...] = w_ref[...].astype(BF16)


def _mm_plain_kernel(a_ref, w_ref, o_ref, wbf_ref, *, scale):
    _cast_weight_once(w_ref, wbf_ref)
    acc = jnp.dot(a_ref[...], wbf_ref[...], preferred_element_type=F32)
    if scale != 1.0:
        acc = acc * scale
    o_ref[...] = acc.astype(o_ref.dtype)


def _mm_residual_kernel(a_ref, w_ref, x_ref, mod_ref, o_ref, wbf_ref, *, n_ctx):
    _cast_weight_once(w_ref, wbf_ref)
    tm = a_ref.shape[0]
    acc = jnp.dot(a_ref[...], wbf_ref[...], preferred_element_type=F32)
    row = pl.program_id(1) * tm + lax.broadcasted_iota(I32, (tm, 1), 0)
    gate = jnp.where(row < n_ctx, mod_ref[1:2, :], mod_ref[0:1, :])
    o_ref[...] = x_ref[...] + gate * acc


def _mm_rope_kernel(a_ref, w_ref, cf_ref, sf_ref, cb_ref, sb_ref, o_ref, wbf_ref, *, scale, head_dim):
    _cast_weight_once(w_ref, wbf_ref)
    acc = jnp.dot(a_ref[...], wbf_ref[...], preferred_element_type=F32)
    if scale != 1.0:
        acc = acc * scale
    half = head_dim // 2
    for d, (c_ref, s_ref) in enumerate(((cf_ref, sf_ref), (cb_ref, sb_ref))):
        c, s = c_ref[...], s_ref[...]
        for h in range(acc.shape[1] // head_dim):
            x1 = acc[:, h * head_dim:h * head_dim + half]
            x2 = acc[:, h * head_dim + half:(h + 1) * head_dim]
            o_ref[d, :, h * head_dim:h * head_dim + half] = (x1 * c - x2 * s).astype(o_ref.dtype)
            o_ref[d, :, h * head_dim + half:(h + 1) * head_dim] = (x1 * s + x2 * c).astype(o_ref.dtype)


def _mm_tiles(m, k, n):
    tm = _pick_tile(m, (768, 512, 384, 256, 128))
    tn_cap = 1024 if k <= 2048 else 512
    tn = _pick_tile(n, tuple(c for c in (1024, 512, 256, 128) if c <= tn_cap))
    return tm, tn


def _matmul(a, w, layer, col0, n, out_dtype, scale=1.0):
    m, k = a.shape
    tm, tn = _mm_tiles(m, k, n)
    assert col0 % tn == 0
    c0 = col0 // tn
    return pl.pallas_call(
        functools.partial(_mm_plain_kernel, scale=scale),
        out_shape=jax.ShapeDtypeStruct((m, n), out_dtype),
        grid=(n // tn, m // tm),
        in_specs=[pl.BlockSpec((tm, k), lambda j, i: (i, 0)),
                  pl.BlockSpec((None, k, tn), lambda j, i: (layer, 0, j + c0))],
        out_specs=pl.BlockSpec((tm, tn), lambda j, i: (i, j)),
        scratch_shapes=[pltpu.VMEM((k, tn), BF16)],
        compiler_params=_cparams(("arbitrary", "arbitrary")),
        name="matmul",
    )(a, w)


def _matmul_residual(a, w, layer, x, mod, k_gate, n_ctx):
    m, k = a.shape
    n = w.shape[2]
    tm, tn = _mm_tiles(m, k, n)
    kb = k_gate * (n // tn)
    return pl.pallas_call(
        functools.partial(_mm_residual_kernel, n_ctx=n_ctx),
        out_shape=jax.ShapeDtypeStruct((m, n), F32),
        grid=(n // tn, m // tm),
        in_specs=[pl.BlockSpec((tm, k), lambda j, i: (i, 0)),
                  pl.BlockSpec((None, k, tn), lambda j, i: (layer, 0, j)),
                  pl.BlockSpec((tm, tn), lambda j, i: (i, j)),
                  pl.BlockSpec((8, tn), lambda j, i: (0, kb + j))],
        out_specs=pl.BlockSpec((tm, tn), lambda j, i: (i, j)),
        scratch_shapes=[pltpu.VMEM((k, tn), BF16)],
        compiler_params=_cparams(("arbitrary", "arbitrary")),
        name="matmul_residual",
    )(a, w, x, mod)


def _matmul_rope(a, w, layer, col0, n, tables, scale, head_dim):
    m, k = a.shape
    tm, tn = _mm_tiles(m, k, n)
    assert col0 % tn == 0 and tn % head_dim == 0
    c0 = col0 // tn
    half = head_dim // 2
    tab_spec = pl.BlockSpec((tm, half), lambda j, i: (i, 0))
    return pl.pallas_call(
        functools.partial(_mm_rope_kernel, scale=scale, head_dim=head_dim),
        out_shape=jax.ShapeDtypeStruct((2, m, n), BF16),
        grid=(n // tn, m // tm),
        in_specs=[pl.BlockSpec((tm, k), lambda j, i: (i, 0)),
                  pl.BlockSpec((None, k, tn), lambda j, i: (layer, 0, j + c0)),
                  tab_spec, tab_spec, tab_spec, tab_spec],
        out_specs=pl.BlockSpec((2, tm, tn), lambda j, i: (0, i, j)),
        scratch_shapes=[pltpu.VMEM((k, tn), BF16)],
        compiler_params=_cparams(("arbitrary", "arbitrary")),
        name="matmul_rope",
    )(a, w, *tables)


def _chunk_index(c, n_chunks, ctx_chunks, reverse):
    if not reverse:
        return c
    return jnp.where(c < ctx_chunks, ctx_chunks - 1 - c, n_chunks - 1 - (c - ctx_chunks))


def _dot_nt(a, b):
    return lax.dot_general(a, b, (((1,), (1,)), ((), ())), preferred_element_type=F32)


def _dot_tn(a, b):
    return lax.dot_general(a, b, (((0,), (0,)), ((), ())), preferred_element_type=F32)


def _retention_kernel(lg_ref, q_ref, k_ref, v_ref, o_ref, s_ref, *, reverse, heads, dk, dv):
    c = pl.program_id(0)

    @pl.when(c == 0)
    def _():
        s_ref[...] = jnp.zeros_like(s_ref)

    n = q_ref.shape[0]
    ri = lax.broadcasted_iota(I32, (n, n), 0)
    ci = lax.broadcasted_iota(I32, (n, n), 1)
    diff = (ci - ri) if reverse else (ri - ci)
    dmask = diff >= 0
    dist = jnp.where(dmask, diff, 0).astype(F32)
    r1 = lax.broadcasted_iota(I32, (n, 1), 0)
    pos = ((n - 1 - r1) if reverse else r1).astype(F32)
    for h in range(heads):
        lg = jnp.full((1, 1), lg_ref[h], F32)
        decay = jnp.where(dmask, jnp.exp(dist * lg), 0.0)
        q_dec = jnp.exp((pos + 1.0) * lg)
        k_dec = jnp.exp((n - 1.0 - pos) * lg)
        c_dec = jnp.exp(float(n) * lg)
        q = q_ref[:, h * dk:(h + 1) * dk]
        k = k_ref[:, h * dk:(h + 1) * dk]
        v = v_ref[:, h * dv:(h + 1) * dv]
        s_prev = s_ref[h]
        scores = (_dot_nt(q, k) * decay).astype(BF16)
        inner = jnp.dot(scores, v, preferred_element_type=F32)
        cross = jnp.dot(q, s_prev.astype(BF16), preferred_element_type=F32)
        o_ref[:, h * dv:(h + 1) * dv] = (inner + q_dec * cross).astype(o_ref.dtype)
        kd = (k.astype(F32) * k_dec).astype(BF16)
        s_ref[h] = c_dec * s_prev + _dot_tn(kd, v)


def _retention_scan(log_gamma, q, k, vg, d_idx, n_ctx, heads, dk, dv, reverse):
    t = q.shape[1]
    nc, cc = t // CHUNK, n_ctx // CHUNK
    cm = lambda c: _chunk_index(c, nc, cc, reverse)
    return pl.pallas_call(
        functools.partial(_retention_kernel, reverse=reverse, heads=heads, dk=dk, dv=dv),
        out_shape=jax.ShapeDtypeStruct((t, heads * dv), BF16),
        grid=(nc,),
        in_specs=[pl.BlockSpec(memory_space=pltpu.SMEM),
                  pl.BlockSpec((None, CHUNK, heads * dk), lambda c: (d_idx, cm(c), 0)),
                  pl.BlockSpec((None, CHUNK, heads * dk), lambda c: (d_idx, cm(c), 0)),
                  pl.BlockSpec((CHUNK, heads * dv), lambda c: (cm(c), 0))],
        out_specs=pl.BlockSpec((CHUNK, heads * dv), lambda c: (cm(c), 0)),
        scratch_shapes=[pltpu.VMEM((heads, dk, dv), F32)],
        compiler_params=_cparams(("arbitrary",)),
        name="retention_scan_rev" if reverse else "retention_scan_fwd",
    )(log_gamma, q, k, vg)


def _ret_post_kernel(yf_ref, yb_ref, g_ref, w_ref, b_ref, o_ref, *, heads, dv):
    for h in range(heads):
        sl = slice(h * dv, (h + 1) * dv)
        y = yf_ref[:, sl].astype(F32) + yb_ref[:, sl].astype(F32)
        mu = jnp.mean(y, axis=-1, keepdims=True)
        yc = y - mu
        var = jnp.mean(yc * yc, axis=-1, keepdims=True)
        yn = yc * lax.rsqrt(var + NORM_EPS)
        g = g_ref[:, sl].astype(F32)
        silu = g * (1.0 / (1.0 + jnp.exp(-g)))
        o_ref[:, sl] = ((yn * w_ref[:, sl] + b_ref[:, sl]) * silu).astype(o_ref.dtype)


def _ret_post(yf, yb, vg, gn_w, gn_b, heads, dv):
    t, vw = yf.shape
    tm = ROW_TILE
    return pl.pallas_call(
        functools.partial(_ret_post_kernel, heads=heads, dv=dv),
        out_shape=jax.ShapeDtypeStruct((t, vw), BF16),
        grid=(t // tm,),
        in_specs=[pl.BlockSpec((tm, vw), lambda i: (i, 0)),
                  pl.BlockSpec((tm, vw), lambda i: (i, 0)),
                  pl.BlockSpec((tm, vw), lambda i: (i, 1)),
                  pl.BlockSpec((1, vw), lambda i: (0, 0)),
                  pl.BlockSpec((1, vw), lambda i: (0, 0))],
        out_specs=pl.BlockSpec((tm, vw), lambda i: (i, 0)),
        compiler_params=_cparams(("arbitrary",)),
        name="retention_post",
    )(yf, yb, vg, gn_w.reshape(1, vw), gn_b.reshape(1, vw))


def _conv_silu_kernel(x_ref, prev_ref, next_ref, w_ref, o_ref, *, n_ctx, t_total, q_cols, q_scale):
    i = pl.program_id(0)
    tm = x_ref.shape[0]
    x = x_ref[...]
    row = lax.broadcasted_iota(I32, (tm, 1), 0)
    grow = i * tm + row
    halo = prev_ref.shape[0]
    x_prev = jnp.where(row == 0, prev_ref[halo - 1:halo, :], pltpu.roll(x, 1, axis=0))
    x_prev = jnp.where((grow == 0) | (grow == n_ctx), 0.0, x_prev)
    x_next = jnp.where(row == tm - 1, next_ref[0:1, :], pltpu.roll(x, tm - 1, axis=0))
    x_next = jnp.where((grow == n_ctx - 1) | (grow == t_total - 1), 0.0, x_next)
    y = x_prev * w_ref[0:1, :] + x * w_ref[1:2, :] + x_next * w_ref[2:3, :]
    y = y * (1.0 / (1.0 + jnp.exp(-y)))
    o_ref[:, :q_cols] = (y[:, :q_cols] * q_scale).astype(o_ref.dtype)
    o_ref[:, q_cols:] = y[:, q_cols:].astype(o_ref.dtype)


def _conv_silu(x, conv_w, n_ctx, q_cols, q_scale):
    t, c = x.shape
    tm = ROW_TILE
    halo = 8
    r = tm // halo
    last = t // halo - 1
    w8 = jnp.zeros((8, c), F32).at[:3].set(conv_w)
    return pl.pallas_call(
        functools.partial(_conv_silu_kernel, n_ctx=n_ctx, t_total=t, q_cols=q_cols, q_scale=q_scale),
        out_shape=jax.ShapeDtypeStruct((t, c), BF16),
        grid=(t // tm,),
        in_specs=[pl.BlockSpec((tm, c), lambda i: (i, 0)),
                  pl.BlockSpec((halo, c), lambda i: (jnp.maximum(i * r - 1, 0), 0)),
                  pl.BlockSpec((halo, c), lambda i: (jnp.minimum((i + 1) * r, last), 0)),
                  pl.BlockSpec((8, c), lambda i: (0, 0))],
        out_specs=pl.BlockSpec((tm, c), lambda i: (i, 0)),
        compiler_params=_cparams(("arbitrary",)),
        name="mlstm_conv_silu",
    )(x, x, x, w8)


def _mlstm_kernel(qk_ref, v_ref, gt_ref, bias_ref, o_ref, c_ref, n_ref, m_ref, *, reverse, heads, dk, dv, d_idx):
    c = pl.program_id(0)

    @pl.when(c == 0)
    def _():
        c_ref[...] = jnp.zeros_like(c_ref)
        n_ref[...] = jnp.zeros_like(n_ref)
        m_ref[...] = jnp.zeros_like(m_ref)

    n = qk_ref.shape[0]
    gates = gt_ref[...] + bias_ref[...]
    i_all = GATE_SOFTCAP * jnp.tanh(gates * (1.0 / GATE_SOFTCAP))
    f_all = -(jnp.maximum(-gates, 0.0) + jnp.log1p(jnp.exp(-jnp.abs(gates))))
    ri = lax.broadcasted_iota(I32, (n, n), 0)
    ci = lax.broadcasted_iota(I32, (n, n), 1)
    mask = (ci >= ri) if reverse else (ci <= ri)
    tri = jnp.where(mask, 1.0, 0.0).astype(F32)
    b_all = jnp.dot(tri, f_all, preferred_element_type=F32, precision=lax.Precision.HIGHEST)
    b_all_t = b_all.T
    i_all_t = i_all.T
    last = 0 if reverse else n - 1
    qk_w = heads * dk
    for h in range(heads):
        ic = d_idx * 2 * heads + h
        fc = ic + heads
        b_col = b_all[:, fc:fc + 1]
        b_row = b_all_t[fc:fc + 1, :]
        i_col = i_all[:, ic:ic + 1]
        i_row = i_all_t[ic:ic + 1, :]
        m_prev = m_ref[h][:, 0:1]
        q = qk_ref[:, h * dk:(h + 1) * dk]
        k = qk_ref[:, qk_w + h * dk:qk_w + (h + 1) * dk]
        v = v_ref[:, h * dv:(h + 1) * dv]
        c_prev = c_ref[h]
        n_prev = n_ref[h]
        d_log = jnp.where(mask, b_col - b_row + i_row, NEG_INF)
        m_t = jnp.maximum(b_col + m_prev, jnp.max(d_log, axis=-1, keepdims=True))
        w = jnp.exp(d_log - m_t)
        s = _dot_nt(q, k) * w
        inter = jnp.exp(b_col + m_prev - m_t)
        num = jnp.dot(s.astype(BF16), v, preferred_element_type=F32) + inter * jnp.dot(
            q, c_prev.astype(BF16), preferred_element_type=F32)
        den = jnp.sum(s, axis=-1, keepdims=True) + inter * jnp.sum(q.astype(F32) * n_prev, axis=-1, keepdims=True)
        o_ref[:, h * dv:(h + 1) * dv] = (num / jnp.maximum(jnp.abs(den), jnp.exp(-m_t))).astype(o_ref.dtype)
        b_last = b_col[last:last + 1, :]
        g = b_last - b_col + i_col
        m_new = jnp.maximum(b_last + m_prev, jnp.max(g, axis=0, keepdims=True))
        wk = jnp.exp(g - m_new) * k.astype(F32)
        dec = jnp.exp(b_last + m_prev - m_new)
        c_ref[h] = dec * c_prev + _dot_tn(wk.astype(BF16), v)
        n_ref[h] = dec * n_prev + jnp.sum(wk, axis=0, keepdims=True)
        m_ref[h] = jnp.broadcast_to(m_new, m_ref.shape[1:])


def _mlstm_scan(qk, vo, gates, bias, n_ctx, heads, dk, dv, reverse):
    t = qk.shape[0]
    nc, cc = t // CHUNK, n_ctx // CHUNK
    cm = lambda c: _chunk_index(c, nc, cc, reverse)
    return pl.pallas_call(
        functools.partial(_mlstm_kernel, reverse=reverse, heads=heads, dk=dk, dv=dv, d_idx=1 if reverse else 0),
        out_shape=jax.ShapeDtypeStruct((t, heads * dv), BF16),
        grid=(nc,),
        in_specs=[pl.BlockSpec((CHUNK, 2 * heads * dk), lambda c: (cm(c), 0)),
                  pl.BlockSpec((CHUNK, heads * dv), lambda c: (cm(c), 0)),
                  pl.BlockSpec((CHUNK, LANES), lambda c: (cm(c), 0)),
                  pl.BlockSpec((1, LANES), lambda c: (0, 0))],
        out_specs=pl.BlockSpec((CHUNK, heads * dv), lambda c: (cm(c), 0)),
        scratch_shapes=[pltpu.VMEM((heads, dk, dv), F32),
                        pltpu.VMEM((heads, 1, dk), F32),
                        pltpu.VMEM((heads, 1, LANES), F32)],
        compiler_params=_cparams(("arbitrary",)),
        name="mlstm_scan_rev" if reverse else "mlstm_scan_fwd",
    )(qk, vo, gates, bias)


def _mlstm_post_kernel(yf_ref, yb_ref, o_ref_in, w_ref, o_ref, *, heads, dv):
    for h in range(heads):
        sl = slice(h * dv, (h + 1) * dv)
        y = _rms(yf_ref[:, sl].astype(F32) + yb_ref[:, sl].astype(F32))
        o = o_ref_in[:, sl].astype(F32)
        o_ref[:, sl] = (y * w_ref[:, sl] * (1.0 / (1.0 + jnp.exp(-o)))).astype(o_ref.dtype)


def _mlstm_post(yf, yb, vo, norm_w, heads, dv):
    t, vw = yf.shape
    tm = ROW_TILE
    return pl.pallas_call(
        functools.partial(_mlstm_post_kernel, heads=heads, dv=dv),
        out_shape=jax.ShapeDtypeStruct((t, vw), BF16),
        grid=(t // tm,),
        in_specs=[pl.BlockSpec((tm, vw), lambda i: (i, 0)),
                  pl.BlockSpec((tm, vw), lambda i: (i, 0)),
                  pl.BlockSpec((tm, vw), lambda i: (i, 1)),
                  pl.BlockSpec((1, vw), lambda i: (0, 0))],
        out_specs=pl.BlockSpec((tm, vw), lambda i: (i, 0)),
        compiler_params=_cparams(("arbitrary",)),
        name="mlstm_post",
    )(yf, yb, vo, norm_w.reshape(1, vw))


def _qk_norm_rope_kernel(x_ref, cos_ref, sin_ref, qn_ref, kn_ref, q_ref, k_ref, *, q_heads, k_heads, hd):
    cos, sin = cos_ref[...], sin_ref[...]
    lane = lax.broadcasted_iota(I32, (1, hd), 1)
    first = (lane % (hd // 2)) < (hd // 4)
    for h in range(q_heads + k_heads):
        x = x_ref[:, h * hd:(h + 1) * hd]
        w = qn_ref[...] if h < q_heads else kn_ref[...]
        xh = _rms(x) * w
        partner = jnp.where(first, pltpu.roll(xh, hd - hd // 4, axis=1), pltpu.roll(xh, hd // 4, axis=1))
        y = (xh * cos + partner * sin).astype(q_ref.dtype)
        if h < q_heads:
            q_ref[:, h * hd:(h + 1) * hd] = y
        else:
            k_ref[:, (h - q_heads) * hd:(h - q_heads + 1) * hd] = y


def _qk_norm_rope(qk, cos, sin, q_norm, k_norm, q_heads, k_heads, hd):
    t = qk.shape[0]
    tm = ROW_TILE
    return pl.pallas_call(
        functools.partial(_qk_norm_rope_kernel, q_heads=q_heads, k_heads=k_heads, hd=hd),
        out_shape=(jax.ShapeDtypeStruct((t, q_heads * hd), BF16), jax.ShapeDtypeStruct((t, k_heads * hd), BF16)),
        grid=(t // tm,),
        in_specs=[pl.BlockSpec((tm, (q_heads + k_heads) * hd), lambda i: (i, 0)),
                  pl.BlockSpec((tm, hd), lambda i: (i, 0)),
                  pl.BlockSpec((tm, hd), lambda i: (i, 0)),
                  pl.BlockSpec((1, hd), lambda i: (0, 0)),
                  pl.BlockSpec((1, hd), lambda i: (0, 0))],
        out_specs=(pl.BlockSpec((tm, q_heads * hd), lambda i: (i, 0)),
                   pl.BlockSpec((tm, k_heads * hd), lambda i: (i, 0))),
        compiler_params=_cparams(("arbitrary",)),
        name="attn_qk_norm_rope",
    )(qk, cos, sin, q_norm.reshape(1, hd), k_norm.reshape(1, hd))


def _flash_kernel(q_ref, k_ref, v_ref, o_ref, m_ref, l_ref, acc_ref, alpha_ref, sa_ref, sb_ref, p_ref,
                  *, groups, hd, tk, n_ctx):
    tq = q_ref.shape[0]
    n_kv = k_ref.shape[0]
    n = n_kv // tk
    q = jnp.concatenate([q_ref[:, g * hd:(g + 1) * hd] for g in range(groups)], axis=0)
    m_ref[...] = jnp.full_like(m_ref, NEG_INF)
    l_ref[...] = jnp.zeros_like(l_ref)
    acc_ref[...] = jnp.zeros_like(acc_ref)

    def scores(off, size):
        return _dot_nt(q, k_ref[pl.ds(off, size), :])

    def absorb(s_ref, off, size):
        for band in range(groups * tq // FLASH_BAND):
            rows = slice(band * FLASH_BAND, (band + 1) * FLASH_BAND)
            s = s_ref[rows, :size]
            m_prev = m_ref[rows, :]
            m_new = jnp.maximum(m_prev, jnp.max(s, axis=-1, keepdims=True))
            alpha = jnp.exp2(m_prev - m_new)
            p = jnp.exp2(s - jnp.concatenate([m_new] * (size // LANES), axis=1))
            l_ref[rows, :] = alpha * l_ref[rows, :] + jnp.sum(p, axis=-1, keepdims=True)
            m_ref[rows, :] = m_new
            alpha_ref[rows, :] = alpha
            p_ref[rows, :size] = p.astype(BF16)
        acc_ref[...] = alpha_ref[...] * acc_ref[...] + jnp.dot(p_ref[:, :size], v_ref[pl.ds(off, size), :],
                                                                preferred_element_type=F32)

    is_ctx = pl.program_id(1) < n_ctx // tq

    @pl.when(is_ctx)
    def _():
        sa_ref[:, :n_ctx] = scores(0, n_ctx)
        absorb(sa_ref, 0, n_ctx)

    @pl.when(jnp.logical_not(is_ctx))
    def _():
        bufs = (sa_ref, sb_ref)
        bufs[0][...] = scores(0, tk)
        for j in range(n):
            if j + 1 < n:
                bufs[(j + 1) % 2][...] = scores((j + 1) * tk, tk)
            absorb(bufs[j % 2], j * tk, tk)

    out = acc_ref[...] / l_ref[...]
    for g in range(groups):
        o_ref[:, g * hd:(g + 1) * hd] = out[g * tq:(g + 1) * tq].astype(o_ref.dtype)


def _flash_attention(q, k, v, n_ctx, kv_heads, groups, hd):
    t = q.shape[0]
    tq = _pick_tile(math.gcd(t, n_ctx), (256, 128))
    tk = _pick_tile(t, (768, 512, 256, 128))
    assert hd == LANES and n_ctx <= tk and (groups * tq) % FLASH_BAND == 0
    gw = groups * hd
    return pl.pallas_call(
        functools.partial(_flash_kernel, groups=groups, hd=hd, tk=tk, n_ctx=n_ctx),
        out_shape=jax.ShapeDtypeStruct((t, kv_heads * gw), BF16),
        grid=(kv_heads, t // tq),
        in_specs=[pl.BlockSpec((tq, gw), lambda g, i: (i, g)),
                  pl.BlockSpec((t, hd), lambda g, i: (0, g)),
                  pl.BlockSpec((t, hd), lambda g, i: (0, g))],
        out_specs=pl.BlockSpec((tq, gw), lambda g, i: (i, g)),
        scratch_shapes=[pltpu.VMEM((groups * tq, LANES), F32),
                        pltpu.VMEM((groups * tq, LANES), F32),
                        pltpu.VMEM((groups * tq, hd), F32),
                        pltpu.VMEM((groups * tq, LANES), F32),
                        pltpu.VMEM((groups * tq, tk), F32),
                        pltpu.VMEM((groups * tq, tk), F32),
                        pltpu.VMEM((groups * tq, tk), BF16)],
        compiler_params=_cparams(("arbitrary", "arbitrary")),
        name="flash_attention",
    )(q, k, v)


def _router_kernel(x_ref, g_ref, sh_ref, sc_ref, wr_ref, br_ref, h_ref, ids_ref, wts_ref, cnt_ref, run_ref,
                   *, n_ctx, row0):
    i = pl.program_id(0)
    tm, d = x_ref.shape

    @pl.when(i == 0)
    def _():
        run_ref[...] = jnp.zeros_like(run_ref)

    ib = i + row0 // tm
    h = _rms(x_ref[...]) * g_ref[...]
    h = h * (1.0 + _mod_row(sc_ref, ib, tm, n_ctx)) + _mod_row(sh_ref, ib, tm, n_ctx)
    h_ref[...] = h
    h_hi = h.astype(BF16)
    h_lo = (h - h_hi.astype(F32)).astype(BF16)
    hw = jnp.dot(h_hi, wr_ref[...], preferred_element_type=F32)
    lw = jnp.dot(h_lo, wr_ref[:, :LANES], preferred_element_type=F32)
    logits = hw[:, :LANES] + (hw[:, LANES:] + lw) + br_ref[...]
    lane = lax.broadcasted_iota(I32, (tm, LANES), 1)
    big = jnp.int32(LANES)
    gl = jnp.where((lane >= MOE_EXPERTS) & (lane < MOE_EXPERTS + MOE_GROUPS), logits, NEG_INF)
    g_max = jnp.max(gl, axis=-1, keepdims=True)
    g_idx = jnp.min(jnp.where(gl == g_max, lane, big), axis=-1, keepdims=True) - MOE_EXPERTS
    g_w = 1.0 / jnp.sum(jnp.exp(gl - g_max), axis=-1, keepdims=True)
    el = jnp.where((lane < MOE_EXPERTS) & ((lane >> 3) == g_idx), logits, NEG_INF)
    m1 = jnp.max(el, axis=-1, keepdims=True)
    i1 = jnp.min(jnp.where(el == m1, lane, big), axis=-1, keepdims=True)
    el2 = jnp.where(lane == i1, NEG_INF, el)
    m2 = jnp.max(el2, axis=-1, keepdims=True)
    i2 = jnp.min(jnp.where(el2 == m2, lane, big), axis=-1, keepdims=True)
    e_sum = jnp.sum(jnp.exp(el - m1), axis=-1, keepdims=True)
    p1 = 1.0 / e_sum
    p2 = jnp.exp(m2 - m1) / e_sum
    w1 = g_w * (p1 / (p1 + p2))
    w2 = g_w * (p2 / (p1 + p2))
    onehot = jnp.where((lane == i1) | (lane == i2), 1.0, 0.0)
    ri = lax.broadcasted_iota(I32, (tm, tm), 0)
    ci = lax.broadcasted_iota(I32, (tm, tm), 1)
    strict = jnp.where(ci < ri, 1.0, 0.0).astype(BF16)
    before = jnp.dot(strict, onehot.astype(BF16), preferred_element_type=F32) + run_ref[0:1, :]
    r1 = jnp.sum(jnp.where(lane == i1, before, 0.0), axis=-1, keepdims=True).astype(I32)
    r2 = jnp.sum(jnp.where(lane == i2, before, 0.0), axis=-1, keepdims=True).astype(I32)
    ids_ref[...] = jnp.where(lane == 0, i1, jnp.where(lane == 1, i2, jnp.where(lane == 2, r1, jnp.where(lane == 3, r2, 0))))
    wts_ref[...] = jnp.where(lane == 0, w1, jnp.where(lane == 1, w2, 0.0))
    total = run_ref[0:1, :] + jnp.sum(onehot, axis=0, keepdims=True)
    run_ref[...] = jnp.broadcast_to(total, run_ref.shape)
    cnt_ref[...] = jnp.broadcast_to(total, cnt_ref.shape)


def _router(x, g, mod, k_shift, k_scale, w_router, b_router, n_ctx, row0):
    t, d = x.shape
    tm = ROW_TILE
    n = t - row0
    off = row0 // tm
    return pl.pallas_call(
        functools.partial(_router_kernel, n_ctx=n_ctx, row0=row0),
        out_shape=(jax.ShapeDtypeStruct((n, d), F32),
                   jax.ShapeDtypeStruct((n, LANES), I32),
                   jax.ShapeDtypeStruct((n, LANES), F32),
                   jax.ShapeDtypeStruct((8, LANES), F32)),
        grid=(n // tm,),
        in_specs=[pl.BlockSpec((tm, d), lambda i: (i + off, 0)),
                  pl.BlockSpec((1, d), lambda i: (0, 0)),
                  pl.BlockSpec((8, d), lambda i: (0, k_shift)),
                  pl.BlockSpec((8, d), lambda i: (0, k_scale)),
                  pl.BlockSpec((d, 2 * LANES), lambda i: (0, 0)),
                  pl.BlockSpec((1, LANES), lambda i: (0, 0))],
        out_specs=(pl.BlockSpec((tm, d), lambda i: (i, 0)),
                   pl.BlockSpec((tm, LANES), lambda i: (i, 0)),
                   pl.BlockSpec((tm, LANES), lambda i: (i, 0)),
                   pl.BlockSpec((8, LANES), lambda i: (0, 0))),
        scratch_shapes=[pltpu.VMEM((8, LANES), F32)],
        compiler_params=_cparams(("arbitrary",)),
        name="moe_router",
    )(x, g.reshape(1, d), mod, mod, w_router, b_router)


def _expert_kernel(be_ref, first_ref, next_ref, nb_ref, tbl_ref, tbl_next_ref, h_hbm, wg_hbm, wu_hbm, wd_hbm, out_hbm,
                   wg_st, wu_st, wd_st, wg_bf, wu_bf, wd_bf, xbuf, ybuf, sem_w, sem_in, sem_out, *, layer, n_rows):
    b = pl.program_id(0)
    nb = pl.num_programs(0)
    n_used = nb_ref[0]
    used = b < n_used
    slot = b % 2
    weights = ((wg_hbm, wg_st, wg_bf), (wu_hbm, wu_st, wu_bf), (wd_hbm, wd_st, wd_bf))

    def fetch(e, idx):
        hbm, st, _ = weights[idx]
        return pltpu.make_async_copy(hbm.at[layer, e], st, sem_w.at[idx])

    def gather_row(table, buf_slot, r):
        src = jnp.minimum(table[0, r] >> 1, n_rows - 1)
        return pltpu.make_async_copy(h_hbm.at[pl.ds(src, 1)], xbuf.at[buf_slot, pl.ds(r, 1)], sem_in.at[buf_slot])

    def scatter_row(table, buf_slot, r):
        return pltpu.make_async_copy(ybuf.at[buf_slot, pl.ds(r, 1)], out_hbm.at[pl.ds(table[0, r], 1)],
                                     sem_out.at[buf_slot])

    def for_rows(fn):
        def body(r, carry):
            fn(r)
            return carry
        lax.fori_loop(0, MOE_BLOCK, body, 0, unroll=8)

    @pl.when(b == 0)
    def _():
        for idx in range(len(weights)):
            fetch(be_ref[0], idx).start()
        for_rows(lambda r: gather_row(tbl_ref, 0, r).start())
        ybuf[...] = jnp.zeros_like(ybuf)
        spare = [pltpu.make_async_copy(ybuf.at[s], out_hbm.at[pl.ds(MOE_TOPK * n_rows + s * MOE_BLOCK, MOE_BLOCK)],
                                       sem_out.at[s]) for s in range(2)]
        for cp in spare:
            cp.start()
        for cp in spare:
            cp.wait()

    @pl.when(b + 1 < n_used)
    def _():
        for_rows(lambda r: gather_row(tbl_next_ref, 1 - slot, r).start())

    @pl.when(used & (first_ref[b] == 1))
    def _():
        has_next = next_ref[b] >= 0
        for idx, (_, st, bf) in enumerate(weights):
            fetch(be_ref[b], idx).wait()
            bf[...] = st[...].astype(BF16)

            @pl.when(has_next)
            def _():
                fetch(next_ref[b], idx).start()

    def drain(buf_slot):
        for_rows(lambda r: scatter_row(tbl_ref, buf_slot, r).wait())

    @pl.when((b >= 2) & (b - 2 < n_used))
    def _():
        drain(slot)

    @pl.when(used)
    def _():
        for_rows(lambda r: gather_row(tbl_ref, slot, r).wait())
        x = xbuf[slot].astype(BF16)
        gate = jnp.dot(x, wg_bf[...], preferred_element_type=F32)
        up = jnp.dot(x, wu_bf[...], preferred_element_type=F32)
        act = (gate * (1.0 / (1.0 + jnp.exp(-gate))) * up).astype(BF16)
        ybuf[slot] = jnp.dot(act, wd_bf[...], preferred_element_type=F32)
        for_rows(lambda r: scatter_row(tbl_ref, slot, r).start())

    @pl.when((b == nb - 1) & (b - 1 < n_used))
    def _():
        drain(1 - slot)


def _experts(block_expert, first, next_expert, n_used, table, h, w_gate, w_up, w_down, layer):
    n_rows, d = h.shape
    ff = w_gate.shape[3]
    nb = table.shape[0]
    tbl_spec = pl.BlockSpec((None, 1, MOE_BLOCK), lambda b, *_: (b, 0, 0), memory_space=pltpu.SMEM)
    tbl_next_spec = pl.BlockSpec((None, 1, MOE_BLOCK), lambda b, *_: (jnp.minimum(b + 1, nb - 1), 0, 0),
                                 memory_space=pltpu.SMEM)
    hbm = pl.BlockSpec(memory_space=pl.ANY)
    return pl.pallas_call(
        functools.partial(_expert_kernel, layer=layer, n_rows=n_rows),
        out_shape=jax.ShapeDtypeStruct((MOE_TOPK * n_rows + 2 * MOE_BLOCK, d), F32),
        grid_spec=pltpu.PrefetchScalarGridSpec(
            num_scalar_prefetch=4,
            grid=(nb,),
            in_specs=[tbl_spec, tbl_next_spec, hbm, hbm, hbm, hbm],
            out_specs=hbm,
            scratch_shapes=[pltpu.VMEM((d, ff), F32), pltpu.VMEM((d, ff), F32), pltpu.VMEM((ff, d), F32),
                            pltpu.VMEM((d, ff), BF16), pltpu.VMEM((d, ff), BF16), pltpu.VMEM((ff, d), BF16),
                            pltpu.VMEM((2, MOE_BLOCK, d), F32), pltpu.VMEM((2, MOE_BLOCK, d), F32),
                            pltpu.SemaphoreType.DMA((3,)), pltpu.SemaphoreType.DMA((2,)),
                            pltpu.SemaphoreType.DMA((2,))]),
        compiler_params=_cparams(("arbitrary",)),
        name="moe_experts",
    )(block_expert, first, next_expert, n_used, table, table, h, w_gate, w_up, w_down)


def _combine_kernel(x_ref, wts_ref, mod_ref, y_ref, g_ref, sh_ref, sc_ref, *rest, n_ctx, row0, last):
    i = pl.program_id(0)
    tm, d = x_ref.shape
    w = wts_ref[...]
    y = w[:, 0:1] * y_ref[:, :d] + w[:, 1:2] * y_ref[:, d:]
    ib = i + row0 // tm
    x_new = x_ref[...] + _mod_row(mod_ref, ib, tm, n_ctx) * y
    hn = _rms(x_new) * g_ref[...]
    if last:
        (o_ref,) = rest
        o_ref[...] = hn
    else:
        o_ref, h_ref = rest
        o_ref[...] = x_new
        hn = hn * (1.0 + _mod_row(sc_ref, ib, tm, n_ctx)) + _mod_row(sh_ref, ib, tm, n_ctx)
        h_ref[...] = hn.astype(h_ref.dtype)


def _combine(x, wts, mod, k_gate, y2, n_ctx, row0, g_next, mod_next, last):
    t, d = x.shape
    tm = ROW_TILE
    n = t - row0
    off = row0 // tm
    row_spec = pl.BlockSpec((tm, d), lambda i: (i, 0))
    out_shape = jax.ShapeDtypeStruct((n, d), F32)
    out_specs = row_spec
    if not last:
        out_shape = (out_shape, jax.ShapeDtypeStruct((n, d), BF16))
        out_specs = (row_spec, row_spec)
    return pl.pallas_call(
        functools.partial(_combine_kernel, n_ctx=n_ctx, row0=row0, last=last),
        out_shape=out_shape,
        grid=(n // tm,),
        in_specs=[pl.BlockSpec((tm, d), lambda i: (i + off, 0)),
                  pl.BlockSpec((tm, LANES), lambda i: (i, 0)),
                  pl.BlockSpec((8, d), lambda i: (0, k_gate)),
                  pl.BlockSpec((tm, MOE_TOPK * d), lambda i: (i, 0)),
                  pl.BlockSpec((1, d), lambda i: (0, 0)),
                  pl.BlockSpec((8, d), lambda i: (0, 0)),
                  pl.BlockSpec((8, d), lambda i: (0, 1))],
        out_specs=out_specs,
        compiler_params=_cparams(("arbitrary",)),
        name="moe_combine",
    )(x, wts, mod, y2, g_next.reshape(1, d), mod_next, mod_next)


def _hier_moe(x, g, mod, w_rg, b_rg, w_re, b_re, w_gate, w_up, w_down, layer, n_ctx, row0, g_next, mod_next, last):
    t, d = x.shape
    n = t - row0
    pad = LANES - MOE_EXPERTS - MOE_GROUPS
    w_router = jnp.concatenate([w_re, w_rg, jnp.zeros((d, pad), F32)], axis=1)
    w_hi = w_router.astype(BF16)
    w_router = jnp.concatenate([w_hi, (w_router - w_hi.astype(F32)).astype(BF16)], axis=1)
    b_router = jnp.concatenate([b_re, b_rg, jnp.zeros((pad,), F32)]).reshape(1, LANES)
    h, ids, wts, counts = _router(x, g, mod, 3, 4, w_router, b_router, n_ctx, row0)
    counts = counts[0, :MOE_EXPERTS].astype(I32)
    padded = (counts + MOE_BLOCK - 1) // MOE_BLOCK * MOE_BLOCK
    seg_end = jnp.cumsum(padded)
    seg_start = seg_end - padded
    dest = (seg_start[ids[:, :MOE_TOPK]] + ids[:, MOE_TOPK:2 * MOE_TOPK]).reshape(-1)
    n_blocks = (n * MOE_TOPK + MOE_EXPERTS * (MOE_BLOCK - 1)) // MOE_BLOCK + 1
    block_start = jnp.arange(n_blocks, dtype=I32) * MOE_BLOCK
    block_expert = jnp.minimum(jnp.sum((seg_end[None, :] <= block_start[:, None]).astype(I32), axis=1),
                               MOE_EXPERTS - 1).astype(I32)
    n_used = (seg_end[-1:] // MOE_BLOCK).astype(I32)
    block_id = jnp.arange(n_blocks, dtype=I32)
    prev_expert = jnp.concatenate([jnp.full((1,), -1, I32), block_expert[:-1]])
    first = ((block_id < n_used[0]) & (block_expert != prev_expert)).astype(I32)
    eid = jnp.arange(MOE_EXPERTS, dtype=I32)
    owner = jnp.where(padded > 0, eid, MOE_EXPERTS)
    later = jnp.concatenate([lax.cummin(owner[::-1])[::-1][1:], jnp.full((1,), MOE_EXPERTS, I32)])
    next_expert = jnp.where(later < MOE_EXPERTS, later, -1)[block_expert].astype(I32)
    slot_id = jnp.arange(n_blocks * MOE_BLOCK, dtype=I32)
    table = (MOE_TOPK * n + slot_id % (2 * MOE_BLOCK)).at[dest].set(jnp.arange(MOE_TOPK * n, dtype=I32))
    out2 = _experts(block_expert, first, next_expert, n_used, table.reshape(n_blocks, 1, MOE_BLOCK), h,
                    w_gate, w_up, w_down, layer)
    y2 = out2.reshape(-1, MOE_TOPK * d)
    return _combine(x, wts, mod, 5, y2, n_ctx, row0, g_next, mod_next, last)


def _rope_table(pos, dim, base):
    inv = jnp.power(base, -jnp.arange(0, dim, 2, dtype=F32) / dim)
    ang = pos.astype(F32)[:, None] * inv[None, :]
    return jnp.cos(ang), jnp.sin(ang)


def _flipped_positions(t, n_ctx):
    idx = jnp.arange(t, dtype=I32)
    return jnp.where(idx < n_ctx, n_ctx - 1 - idx, n_ctx + (t - 1 - idx))


def _axial_tables(n_ctx, n_lat, hd):
    rows = n_lat // GRID_W
    row = jnp.repeat(jnp.arange(rows, dtype=I32), GRID_W)
    col = jnp.tile(jnp.arange(GRID_W, dtype=I32), rows)
    half = hd // 2
    cr, sr = _rope_table(row, half, ROPE_BASE)
    cc, sc = _rope_table(col, half, ROPE_BASE)
    cos = jnp.concatenate([cr, cr, cc, cc], axis=1)
    sin = jnp.concatenate([-sr, sr, -sc, sc], axis=1)
    cos = jnp.concatenate([jnp.ones((n_ctx, hd), F32), cos], axis=0)
    sin = jnp.concatenate([jnp.zeros((n_ctx, hd), F32), sin], axis=0)
    return cos, sin


def _retention_layer(xa, h, mod, w_in, logit_gamma, gn_w, gn_b, w_out, j, n_ctx):
    t, d = xa.shape
    heads = RET_HEADS
    dk = d // heads
    dv = 2 * dk
    qk_w, v_w = heads * dk, heads * dv
    pos_f = jnp.arange(t, dtype=I32)
    cf, sf = _rope_table(pos_f, dk, RET_ROPE_BASE)
    cb, sb = _rope_table(_flipped_positions(t, n_ctx), dk, RET_ROPE_BASE)
    tables = (cf, sf, cb, sb)
    q = _matmul_rope(h, w_in, j, 0, qk_w, tables, dk ** -0.5, dk)
    k = _matmul_rope(h, w_in, j, qk_w, qk_w, tables, 1.0, dk)
    vg = _matmul(h, w_in, j, 2 * qk_w, 2 * v_w, BF16)
    log_gamma = jax.nn.log_sigmoid(logit_gamma.astype(F32))
    yf = _retention_scan(log_gamma[0], q, k, vg, 0, n_ctx, heads, dk, dv, reverse=False)
    yb = _retention_scan(log_gamma[1], q, k, vg, 1, n_ctx, heads, dk, dv, reverse=True)
    yn = _ret_post(yf, yb, vg, gn_w, gn_b, heads, dv)
    return _matmul_residual(yn, w_out, j, xa, mod, 2, n_ctx)


def _mlstm_layer(xa, h, mod, w_in, conv_w, b_gate, norm_w, w_out, j, n_ctx):
    t, d = xa.shape
    heads = MLSTM_HEADS
    dk = d // (2 * heads)
    dv = d // heads
    qk_w, v_w = heads * dk, heads * dv
    n_gates = 4 * heads
    qk_pre = _matmul(h, w_in, j, 0, 2 * qk_w, F32)
    vo = _matmul(h, w_in, j, 2 * qk_w, 2 * v_w, BF16)
    w_gates = jnp.pad(w_in[j, :, 2 * qk_w + 2 * v_w:], ((0, 0), (0, LANES - n_gates)))
    gates = _matmul(h, w_gates[None], 0, 0, LANES, F32)
    bias = jnp.pad(b_gate.astype(F32).reshape(1, n_gates), ((0, 0), (0, LANES - n_gates)))
    qk = _conv_silu(qk_pre, conv_w, n_ctx, qk_w, dk ** -0.5)
    yf = _mlstm_scan(qk, vo, gates, bias, n_ctx, heads, dk, dv, reverse=False)
    yb = _mlstm_scan(qk, vo, gates, bias, n_ctx, heads, dk, dv, reverse=True)
    yn = _mlstm_post(yf, yb, vo, norm_w, heads, dv)
    return _matmul_residual(yn, w_out, j, xa, mod, 2, n_ctx)


def _attention_layer(xa, h, mod, w_in, q_norm, k_norm, w_out, j, n_ctx):
    t, d = xa.shape
    hd = d // ATTN_HEADS
    groups = ATTN_HEADS // ATTN_KV_HEADS
    q_w, kv_w = ATTN_HEADS * hd, ATTN_KV_HEADS * hd
    n_lat = t - n_ctx
    qk_pre = _matmul(h, w_in, j, 0, q_w + kv_w, F32)
    v = _matmul(h, w_in, j, q_w + kv_w, kv_w, BF16)
    cos, sin = _axial_tables(n_ctx, n_lat, hd)
    q_gain = q_norm * (hd ** -0.5 * math.log2(math.e))
    q, k = _qk_norm_rope(qk_pre, cos, sin, q_gain, k_norm, ATTN_HEADS, ATTN_KV_HEADS, hd)
    o = _flash_attention(q, k, v, n_ctx, ATTN_KV_HEADS, groups, hd)
    return _matmul_residual(o, w_out, j, xa, mod, 2, n_ctx)


def kernel(x, c, ctx, c_ctx, w_ada, b_ada, norm_g, ret_w_in, ret_logit_gamma, ret_gn_w, ret_gn_b, ret_w_out, mlstm_w_in, mlstm_conv_w, mlstm_b_gate, mlstm_norm_w, mlstm_w_out, attn_w_in, attn_q_norm, attn_k_norm, attn_w_out, moe_w_router_group, moe_b_router_group, moe_w_router_expert, moe_b_router_expert, moe_w_gate, moe_w_up, moe_w_down, final_norm_g):
    bsz, n_lat, d = x.shape
    n_ctx = ctx.shape[1]
    depth = w_ada.shape[0]
    assert bsz == 1 and n_ctx % ROW_TILE == 0 and n_lat % ROW_TILE == 0 and n_lat % GRID_W == 0
    xa = jnp.concatenate([ctx[0], x[0]], axis=0)
    s = jnp.stack([jax.nn.silu(c[0]), jax.nn.silu(c_ctx)])
    s8 = jnp.zeros((8, d), F32).at[:2].set(s).astype(BF16)
    mods = _ada_modulation(s8, w_ada, b_ada)
    h = _norm_mod(xa, norm_g[0, 0], mods[0], 0, 1, n_ctx)
    for i in range(depth):
        kind, j = i % 3, i // 3
        mod = mods[i]
        last = i == depth - 1
        if kind == 0:
            xa = _retention_layer(xa, h, mod, ret_w_in, ret_logit_gamma[j], ret_gn_w[j], ret_gn_b[j],
                                  ret_w_out, j, n_ctx)
        elif kind == 1:
            xa = _mlstm_layer(xa, h, mod, mlstm_w_in, mlstm_conv_w[j], mlstm_b_gate[j],
                              mlstm_norm_w[j], mlstm_w_out, j, n_ctx)
        else:
            xa = _attention_layer(xa, h, mod, attn_w_in, attn_q_norm[j], attn_k_norm[j],
                                  attn_w_out, j, n_ctx)
        row0 = n_ctx if last else 0
        g_next = final_norm_g if last else norm_g[i + 1, 0]
        mod_next = mod if last else mods[i + 1]
        out = _hier_moe(xa, norm_g[i, 1], mod, moe_w_router_group[i], moe_b_router_group[i],
                        moe_w_router_expert[i], moe_b_router_expert[i], moe_w_gate, moe_w_up, moe_w_down, i,
                        n_ctx, row0, g_next, mod_next, last)
        if last:
            return out[None]
        xa, h = out
```

```python
import functools
import math

import jax
import jax.numpy as jnp
import numpy as np
from jax import lax
from jax.experimental import pallas as pl
from jax.experimental.pallas import tpu as pltpu

F32 = jnp.float32
BF16 = jnp.bfloat16
I32 = jnp.int32
U32 = jnp.uint32

LANES = 128
CHUNK = 128
NORM_EPS = 1e-6
GRID_W = 64
RET_HEADS = 8
RET_ROPE_BASE = 10000.0
MLSTM_HEADS = 8
GATE_SOFTCAP = 15.0
ATTN_HEADS = 16
ATTN_KV_HEADS = 8
ROPE_BASE = 10000.0
MOE_GROUPS = 4
MOE_PER_GROUP = 8
MOE_EXPERTS = MOE_GROUPS * MOE_PER_GROUP
MOE_TOPK = 2
MOE_BLOCK = 256
ROW_TILE = 256
FLASH_BAND = 32
VMEM_LIMIT = 56 * 1024 * 1024
NEG_INF = float("-inf")


def _cparams(sem, vmem=VMEM_LIMIT):
    return pltpu.CompilerParams(dimension_semantics=sem, vmem_limit_bytes=vmem)


def _pick_tile(total, candidates):
    for c in candidates:
        if total % c == 0:
            return c
    raise ValueError(f"no tile for {total} in {candidates}")


def _mod_row(mod_ref, i, tile, n_ctx):
    sel = jnp.where(i < n_ctx // tile, 1, 0)
    return mod_ref[pl.ds(sel, 1), :]


def _ada_kernel(s_ref, w_ref, b_ref, o_ref):
    w = w_ref[0].astype(BF16)
    o_ref[0] = jnp.dot(s_ref[...], w, preferred_element_type=F32) + b_ref[0]


def _ada_modulation(s8, w_ada, b_ada):
    depth, d, n = w_ada.shape
    tn = _pick_tile(n, (1024, 512, 256, 128))
    return pl.pallas_call(
        _ada_kernel,
        out_shape=jax.ShapeDtypeStruct((depth, 8, n), F32),
        grid=(depth, n // tn),
        in_specs=[pl.BlockSpec((8, d), lambda l, j: (0, 0)),
                  pl.BlockSpec((1, d, tn), lambda l, j: (l, 0, j)),
                  pl.BlockSpec((1, 1, tn), lambda l, j: (l, 0, j))],
        out_specs=pl.BlockSpec((1, 8, tn), lambda l, j: (l, 0, j)),
        compiler_params=_cparams(("arbitrary", "arbitrary")),
        name="ada_modulation",
    )(s8, w_ada, b_ada.reshape(depth, 1, n))


def _rms(x):
    return x * lax.rsqrt(jnp.mean(x * x, axis=-1, keepdims=True) + NORM_EPS)


def _norm_mod_kernel(x_ref, g_ref, sh_ref, sc_ref, o_ref, *, n_ctx):
    i = pl.program_id(0)
    tm = x_ref.shape[0]
    y = _rms(x_ref[...]) * g_ref[...]
    y = y * (1.0 + _mod_row(sc_ref, i, tm, n_ctx)) + _mod_row(sh_ref, i, tm, n_ctx)
    o_ref[...] = y.astype(o_ref.dtype)


def _norm_mod(x, g, mod, k_shift, k_scale, n_ctx):
    t, d = x.shape
    tm = ROW_TILE
    return pl.pallas_call(
        functools.partial(_norm_mod_kernel, n_ctx=n_ctx),
        out_shape=jax.ShapeDtypeStruct((t, d), BF16),
        grid=(t // tm,),
        in_specs=[pl.BlockSpec((tm, d), lambda i: (i, 0)),
                  pl.BlockSpec((1, d), lambda i: (0, 0)),
                  pl.BlockSpec((8, d), lambda i: (0, k_shift)),
                  pl.BlockSpec((8, d), lambda i: (0, k_scale))],
        out_specs=pl.BlockSpec((tm, d), lambda i: (i, 0)),
        compiler_params=_cparams(("arbitrary",)),
        name="norm_mod",
    )(x, g.reshape(1, d), mod, mod)


def _cast_weight_once(w_ref, wbf_ref):
    @pl.when(pl.program_id(1) == 0)
    def _():
        wbf_ref[...] = w_ref[...].astype(BF16)


def _mm_plain_kernel(a_ref, w_ref, o_ref, wbf_ref, *, scale):
    _cast_weight_once(w_ref, wbf_ref)
    acc = jnp.dot(a_ref[...], wbf_ref[...], preferred_element_type=F32)
    if scale != 1.0:
        acc = acc * scale
    o_ref[...] = acc.astype(o_ref.dtype)


def _mm_residual_kernel(a_ref, w_ref, x_ref, mod_ref, o_ref, wbf_ref, *, n_ctx):
    _cast_weight_once(w_ref, wbf_ref)
    tm = a_ref.shape[0]
    acc = jnp.dot(a_ref[...], wbf_ref[...], preferred_element_type=F32)
    row = pl.program_id(1) * tm + lax.broadcasted_iota(I32, (tm, 1), 0)
    gate = jnp.where(row < n_ctx, mod_ref[1:2, :], mod_ref[0:1, :])
    o_ref[...] = x_ref[...] + gate * acc


def _mm_rope_kernel(a_ref, w_ref, cf_ref, sf_ref, cb_ref, sb_ref, o_ref, wbf_ref, *, scale, head_dim):
    _cast_weight_once(w_ref, wbf_ref)
    acc = jnp.dot(a_ref[...], wbf_ref[...], preferred_element_type=F32)
    if scale != 1.0:
        acc = acc * scale
    half = head_dim // 2
    for d, (c_ref, s_ref) in enumerate(((cf_ref, sf_ref), (cb_ref, sb_ref))):
        c, s = c_ref[...], s_ref[...]
        for h in range(acc.shape[1] // head_dim):
            x1 = acc[:, h * head_dim:h * head_dim + half]
            x2 = acc[:, h * head_dim + half:(h + 1) * head_dim]
            o_ref[d, :, h * head_dim:h * head_dim + half] = (x1 * c - x2 * s).astype(o_ref.dtype)
            o_ref[d, :, h * head_dim + half:(h + 1) * head_dim] = (x1 * s + x2 * c).astype(o_ref.dtype)


def _mm_tiles(m, k, n):
    tm = _pick_tile(m, (768, 512, 384, 256, 128))
    tn_cap = 1024 if k <= 2048 else 512
    tn = _pick_tile(n, tuple(c for c in (1024, 512, 256, 128) if c <= tn_cap))
    return tm, tn


def _matmul(a, w, layer, col0, n, out_dtype, scale=1.0):
    m, k = a.shape
    tm, tn = _mm_tiles(m, k, n)
    assert col0 % tn == 0
    c0 = col0 // tn
    return pl.pallas_call(
        functools.partial(_mm_plain_kernel, scale=scale),
        out_shape=jax.ShapeDtypeStruct((m, n), out_dtype),
        grid=(n // tn, m // tm),
        in_specs=[pl.BlockSpec((tm, k), lambda j, i: (i, 0)),
                  pl.BlockSpec((None, k, tn), lambda j, i: (layer, 0, j + c0))],
        out_specs=pl.BlockSpec((tm, tn), lambda j, i: (i, j)),
        scratch_shapes=[pltpu.VMEM((k, tn), BF16)],
        compiler_params=_cparams(("arbitrary", "arbitrary")),
        name="matmul",
    )(a, w)


def _matmul_residual(a, w, layer, x, mod, k_gate, n_ctx):
    m, k = a.shape
    n = w.shape[2]
    tm, tn = _mm_tiles(m, k, n)
    kb = k_gate * (n // tn)
    return pl.pallas_call(
        functools.partial(_mm_residual_kernel, n_ctx=n_ctx),
        out_shape=jax.ShapeDtypeStruct((m, n), F32),
        grid=(n // tn, m // tm),
        in_specs=[pl.BlockSpec((tm, k), lambda j, i: (i, 0)),
                  pl.BlockSpec((None, k, tn), lambda j, i: (layer, 0, j)),
                  pl.BlockSpec((tm, tn), lambda j, i: (i, j)),
                  pl.BlockSpec((8, tn), lambda j, i: (0, kb + j))],
        out_specs=pl.BlockSpec((tm, tn), lambda j, i: (i, j)),
        scratch_shapes=[pltpu.VMEM((k, tn), BF16)],
        compiler_params=_cparams(("arbitrary", "arbitrary")),
        name="matmul_residual",
    )(a, w, x, mod)


def _matmul_rope(a, w, layer, col0, n, tables, scale, head_dim):
    m, k = a.shape
    tm, tn = _mm_tiles(m, k, n)
    assert col0 % tn == 0 and tn % head_dim == 0
    c0 = col0 // tn
    half = head_dim // 2
    tab_spec = pl.BlockSpec((tm, half), lambda j, i: (i, 0))
    return pl.pallas_call(
        functools.partial(_mm_rope_kernel, scale=scale, head_dim=head_dim),
        out_shape=jax.ShapeDtypeStruct((2, m, n), BF16),
        grid=(n // tn, m // tm),
        in_specs=[pl.BlockSpec((tm, k), lambda j, i: (i, 0)),
                  pl.BlockSpec((None, k, tn), lambda j, i: (layer, 0, j + c0)),
                  tab_spec, tab_spec, tab_spec, tab_spec],
        out_specs=pl.BlockSpec((2, tm, tn), lambda j, i: (0, i, j)),
        scratch_shapes=[pltpu.VMEM((k, tn), BF16)],
        compiler_params=_cparams(("arbitrary", "arbitrary")),
        name="matmul_rope",
    )(a, w, *tables)


def _chunk_index(c, n_chunks, ctx_chunks, reverse):
    if not reverse:
        return c
    return jnp.where(c < ctx_chunks, ctx_chunks - 1 - c, n_chunks - 1 - (c - ctx_chunks))


def _dot_nt(a, b):
    return lax.dot_general(a, b, (((1,), (1,)), ((), ())), preferred_element_type=F32)


def _dot_tn(a, b):
    return lax.dot_general(a, b, (((0,), (0,)), ((), ())), preferred_element_type=F32)


def _retention_kernel(lg_ref, q_ref, k_ref, v_ref, o_ref, s_ref, *, reverse, heads, dk, dv):
    c = pl.program_id(0)

    @pl.when(c == 0)
    def _():
        s_ref[...] = jnp.zeros_like(s_ref)

    n = q_ref.shape[0]
    ri = lax.broadcasted_iota(I32, (n, n), 0)
    ci = lax.broadcasted_iota(I32, (n, n), 1)
    diff = (ci - ri) if reverse else (ri - ci)
    dmask = diff >= 0
    dist = jnp.where(dmask, diff, 0).astype(F32)
    r1 = lax.broadcasted_iota(I32, (n, 1), 0)
    pos = ((n - 1 - r1) if reverse else r1).astype(F32)
    for h in range(heads):
        lg = jnp.full((1, 1), lg_ref[h], F32)
        decay = jnp.where(dmask, jnp.exp(dist * lg), 0.0)
        q_dec = jnp.exp((pos + 1.0) * lg)
        k_dec = jnp.exp((n - 1.0 - pos) * lg)
        c_dec = jnp.exp(float(n) * lg)
        q = q_ref[:, h * dk:(h + 1) * dk]
        k = k_ref[:, h * dk:(h + 1) * dk]
        v = v_ref[:, h * dv:(h + 1) * dv]
        s_prev = s_ref[h]
        scores = (_dot_nt(q, k) * decay).astype(BF16)
        inner = jnp.dot(scores, v, preferred_element_type=F32)
        cross = jnp.dot(q, s_prev.astype(BF16), preferred_element_type=F32)
        o_ref[:, h * dv:(h + 1) * dv] = (inner + q_dec * cross).astype(o_ref.dtype)
        kd = (k.astype(F32) * k_dec).astype(BF16)
        s_ref[h] = c_dec * s_prev + _dot_tn(kd, v)


def _retention_scan(log_gamma, q, k, vg, d_idx, n_ctx, heads, dk, dv, reverse):
    t = q.shape[1]
    nc, cc = t // CHUNK, n_ctx // CHUNK
    cm = lambda c: _chunk_index(c, nc, cc, reverse)
    return pl.pallas_call(
        functools.partial(_retention_kernel, reverse=reverse, heads=heads, dk=dk, dv=dv),
        out_shape=jax.ShapeDtypeStruct((t, heads * dv), BF16),
        grid=(nc,),
        in_specs=[pl.BlockSpec(memory_space=pltpu.SMEM),
                  pl.BlockSpec((None, CHUNK, heads * dk), lambda c: (d_idx, cm(c), 0)),
                  pl.BlockSpec((None, CHUNK, heads * dk), lambda c: (d_idx, cm(c), 0)),
                  pl.BlockSpec((CHUNK, heads * dv), lambda c: (cm(c), 0))],
        out_specs=pl.BlockSpec((CHUNK, heads * dv), lambda c: (cm(c), 0)),
        scratch_shapes=[pltpu.VMEM((heads, dk, dv), F32)],
        compiler_params=_cparams(("arbitrary",)),
        name="retention_scan_rev" if reverse else "retention_scan_fwd",
    )(log_gamma, q, k, vg)


def _ret_post_kernel(yf_ref, yb_ref, g_ref, w_ref, b_ref, o_ref, *, heads, dv):
    for h in range(heads):
        sl = slice(h * dv, (h + 1) * dv)
        y = yf_ref[:, sl].astype(F32) + yb_ref[:, sl].astype(F32)
        mu = jnp.mean(y, axis=-1, keepdims=True)
        yc = y - mu
        var = jnp.mean(yc * yc, axis=-1, keepdims=True)
        yn = yc * lax.rsqrt(var + NORM_EPS)
        g = g_ref[:, sl].astype(F32)
        silu = g * (1.0 / (1.0 + jnp.exp(-g)))
        o_ref[:, sl] = ((yn * w_ref[:, sl] + b_ref[:, sl]) * silu).astype(o_ref.dtype)


def _ret_post(yf, yb, vg, gn_w, gn_b, heads, dv):
    t, vw = yf.shape
    tm = ROW_TILE
    return pl.pallas_call(
        functools.partial(_ret_post_kernel, heads=heads, dv=dv),
        out_shape=jax.ShapeDtypeStruct((t, vw), BF16),
        grid=(t // tm,),
        in_specs=[pl.BlockSpec((tm, vw), lambda i: (i, 0)),
                  pl.BlockSpec((tm, vw), lambda i: (i, 0)),
                  pl.BlockSpec((tm, vw), lambda i: (i, 1)),
                  pl.BlockSpec((1, vw), lambda i: (0, 0)),
                  pl.BlockSpec((1, vw), lambda i: (0, 0))],
        out_specs=pl.BlockSpec((tm, vw), lambda i: (i, 0)),
        compiler_params=_cparams(("arbitrary",)),
        name="retention_post",
    )(yf, yb, vg, gn_w.reshape(1, vw), gn_b.reshape(1, vw))


def _conv_silu_kernel(x_ref, prev_ref, next_ref, w_ref, o_ref, *, n_ctx, t_total, q_cols, q_scale):
    i = pl.program_id(0)
    tm = x_ref.shape[0]
    x = x_ref[...]
    row = lax.broadcasted_iota(I32, (tm, 1), 0)
    grow = i * tm + row
    halo = prev_ref.shape[0]
    x_prev = jnp.where(row == 0, prev_ref[halo - 1:halo, :], pltpu.roll(x, 1, axis=0))
    x_prev = jnp.where((grow == 0) | (grow == n_ctx), 0.0, x_prev)
    x_next = jnp.where(row == tm - 1, next_ref[0:1, :], pltpu.roll(x, tm - 1, axis=0))
    x_next = jnp.where((grow == n_ctx - 1) | (grow == t_total - 1), 0.0, x_next)
    y = x_prev * w_ref[0:1, :] + x * w_ref[1:2, :] + x_next * w_ref[2:3, :]
    y = y * (1.0 / (1.0 + jnp.exp(-y)))
    o_ref[:, :q_cols] = (y[:, :q_cols] * q_scale).astype(o_ref.dtype)
    o_ref[:, q_cols:] = y[:, q_cols:].astype(o_ref.dtype)


def _conv_silu(x, conv_w, n_ctx, q_cols, q_scale):
    t, c = x.shape
    tm = ROW_TILE
    halo = 8
    r = tm // halo
    last = t // halo - 1
    w8 = jnp.zeros((8, c), F32).at[:3].set(conv_w)
    return pl.pallas_call(
        functools.partial(_conv_silu_kernel, n_ctx=n_ctx, t_total=t, q_cols=q_cols, q_scale=q_scale),
        out_shape=jax.ShapeDtypeStruct((t, c), BF16),
        grid=(t // tm,),
        in_specs=[pl.BlockSpec((tm, c), lambda i: (i, 0)),
                  pl.BlockSpec((halo, c), lambda i: (jnp.maximum(i * r - 1, 0), 0)),
                  pl.BlockSpec((halo, c), lambda i: (jnp.minimum((i + 1) * r, last), 0)),
                  pl.BlockSpec((8, c), lambda i: (0, 0))],
        out_specs=pl.BlockSpec((tm, c), lambda i: (i, 0)),
        compiler_params=_cparams(("arbitrary",)),
        name="mlstm_conv_silu",
    )(x, x, x, w8)


def _mlstm_kernel(qk_ref, v_ref, gt_ref, bias_ref, o_ref, c_ref, n_ref, m_ref, *, reverse, heads, dk, dv, d_idx):
    c = pl.program_id(0)

    @pl.when(c == 0)
    def _():
        c_ref[...] = jnp.zeros_like(c_ref)
        n_ref[...] = jnp.zeros_like(n_ref)
        m_ref[...] = jnp.zeros_like(m_ref)

    n = qk_ref.shape[0]
    gates = gt_ref[...] + bias_ref[...]
    i_all = GATE_SOFTCAP * jnp.tanh(gates * (1.0 / GATE_SOFTCAP))
    f_all = -(jnp.maximum(-gates, 0.0) + jnp.log1p(jnp.exp(-jnp.abs(gates))))
    ri = lax.broadcasted_iota(I32, (n, n), 0)
    ci = lax.broadcasted_iota(I32, (n, n), 1)
    mask = (ci >= ri) if reverse else (ci <= ri)
    tri = jnp.where(mask, 1.0, 0.0).astype(F32)
    b_all = jnp.dot(tri, f_all, preferred_element_type=F32, precision=lax.Precision.HIGHEST)
    b_all_t = b_all.T
    i_all_t = i_all.T
    last = 0 if reverse else n - 1
    qk_w = heads * dk
    for h in range(heads):
        ic = d_idx * 2 * heads + h
        fc = ic + heads
        b_col = b_all[:, fc:fc + 1]
        b_row = b_all_t[fc:fc + 1, :]
        i_col = i_all[:, ic:ic + 1]
        i_row = i_all_t[ic:ic + 1, :]
        m_prev = m_ref[h][:, 0:1]
        q = qk_ref[:, h * dk:(h + 1) * dk]
        k = qk_ref[:, qk_w + h * dk:qk_w + (h + 1) * dk]
        v = v_ref[:, h * dv:(h + 1) * dv]
        c_prev = c_ref[h]
        n_prev = n_ref[h]
        d_log = jnp.where(mask, b_col - b_row + i_row, NEG_INF)
        m_t = jnp.maximum(b_col + m_prev, jnp.max(d_log, axis=-1, keepdims=True))
        w = jnp.exp(d_log - m_t)
        s = _dot_nt(q, k) * w
        inter = jnp.exp(b_col + m_prev - m_t)
        num = jnp.dot(s.astype(BF16), v, preferred_element_type=F32) + inter * jnp.dot(
            q, c_prev.astype(BF16), preferred_element_type=F32)
        den = jnp.sum(s, axis=-1, keepdims=True) + inter * jnp.sum(q.astype(F32) * n_prev, axis=-1, keepdims=True)
        o_ref[:, h * dv:(h + 1) * dv] = (num / jnp.maximum(jnp.abs(den), jnp.exp(-m_t))).astype(o_ref.dtype)
        b_last = b_col[last:last + 1, :]
        g = b_last - b_col + i_col
        m_new = jnp.maximum(b_last + m_prev, jnp.max(g, axis=0, keepdims=True))
        wk = jnp.exp(g - m_new) * k.astype(F32)
        dec = jnp.exp(b_last + m_prev - m_new)
        c_ref[h] = dec * c_prev + _dot_tn(wk.astype(BF16), v)
        n_ref[h] = dec * n_prev + jnp.sum(wk, axis=0, keepdims=True)
        m_ref[h] = jnp.broadcast_to(m_new, m_ref.shape[1:])


def _mlstm_scan(qk, vo, gates, bias, n_ctx, heads, dk, dv, reverse):
    t = qk.shape[0]
    nc, cc = t // CHUNK, n_ctx // CHUNK
    cm = lambda c: _chunk_index(c, nc, cc, reverse)
    return pl.pallas_call(
        functools.partial(_mlstm_kernel, reverse=reverse, heads=heads, dk=dk, dv=dv, d_idx=1 if reverse else 0),
        out_shape=jax.ShapeDtypeStruct((t, heads * dv), BF16),
        grid=(nc,),
        in_specs=[pl.BlockSpec((CHUNK, 2 * heads * dk), lambda c: (cm(c), 0)),
                  pl.BlockSpec((CHUNK, heads * dv), lambda c: (cm(c), 0)),
                  pl.BlockSpec((CHUNK, LANES), lambda c: (cm(c), 0)),
                  pl.BlockSpec((1, LANES), lambda c: (0, 0))],
        out_specs=pl.BlockSpec((CHUNK, heads * dv), lambda c: (cm(c), 0)),
        scratch_shapes=[pltpu.VMEM((heads, dk, dv), F32),
                        pltpu.VMEM((heads, 1, dk), F32),
                        pltpu.VMEM((heads, 1, LANES), F32)],
        compiler_params=_cparams(("arbitrary",)),
        name="mlstm_scan_rev" if reverse else "mlstm_scan_fwd",
    )(qk, vo, gates, bias)


def _mlstm_post_kernel(yf_ref, yb_ref, o_ref_in, w_ref, o_ref, *, heads, dv):
    for h in range(heads):
        sl = slice(h * dv, (h + 1) * dv)
        y = _rms(yf_ref[:, sl].astype(F32) + yb_ref[:, sl].astype(F32))
        o = o_ref_in[:, sl].astype(F32)
        o_ref[:, sl] = (y * w_ref[:, sl] * (1.0 / (1.0 + jnp.exp(-o)))).astype(o_ref.dtype)


def _mlstm_post(yf, yb, vo, norm_w, heads, dv):
    t, vw = yf.shape
    tm = ROW_TILE
    return pl.pallas_call(
        functools.partial(_mlstm_post_kernel, heads=heads, dv=dv),
        out_shape=jax.ShapeDtypeStruct((t, vw), BF16),
        grid=(t // tm,),
        in_specs=[pl.BlockSpec((tm, vw), lambda i: (i, 0)),
                  pl.BlockSpec((tm, vw), lambda i: (i, 0)),
                  pl.BlockSpec((tm, vw), lambda i: (i, 1)),
                  pl.BlockSpec((1, vw), lambda i: (0, 0))],
        out_specs=pl.BlockSpec((tm, vw), lambda i: (i, 0)),
        compiler_params=_cparams(("arbitrary",)),
        name="mlstm_post",
    )(yf, yb, vo, norm_w.reshape(1, vw))


def _qk_norm_rope_kernel(x_ref, cos_ref, sin_ref, qn_ref, kn_ref, q_ref, k_ref, *, q_heads, k_heads, hd):
    cos, sin = cos_ref[...], sin_ref[...]
    lane = lax.broadcasted_iota(I32, (1, hd), 1)
    first = (lane % (hd // 2)) < (hd // 4)
    for h in range(q_heads + k_heads):
        x = x_ref[:, h * hd:(h + 1) * hd]
        w = qn_ref[...] if h < q_heads else kn_ref[...]
        xh = _rms(x) * w
        partner = jnp.where(first, pltpu.roll(xh, hd - hd // 4, axis=1), pltpu.roll(xh, hd // 4, axis=1))
        y = (xh * cos + partner * sin).astype(q_ref.dtype)
        if h < q_heads:
            q_ref[:, h * hd:(h + 1) * hd] = y
        else:
            k_ref[:, (h - q_heads) * hd:(h - q_heads + 1) * hd] = y


def _qk_norm_rope(qk, cos, sin, q_norm, k_norm, q_heads, k_heads, hd):
    t = qk.shape[0]
    tm = ROW_TILE
    return pl.pallas_call(
        functools.partial(_qk_norm_rope_kernel, q_heads=q_heads, k_heads=k_heads, hd=hd),
        out_shape=(jax.ShapeDtypeStruct((t, q_heads * hd), BF16), jax.ShapeDtypeStruct((t, k_heads * hd), BF16)),
        grid=(t // tm,),
        in_specs=[pl.BlockSpec((tm, (q_heads + k_heads) * hd), lambda i: (i, 0)),
                  pl.BlockSpec((tm, hd), lambda i: (i, 0)),
                  pl.BlockSpec((tm, hd), lambda i: (i, 0)),
                  pl.BlockSpec((1, hd), lambda i: (0, 0)),
                  pl.BlockSpec((1, hd), lambda i: (0, 0))],
        out_specs=(pl.BlockSpec((tm, q_heads * hd), lambda i: (i, 0)),
                   pl.BlockSpec((tm, k_heads * hd), lambda i: (i, 0))),
        compiler_params=_cparams(("arbitrary",)),
        name="attn_qk_norm_rope",
    )(qk, cos, sin, q_norm.reshape(1, hd), k_norm.reshape(1, hd))


def _flash_kernel(q_ref, k_ref, v_ref, o_ref, m_ref, l_ref, acc_ref, alpha_ref, sa_ref, sb_ref, p_ref,
                  *, groups, hd, tk, n_ctx):
    tq = q_ref.shape[0]
    n_kv = k_ref.shape[0]
    n = n_kv // tk
    q = jnp.concatenate([q_ref[:, g * hd:(g + 1) * hd] for g in range(groups)], axis=0)
    m_ref[...] = jnp.full_like(m_ref, NEG_INF)
    l_ref[...] = jnp.zeros_like(l_ref)
    acc_ref[...] = jnp.zeros_like(acc_ref)

    def scores(off, size):
        return _dot_nt(q, k_ref[pl.ds(off, size), :])

    def absorb(s_ref, off, size):
        for band in range(groups * tq // FLASH_BAND):
            rows = slice(band * FLASH_BAND, (band + 1) * FLASH_BAND)
            s = s_ref[rows, :size]
            m_prev = m_ref[rows, :]
            m_new = jnp.maximum(m_prev, jnp.max(s, axis=-1, keepdims=True))
            alpha = jnp.exp2(m_prev - m_new)
            p = jnp.exp2(s - jnp.concatenate([m_new] * (size // LANES), axis=1))
            l_ref[rows, :] = alpha * l_ref[rows, :] + jnp.sum(p, axis=-1, keepdims=True)
            m_ref[rows, :] = m_new
            alpha_ref[rows, :] = alpha
            p_ref[rows, :size] = p.astype(BF16)
        acc_ref[...] = alpha_ref[...] * acc_ref[...] + jnp.dot(p_ref[:, :size], v_ref[pl.ds(off, size), :],
                                                                preferred_element_type=F32)

    is_ctx = pl.program_id(1) < n_ctx // tq

    @pl.when(is_ctx)
    def _():
        sa_ref[:, :n_ctx] = scores(0, n_ctx)
        absorb(sa_ref, 0, n_ctx)

    @pl.when(jnp.logical_not(is_ctx))
    def _():
        bufs = (sa_ref, sb_ref)
        bufs[0][...] = scores(0, tk)
        for j in range(n):
            if j + 1 < n:
                bufs[(j + 1) % 2][...] = scores((j + 1) * tk, tk)
            absorb(bufs[j % 2], j * tk, tk)

    out = acc_ref[...] / l_ref[...]
    for g in range(groups):
        o_ref[:, g * hd:(g + 1) * hd] = out[g * tq:(g + 1) * tq].astype(o_ref.dtype)


def _flash_attention(q, k, v, n_ctx, kv_heads, groups, hd):
    t = q.shape[0]
    tq = _pick_tile(math.gcd(t, n_ctx), (256, 128))
    tk = _pick_tile(t, (768, 512, 256, 128))
    assert hd == LANES and n_ctx <= tk and (groups * tq) % FLASH_BAND == 0
    gw = groups * hd
    return pl.pallas_call(
        functools.partial(_flash_kernel, groups=groups, hd=hd, tk=tk, n_ctx=n_ctx),
        out_shape=jax.ShapeDtypeStruct((t, kv_heads * gw), BF16),
        grid=(kv_heads, t // tq),
        in_specs=[pl.BlockSpec((tq, gw), lambda g, i: (i, g)),
                  pl.BlockSpec((t, hd), lambda g, i: (0, g)),
                  pl.BlockSpec((t, hd), lambda g, i: (0, g))],
        out_specs=pl.BlockSpec((tq, gw), lambda g, i: (i, g)),
        scratch_shapes=[pltpu.VMEM((groups * tq, LANES), F32),
                        pltpu.VMEM((groups * tq, LANES), F32),
                        pltpu.VMEM((groups * tq, hd), F32),
                        pltpu.VMEM((groups * tq, LANES), F32),
                        pltpu.VMEM((groups * tq, tk), F32),
                        pltpu.VMEM((groups * tq, tk), F32),
                        pltpu.VMEM((groups * tq, tk), BF16)],
        compiler_params=_cparams(("arbitrary", "arbitrary")),
        name="flash_attention",
    )(q, k, v)


def _router_kernel(x_ref, g_ref, sh_ref, sc_ref, wr_ref, br_ref, h_ref, ids_ref, wts_ref, cnt_ref, run_ref,
                   *, n_ctx, row0):
    i = pl.program_id(0)
    tm, d = x_ref.shape

    @pl.when(i == 0)
    def _():
        run_ref[...] = jnp.zeros_like(run_ref)

    ib = i + row0 // tm
    h = _rms(x_ref[...]) * g_ref[...]
    h = h * (1.0 + _mod_row(sc_ref, ib, tm, n_ctx)) + _mod_row(sh_ref, ib, tm, n_ctx)
    h_ref[...] = h
    h_hi = h.astype(BF16)
    h_lo = (h - h_hi.astype(F32)).astype(BF16)
    hw = jnp.dot(h_hi, wr_ref[...], preferred_element_type=F32)
    lw = jnp.dot(h_lo, wr_ref[:, :LANES], preferred_element_type=F32)
    logits = hw[:, :LANES] + (hw[:, LANES:] + lw) + br_ref[...]
    lane = lax.broadcasted_iota(I32, (tm, LANES), 1)
    big = jnp.int32(LANES)
    gl = jnp.where((lane >= MOE_EXPERTS) & (lane < MOE_EXPERTS + MOE_GROUPS), logits, NEG_INF)
    g_max = jnp.max(gl, axis=-1, keepdims=True)
    g_idx = jnp.min(jnp.where(gl == g_max, lane, big), axis=-1, keepdims=True) - MOE_EXPERTS
    g_w = 1.0 / jnp.sum(jnp.exp(gl - g_max), axis=-1, keepdims=True)
    el = jnp.where((lane < MOE_EXPERTS) & ((lane >> 3) == g_idx), logits, NEG_INF)
    m1 = jnp.max(el, axis=-1, keepdims=True)
    i1 = jnp.min(jnp.where(el == m1, lane, big), axis=-1, keepdims=True)
    el2 = jnp.where(lane == i1, NEG_INF, el)
    m2 = jnp.max(el2, axis=-1, keepdims=True)
    i2 = jnp.min(jnp.where(el2 == m2, lane, big), axis=-1, keepdims=True)
    e_sum = jnp.sum(jnp.exp(el - m1), axis=-1, keepdims=True)
    p1 = 1.0 / e_sum
    p2 = jnp.exp(m2 - m1) / e_sum
    w1 = g_w * (p1 / (p1 + p2))
    w2 = g_w * (p2 / (p1 + p2))
    onehot = jnp.where((lane == i1) | (lane == i2), 1.0, 0.0)
    ri = lax.broadcasted_iota(I32, (tm, tm), 0)
    ci = lax.broadcasted_iota(I32, (tm, tm), 1)
    strict = jnp.where(ci < ri, 1.0, 0.0).astype(BF16)
    before = jnp.dot(strict, onehot.astype(BF16), preferred_element_type=F32) + run_ref[0:1, :]
    r1 = jnp.sum(jnp.where(lane == i1, before, 0.0), axis=-1, keepdims=True).astype(I32)
    r2 = jnp.sum(jnp.where(lane == i2, before, 0.0), axis=-1, keepdims=True).astype(I32)
    ids_ref[...] = jnp.where(lane == 0, i1, jnp.where(lane == 1, i2, jnp.where(lane == 2, r1, jnp.where(lane == 3, r2, 0))))
    wts_ref[...] = jnp.where(lane == 0, w1, jnp.where(lane == 1, w2, 0.0))
    total = run_ref[0:1, :] + jnp.sum(onehot, axis=0, keepdims=True)
    run_ref[...] = jnp.broadcast_to(total, run_ref.shape)
    cnt_ref[...] = jnp.broadcast_to(total, cnt_ref.shape)


def _router(x, g, mod, k_shift, k_scale, w_router, b_router, n_ctx, row0):
    t, d = x.shape
    tm = ROW_TILE
    n = t - row0
    off = row0 // tm
    return pl.pallas_call(
        functools.partial(_router_kernel, n_ctx=n_ctx, row0=row0),
        out_shape=(jax.ShapeDtypeStruct((n, d), F32),
                   jax.ShapeDtypeStruct((n, LANES), I32),
                   jax.ShapeDtypeStruct((n, LANES), F32),
                   jax.ShapeDtypeStruct((8, LANES), F32)),
        grid=(n // tm,),
        in_specs=[pl.BlockSpec((tm, d), lambda i: (i + off, 0)),
                  pl.BlockSpec((1, d), lambda i: (0, 0)),
                  pl.BlockSpec((8, d), lambda i: (0, k_shift)),
                  pl.BlockSpec((8, d), lambda i: (0, k_scale)),
                  pl.BlockSpec((d, 2 * LANES), lambda i: (0, 0)),
                  pl.BlockSpec((1, LANES), lambda i: (0, 0))],
        out_specs=(pl.BlockSpec((tm, d), lambda i: (i, 0)),
                   pl.BlockSpec((tm, LANES), lambda i: (i, 0)),
                   pl.BlockSpec((tm, LANES), lambda i: (i, 0)),
                   pl.BlockSpec((8, LANES), lambda i: (0, 0))),
        scratch_shapes=[pltpu.VMEM((8, LANES), F32)],
        compiler_params=_cparams(("arbitrary",)),
        name="moe_router",
    )(x, g.reshape(1, d), mod, mod, w_router, b_router)


def _slot_index_kernel(ids_ref, seg_ref, o_ref):
    ids = ids_ref[...]
    lane = lax.broadcasted_iota(I32, ids.shape, 1)
    seg = seg_ref[...]
    slots = []
    for k in range(MOE_TOPK):
        start = jnp.sum(jnp.where(lane == ids[:, k:k + 1], seg, 0.0), axis=-1, keepdims=True)
        slots.append(start.astype(I32) + ids[:, MOE_TOPK + k:MOE_TOPK + k + 1])
    o_ref[...] = jnp.where(lane == 0, slots[0], jnp.where(lane == 1, slots[1], 0))


def _slot_index(ids, seg_start):
    n = ids.shape[0]
    tm = ROW_TILE
    seg = jnp.zeros((1, LANES), F32).at[0, :MOE_EXPERTS].set(seg_start.astype(F32))
    out = pl.pallas_call(
        _slot_index_kernel,
        out_shape=jax.ShapeDtypeStruct((n, LANES), I32),
        grid=(n // tm,),
        in_specs=[pl.BlockSpec((tm, LANES), lambda i: (i, 0)),
                  pl.BlockSpec((1, LANES), lambda i: (0, 0))],
        out_specs=pl.BlockSpec((tm, LANES), lambda i: (i, 0)),
        compiler_params=_cparams(("arbitrary",)),
        name="moe_slot_index",
    )(ids, seg)
    return out[:, :MOE_TOPK].reshape(-1)


def _dispatch_kernel(dest_ref, h_ref, xs_in_ref, xs_ref, sem):
    del xs_in_ref
    i = pl.program_id(0)
    tm = h_ref.shape[0]

    def row_copy(r, k):
        slot = dest_ref[(i * tm + r) * MOE_TOPK + k]
        return pltpu.make_async_copy(h_ref.at[pl.ds(r, 1)], xs_ref.at[pl.ds(slot, 1)], sem)

    def start(r, carry):
        for k in range(MOE_TOPK):
            row_copy(r, k).start()
        return carry

    def wait(r, carry):
        for k in range(MOE_TOPK):
            row_copy(r, k).wait()
        return carry

    lax.fori_loop(0, tm, start, 0, unroll=8)
    lax.fori_loop(0, tm, wait, 0, unroll=8)


def _dispatch(dest, h, xs_init):
    n, w = h.shape
    tm = ROW_TILE
    return pl.pallas_call(
        _dispatch_kernel,
        out_shape=jax.ShapeDtypeStruct(xs_init.shape, xs_init.dtype),
        grid_spec=pltpu.PrefetchScalarGridSpec(
            num_scalar_prefetch=1,
            grid=(n // tm,),
            in_specs=[pl.BlockSpec((tm, w), lambda i, dest: (i, 0)),
                      pl.BlockSpec(memory_space=pl.ANY)],
            out_specs=pl.BlockSpec(memory_space=pl.ANY),
            scratch_shapes=[pltpu.SemaphoreType.DMA(())]),
        input_output_aliases={2: 0},
        compiler_params=_cparams(("arbitrary",)),
        name="moe_dispatch",
    )(dest, h, xs_init)


def _expert_kernel(be_ref, first_ref, next_ref, nb_ref, xs_ref, wg_hbm, wu_hbm, wd_hbm, ys_ref,
                   wg_st, wu_st, wd_st, wg_bf, wu_bf, wd_bf, sem, *, layer):
    b = pl.program_id(0)
    used = b < nb_ref[0]

    weights = ((wg_hbm, wg_st, wg_bf), (wu_hbm, wu_st, wu_bf), (wd_hbm, wd_st, wd_bf))

    def fetch(e, idx):
        hbm, st, _ = weights[idx]
        return pltpu.make_async_copy(hbm.at[layer, e], st, sem.at[idx])

    @pl.when(b == 0)
    def _():
        for idx in range(len(weights)):
            fetch(be_ref[0], idx).start()

    @pl.when(used & (first_ref[b] == 1))
    def _():
        has_next = next_ref[b] >= 0
        for idx, (_, st, bf) in enumerate(weights):
            fetch(be_ref[b], idx).wait()
            bf[...] = st[...].astype(BF16)

            @pl.when(has_next)
            def _():
                fetch(next_ref[b], idx).start()

    @pl.when(used)
    def _():
        x = xs_ref[...].astype(BF16)
        gate = jnp.dot(x, wg_bf[...], preferred_element_type=F32)
        up = jnp.dot(x, wu_bf[...], preferred_element_type=F32)
        act = (gate * (1.0 / (1.0 + jnp.exp(-gate))) * up).astype(BF16)
        ys_ref[...] = jnp.dot(act, wd_bf[...], preferred_element_type=F32)

    @pl.when(jnp.logical_not(used))
    def _():
        ys_ref[...] = jnp.zeros_like(ys_ref)


def _experts(block_expert, first, next_expert, n_used, xs, w_gate, w_up, w_down, layer):
    n_slots, d = xs.shape
    ff = w_gate.shape[3]
    nb = n_slots // MOE_BLOCK
    row_spec = pl.BlockSpec((MOE_BLOCK, d), lambda b, *_: (b, 0))
    hbm = pl.BlockSpec(memory_space=pl.ANY)
    return pl.pallas_call(
        functools.partial(_expert_kernel, layer=layer),
        out_shape=jax.ShapeDtypeStruct((n_slots, d), F32),
        grid_spec=pltpu.PrefetchScalarGridSpec(
            num_scalar_prefetch=4,
            grid=(nb,),
            in_specs=[row_spec, hbm, hbm, hbm],
            out_specs=row_spec,
            scratch_shapes=[pltpu.VMEM((d, ff), F32), pltpu.VMEM((d, ff), F32), pltpu.VMEM((ff, d), F32),
                            pltpu.VMEM((d, ff), BF16), pltpu.VMEM((d, ff), BF16), pltpu.VMEM((ff, d), BF16),
                            pltpu.SemaphoreType.DMA((3,))]),
        compiler_params=_cparams(("arbitrary",)),
        name="moe_experts",
    )(block_expert, first, next_expert, n_used, xs, w_gate, w_up, w_down)


def _combine_kernel(dest_ref, x_ref, wts_ref, mod_ref, ys_ref, g_ref, sh_ref, sc_ref, *rest, n_ctx, row0, last):
    if last:
        o_ref, buf, sem = rest
    else:
        o_ref, h_ref, buf, sem = rest
    i = pl.program_id(0)
    tm = x_ref.shape[0]

    def row_copy(r, k):
        slot = dest_ref[(i * tm + r) * MOE_TOPK + k]
        return pltpu.make_async_copy(ys_ref.at[pl.ds(slot, 1)], buf.at[k, pl.ds(r, 1)], sem)

    def start(r, carry):
        for k in range(MOE_TOPK):
            row_copy(r, k).start()
        return carry

    def wait(r, carry):
        for k in range(MOE_TOPK):
            row_copy(r, k).wait()
        return carry

    lax.fori_loop(0, tm, start, 0, unroll=8)
    lax.fori_loop(0, tm, wait, 0, unroll=8)
    w = wts_ref[...]
    y = w[:, 0:1] * buf[0] + w[:, 1:2] * buf[1]
    ib = i + row0 // tm
    x_new = x_ref[...] + _mod_row(mod_ref, ib, tm, n_ctx) * y
    hn = _rms(x_new) * g_ref[...]
    if last:
        o_ref[...] = hn
    else:
        o_ref[...] = x_new
        hn = hn * (1.0 + _mod_row(sc_ref, ib, tm, n_ctx)) + _mod_row(sh_ref, ib, tm, n_ctx)
        h_ref[...] = hn.astype(h_ref.dtype)


def _combine(dest, x, wts, mod, k_gate, ys, n_ctx, row0, g_next, mod_next, last):
    t, d = x.shape
    tm = ROW_TILE
    n = t - row0
    off = row0 // tm
    row_spec = pl.BlockSpec((tm, d), lambda i, dest: (i, 0))
    out_shape = jax.ShapeDtypeStruct((n, d), F32)
    out_specs = row_spec
    if not last:
        out_shape = (out_shape, jax.ShapeDtypeStruct((n, d), BF16))
        out_specs = (row_spec, row_spec)
    return pl.pallas_call(
        functools.partial(_combine_kernel, n_ctx=n_ctx, row0=row0, last=last),
        out_shape=out_shape,
        grid_spec=pltpu.PrefetchScalarGridSpec(
            num_scalar_prefetch=1,
            grid=(n // tm,),
            in_specs=[pl.BlockSpec((tm, d), lambda i, dest: (i + off, 0)),
                      pl.BlockSpec((tm, LANES), lambda i, dest: (i, 0)),
                      pl.BlockSpec((8, d), lambda i, dest: (0, k_gate)),
                      pl.BlockSpec(memory_space=pl.ANY),
                      pl.BlockSpec((1, d), lambda i, dest: (0, 0)),
                      pl.BlockSpec((8, d), lambda i, dest: (0, 0)),
                      pl.BlockSpec((8, d), lambda i, dest: (0, 1))],
            out_specs=out_specs,
            scratch_shapes=[pltpu.VMEM((MOE_TOPK, tm, d), F32), pltpu.SemaphoreType.DMA(())]),
        compiler_params=_cparams(("arbitrary",)),
        name="moe_combine",
    )(dest, x, wts, mod, ys, g_next.reshape(1, d), mod_next, mod_next)


def _hier_moe(x, g, mod, w_rg, b_rg, w_re, b_re, w_gate, w_up, w_down, layer, n_ctx, row0, xs_buf,
              g_next, mod_next, last):
    t, d = x.shape
    n = t - row0
    pad = LANES - MOE_EXPERTS - MOE_GROUPS
    w_router = jnp.concatenate([w_re, w_rg, jnp.zeros((d, pad), F32)], axis=1)
    w_hi = w_router.astype(BF16)
    w_router = jnp.concatenate([w_hi, (w_router - w_hi.astype(F32)).astype(BF16)], axis=1)
    b_router = jnp.concatenate([b_re, b_rg, jnp.zeros((pad,), F32)]).reshape(1, LANES)
    h, ids, wts, counts = _router(x, g, mod, 3, 4, w_router, b_router, n_ctx, row0)
    counts = counts[0, :MOE_EXPERTS].astype(I32)
    padded = (counts + MOE_BLOCK - 1) // MOE_BLOCK * MOE_BLOCK
    seg_end = jnp.cumsum(padded)
    seg_start = seg_end - padded
    dest = _slot_index(ids, seg_start)
    n_blocks = (t * MOE_TOPK + MOE_EXPERTS * (MOE_BLOCK - 1)) // MOE_BLOCK + 1
    block_start = jnp.arange(n_blocks, dtype=I32) * MOE_BLOCK
    block_expert = jnp.minimum(jnp.sum((seg_end[None, :] <= block_start[:, None]).astype(I32), axis=1),
                               MOE_EXPERTS - 1).astype(I32)
    n_used = (seg_end[-1:] // MOE_BLOCK).astype(I32)
    block_id = jnp.arange(n_blocks, dtype=I32)
    prev_expert = jnp.concatenate([jnp.full((1,), -1, I32), block_expert[:-1]])
    first = ((block_id < n_used[0]) & (block_expert != prev_expert)).astype(I32)
    eid = jnp.arange(MOE_EXPERTS, dtype=I32)
    owner = jnp.where(padded > 0, eid, MOE_EXPERTS)
    later = jnp.concatenate([lax.cummin(owner[::-1])[::-1][1:], jnp.full((1,), MOE_EXPERTS, I32)])
    next_expert = jnp.where(later < MOE_EXPERTS, later, -1)[block_expert].astype(I32)
    if xs_buf is None:
        xs_buf = jnp.zeros((n_blocks * MOE_BLOCK, d), F32)
    xs = _dispatch(dest, h, xs_buf)
    ys = _experts(block_expert, first, next_expert, n_used, xs, w_gate, w_up, w_down, layer)
    return _combine(dest, x, wts, mod, 5, ys, n_ctx, row0, g_next, mod_next, last), xs


def _rope_table(pos, dim, base):
    inv = jnp.power(base, -jnp.arange(0, dim, 2, dtype=F32) / dim)
    ang = pos.astype(F32)[:, None] * inv[None, :]
    return jnp.cos(ang), jnp.sin(ang)


def _flipped_positions(t, n_ctx):
    idx = jnp.arange(t, dtype=I32)
    return jnp.where(idx < n_ctx, n_ctx - 1 - idx, n_ctx + (t - 1 - idx))


def _axial_tables(n_ctx, n_lat, hd):
    rows = n_lat // GRID_W
    row = jnp.repeat(jnp.arange(rows, dtype=I32), GRID_W)
    col = jnp.tile(jnp.arange(GRID_W, dtype=I32), rows)
    half = hd // 2
    cr, sr = _rope_table(row, half, ROPE_BASE)
    cc, sc = _rope_table(col, half, ROPE_BASE)
    cos = jnp.concatenate([cr, cr, cc, cc], axis=1)
    sin = jnp.concatenate([-sr, sr, -sc, sc], axis=1)
    cos = jnp.concatenate([jnp.ones((n_ctx, hd), F32), cos], axis=0)
    sin = jnp.concatenate([jnp.zeros((n_ctx, hd), F32), sin], axis=0)
    return cos, sin


def _retention_layer(xa, h, mod, w_in, logit_gamma, gn_w, gn_b, w_out, j, n_ctx):
    t, d = xa.shape
    heads = RET_HEADS
    dk = d // heads
    dv = 2 * dk
    qk_w, v_w = heads * dk, heads * dv
    pos_f = jnp.arange(t, dtype=I32)
    cf, sf = _rope_table(pos_f, dk, RET_ROPE_BASE)
    cb, sb = _rope_table(_flipped_positions(t, n_ctx), dk, RET_ROPE_BASE)
    tables = (cf, sf, cb, sb)
    q = _matmul_rope(h, w_in, j, 0, qk_w, tables, dk ** -0.5, dk)
    k = _matmul_rope(h, w_in, j, qk_w, qk_w, tables, 1.0, dk)
    vg = _matmul(h, w_in, j, 2 * qk_w, 2 * v_w, BF16)
    log_gamma = jax.nn.log_sigmoid(logit_gamma.astype(F32))
    yf = _retention_scan(log_gamma[0], q, k, vg, 0, n_ctx, heads, dk, dv, reverse=False)
    yb = _retention_scan(log_gamma[1], q, k, vg, 1, n_ctx, heads, dk, dv, reverse=True)
    yn = _ret_post(yf, yb, vg, gn_w, gn_b, heads, dv)
    return _matmul_residual(yn, w_out, j, xa, mod, 2, n_ctx)


def _mlstm_layer(xa, h, mod, w_in, conv_w, b_gate, norm_w, w_out, j, n_ctx):
    t, d = xa.shape
    heads = MLSTM_HEADS
    dk = d // (2 * heads)
    dv = d // heads
    qk_w, v_w = heads * dk, heads * dv
    n_gates = 4 * heads
    qk_pre = _matmul(h, w_in, j, 0, 2 * qk_w, F32)
    vo = _matmul(h, w_in, j, 2 * qk_w, 2 * v_w, BF16)
    w_gates = jnp.pad(w_in[j, :, 2 * qk_w + 2 * v_w:], ((0, 0), (0, LANES - n_gates)))
    gates = _matmul(h, w_gates[None], 0, 0, LANES, F32)
    bias = jnp.pad(b_gate.astype(F32).reshape(1, n_gates), ((0, 0), (0, LANES - n_gates)))
    qk = _conv_silu(qk_pre, conv_w, n_ctx, qk_w, dk ** -0.5)
    yf = _mlstm_scan(qk, vo, gates, bias, n_ctx, heads, dk, dv, reverse=False)
    yb = _mlstm_scan(qk, vo, gates, bias, n_ctx, heads, dk, dv, reverse=True)
    yn = _mlstm_post(yf, yb, vo, norm_w, heads, dv)
    return _matmul_residual(yn, w_out, j, xa, mod, 2, n_ctx)


def _attention_layer(xa, h, mod, w_in, q_norm, k_norm, w_out, j, n_ctx):
    t, d = xa.shape
    hd = d // ATTN_HEADS
    groups = ATTN_HEADS // ATTN_KV_HEADS
    q_w, kv_w = ATTN_HEADS * hd, ATTN_KV_HEADS * hd
    n_lat = t - n_ctx
    qk_pre = _matmul(h, w_in, j, 0, q_w + kv_w, F32)
    v = _matmul(h, w_in, j, q_w + kv_w, kv_w, BF16)
    cos, sin = _axial_tables(n_ctx, n_lat, hd)
    q_gain = q_norm * (hd ** -0.5 * math.log2(math.e))
    q, k = _qk_norm_rope(qk_pre, cos, sin, q_gain, k_norm, ATTN_HEADS, ATTN_KV_HEADS, hd)
    o = _flash_attention(q, k, v, n_ctx, ATTN_KV_HEADS, groups, hd)
    return _matmul_residual(o, w_out, j, xa, mod, 2, n_ctx)


def kernel(x, c, ctx, c_ctx, w_ada, b_ada, norm_g, ret_w_in, ret_logit_gamma, ret_gn_w, ret_gn_b, ret_w_out, mlstm_w_in, mlstm_conv_w, mlstm_b_gate, mlstm_norm_w, mlstm_w_out, attn_w_in, attn_q_norm, attn_k_norm, attn_w_out, moe_w_router_group, moe_b_router_group, moe_w_router_expert, moe_b_router_expert, moe_w_gate, moe_w_up, moe_w_down, final_norm_g):
    bsz, n_lat, d = x.shape
    n_ctx = ctx.shape[1]
    depth = w_ada.shape[0]
    assert bsz == 1 and n_ctx % ROW_TILE == 0 and n_lat % ROW_TILE == 0 and n_lat % GRID_W == 0
    xa = jnp.concatenate([ctx[0], x[0]], axis=0)
    s = jnp.stack([jax.nn.silu(c[0]), jax.nn.silu(c_ctx)])
    s8 = jnp.zeros((8, d), F32).at[:2].set(s).astype(BF16)
    mods = _ada_modulation(s8, w_ada, b_ada)
    xs_buf = None
    h = _norm_mod(xa, norm_g[0, 0], mods[0], 0, 1, n_ctx)
    for i in range(depth):
        kind, j = i % 3, i // 3
        mod = mods[i]
        last = i == depth - 1
        if kind == 0:
            xa = _retention_layer(xa, h, mod, ret_w_in, ret_logit_gamma[j], ret_gn_w[j], ret_gn_b[j],
                                  ret_w_out, j, n_ctx)
        elif kind == 1:
            xa = _mlstm_layer(xa, h, mod, mlstm_w_in, mlstm_conv_w[j], mlstm_b_gate[j],
                              mlstm_norm_w[j], mlstm_w_out, j, n_ctx)
        else:
            xa = _attention_layer(xa, h, mod, attn_w_in, attn_q_norm[j], attn_k_norm[j],
                                  attn_w_out, j, n_ctx)
        row0 = n_ctx if last else 0
        g_next = final_norm_g if last else norm_g[i + 1, 0]
        mod_next = mod if last else mods[i + 1]
        out, xs_buf = _hier_moe(xa, norm_g[i, 1], mod, moe_w_router_group[i], moe_b_router_group[i],
                                moe_w_router_expert[i], moe_b_router_expert[i], moe_w_gate, moe_w_up, moe_w_down, i,
                                n_ctx, row0, xs_buf, g_next, mod_next, last)
        if last:
            return out[None]
        xa, h = out
```

```python
import functools
import math

import jax
import jax.numpy as jnp
import numpy as np
from jax import lax
from jax.experimental import pallas as pl
from jax.experimental.pallas import tpu as pltpu

F32 = jnp.float32
BF16 = jnp.bfloat16
I32 = jnp.int32
U32 = jnp.uint32

LANES = 128
CHUNK = 128
NORM_EPS = 1e-6
GRID_W = 64
RET_HEADS = 8
RET_ROPE_BASE = 10000.0
MLSTM_HEADS = 8
GATE_SOFTCAP = 15.0
ATTN_HEADS = 16
ATTN_KV_HEADS = 8
ROPE_BASE = 10000.0
MOE_GROUPS = 4
MOE_PER_GROUP = 8
MOE_EXPERTS = MOE_GROUPS * MOE_PER_GROUP
MOE_TOPK = 2
MOE_BLOCK = 256
ROW_TILE = 256
FLASH_BAND = 32
VMEM_LIMIT = 56 * 1024 * 1024
NEG_INF = float("-inf")


def _cparams(sem, vmem=VMEM_LIMIT):
    return pltpu.CompilerParams(dimension_semantics=sem, vmem_limit_bytes=vmem)


def _pick_tile(total, candidates):
    for c in candidates:
        if total % c == 0:
            return c
    raise ValueError(f"no tile for {total} in {candidates}")


def _mod_row(mod_ref, i, tile, n_ctx):
    sel = jnp.where(i < n_ctx // tile, 1, 0)
    return mod_ref[pl.ds(sel, 1), :]


def _ada_kernel(s_ref, w_ref, b_ref, o_ref):
    w = w_ref[0].astype(BF16)
    o_ref[0] = jnp.dot(s_ref[...], w, preferred_element_type=F32) + b_ref[0]


def _ada_modulation(s8, w_ada, b_ada):
    depth, d, n = w_ada.shape
    tn = _pick_tile(n, (1024, 512, 256, 128))
    return pl.pallas_call(
        _ada_kernel,
        out_shape=jax.ShapeDtypeStruct((depth, 8, n), F32),
        grid=(depth, n // tn),
        in_specs=[pl.BlockSpec((8, d), lambda l, j: (0, 0)),
                  pl.BlockSpec((1, d, tn), lambda l, j: (l, 0, j)),
                  pl.BlockSpec((1, 1, tn), lambda l, j: (l, 0, j))],
        out_specs=pl.BlockSpec((1, 8, tn), lambda l, j: (l, 0, j)),
        compiler_params=_cparams(("arbitrary", "arbitrary")),
        name="ada_modulation",
    )(s8, w_ada, b_ada.reshape(depth, 1, n))


def _rms(x):
    return x * lax.rsqrt(jnp.mean(x * x, axis=-1, keepdims=True) + NORM_EPS)


def _norm_mod_kernel(x_ref, g_ref, sh_ref, sc_ref, o_ref, *, n_ctx):
    i = pl.program_id(0)
    tm = x_ref.shape[0]
    y = _rms(x_ref[...]) * g_ref[...]
    y = y * (1.0 + _mod_row(sc_ref, i, tm, n_ctx)) + _mod_row(sh_ref, i, tm, n_ctx)
    o_ref[...] = y.astype(o_ref.dtype)


def _norm_mod(x, g, mod, k_shift, k_scale, n_ctx):
    t, d = x.shape
    tm = ROW_TILE
    return pl.pallas_call(
        functools.partial(_norm_mod_kernel, n_ctx=n_ctx),
        out_shape=jax.ShapeDtypeStruct((t, d), BF16),
        grid=(t // tm,),
        in_specs=[pl.BlockSpec((tm, d), lambda i: (i, 0)),
                  pl.BlockSpec((1, d), lambda i: (0, 0)),
                  pl.BlockSpec((8, d), lambda i: (0, k_shift)),
                  pl.BlockSpec((8, d), lambda i: (0, k_scale))],
        out_specs=pl.BlockSpec((tm, d), lambda i: (i, 0)),
        compiler_params=_cparams(("arbitrary",)),
        name="norm_mod",
    )(x, g.reshape(1, d), mod, mod)


def _cast_weight_once(w_ref, wbf_ref):
    @pl.when(pl.program_id(1) == 0)
    def _():
        wbf_ref[...] = w_ref[...].astype(BF16)


def _mm_plain_kernel(a_ref, w_ref, o_ref, wbf_ref, *, scale):
    _cast_weight_once(w_ref, wbf_ref)
    acc = jnp.dot(a_ref[...], wbf_ref[...], preferred_element_type=F32)
    if scale != 1.0:
        acc = acc * scale
    o_ref[...] = acc.astype(o_ref.dtype)


def _mm_residual_kernel(a_ref, w_ref, x_ref, mod_ref, o_ref, wbf_ref, *, n_ctx):
    _cast_weight_once(w_ref, wbf_ref)
    tm = a_ref.shape[0]
    acc = jnp.dot(a_ref[...], wbf_ref[...], preferred_element_type=F32)
    row = pl.program_id(1) * tm + lax.broadcasted_iota(I32, (tm, 1), 0)
    gate = jnp.where(row < n_ctx, mod_ref[1:2, :], mod_ref[0:1, :])
    o_ref[...] = x_ref[...] + gate * acc


def _mm_rope_kernel(a_ref, w_ref, cf_ref, sf_ref, cb_ref, sb_ref, o_ref, wbf_ref, *, scale, head_dim):
    _cast_weight_once(w_ref, wbf_ref)
    acc = jnp.dot(a_ref[...], wbf_ref[...], preferred_element_type=F32)
    if scale != 1.0:
        acc = acc * scale
    half = head_dim // 2
    for d, (c_ref, s_ref) in enumerate(((cf_ref, sf_ref), (cb_ref, sb_ref))):
        c, s = c_ref[...], s_ref[...]
        for h in range(acc.shape[1] // head_dim):
            x1 = acc[:, h * head_dim:h * head_dim + half]
            x2 = acc[:, h * head_dim + half:(h + 1) * head_dim]
            o_ref[d, :, h * head_dim:h * head_dim + half] = (x1 * c - x2 * s).astype(o_ref.dtype)
            o_ref[d, :, h * head_dim + half:(h + 1) * head_dim] = (x1 * s + x2 * c).astype(o_ref.dtype)


def _mm_tiles(m, k, n):
    tm = _pick_tile(m, (768, 512, 384, 256, 128))
    tn_cap = 1024 if k <= 2048 else 512
    tn = _pick_tile(n, tuple(c for c in (1024, 512, 256, 128) if c <= tn_cap))
    return tm, tn


def _matmul(a, w, layer, col0, n, out_dtype, scale=1.0):
    m, k = a.shape
    tm, tn = _mm_tiles(m, k, n)
    assert col0 % tn == 0
    c0 = col0 // tn
    return pl.pallas_call(
        functools.partial(_mm_plain_kernel, scale=scale),
        out_shape=jax.ShapeDtypeStruct((m, n), out_dtype),
        grid=(n // tn, m // tm),
        in_specs=[pl.BlockSpec((tm, k), lambda j, i: (i, 0)),
                  pl.BlockSpec((None, k, tn), lambda j, i: (layer, 0, j + c0))],
        out_specs=pl.BlockSpec((tm, tn), lambda j, i: (i, j)),
        scratch_shapes=[pltpu.VMEM((k, tn), BF16)],
        compiler_params=_cparams(("arbitrary", "arbitrary")),
        name="matmul",
    )(a, w)


def _matmul_residual(a, w, layer, x, mod, k_gate, n_ctx):
    m, k = a.shape
    n = w.shape[2]
    tm, tn = _mm_tiles(m, k, n)
    kb = k_gate * (n // tn)
    return pl.pallas_call(
        functools.partial(_mm_residual_kernel, n_ctx=n_ctx),
        out_shape=jax.ShapeDtypeStruct((m, n), F32),
        grid=(n // tn, m // tm),
        in_specs=[pl.BlockSpec((tm, k), lambda j, i: (i, 0)),
                  pl.BlockSpec((None, k, tn), lambda j, i: (layer, 0, j)),
                  pl.BlockSpec((tm, tn), lambda j, i: (i, j)),
                  pl.BlockSpec((8, tn), lambda j, i: (0, kb + j))],
        out_specs=pl.BlockSpec((tm, tn), lambda j, i: (i, j)),
        scratch_shapes=[pltpu.VMEM((k, tn), BF16)],
        compiler_params=_cparams(("arbitrary", "arbitrary")),
        name="matmul_residual",
    )(a, w, x, mod)


def _matmul_rope(a, w, layer, col0, n, tables, scale, head_dim):
    m, k = a.shape
    tm, tn = _mm_tiles(m, k, n)
    assert col0 % tn == 0 and tn % head_dim == 0
    c0 = col0 // tn
    half = head_dim // 2
    tab_spec = pl.BlockSpec((tm, half), lambda j, i: (i, 0))
    return pl.pallas_call(
        functools.partial(_mm_rope_kernel, scale=scale, head_dim=head_dim),
        out_shape=jax.ShapeDtypeStruct((2, m, n), BF16),
        grid=(n // tn, m // tm),
        in_specs=[pl.BlockSpec((tm, k), lambda j, i: (i, 0)),
                  pl.BlockSpec((None, k, tn), lambda j, i: (layer, 0, j + c0)),
                  tab_spec, tab_spec, tab_spec, tab_spec],
        out_specs=pl.BlockSpec((2, tm, tn), lambda j, i: (0, i, j)),
        scratch_shapes=[pltpu.VMEM((k, tn), BF16)],
        compiler_params=_cparams(("arbitrary", "arbitrary")),
        name="matmul_rope",
    )(a, w, *tables)


def _chunk_index(c, n_chunks, ctx_chunks, reverse):
    if not reverse:
        return c
    return jnp.where(c < ctx_chunks, ctx_chunks - 1 - c, n_chunks - 1 - (c - ctx_chunks))


def _dot_nt(a, b):
    return lax.dot_general(a, b, (((1,), (1,)), ((), ())), preferred_element_type=F32)


def _dot_tn(a, b):
    return lax.dot_general(a, b, (((0,), (0,)), ((), ())), preferred_element_type=F32)


def _retention_kernel(lg_ref, qf_ref, kf_ref, vf_ref, qb_ref, kb_ref, vb_ref, of_ref, ob_ref, sf_ref, sb_ref,
                      *, heads, dk, dv):
    c = pl.program_id(0)

    @pl.when(c == 0)
    def _():
        sf_ref[...] = jnp.zeros_like(sf_ref)
        sb_ref[...] = jnp.zeros_like(sb_ref)

    n = qf_ref.shape[0]
    ri = lax.broadcasted_iota(I32, (n, n), 0)
    ci = lax.broadcasted_iota(I32, (n, n), 1)
    r1 = lax.broadcasted_iota(I32, (n, 1), 0)
    streams = []
    for reverse, refs in ((False, (qf_ref, kf_ref, vf_ref, of_ref, sf_ref)),
                          (True, (qb_ref, kb_ref, vb_ref, ob_ref, sb_ref))):
        diff = (ci - ri) if reverse else (ri - ci)
        dmask = diff >= 0
        dist = jnp.where(dmask, diff, 0).astype(F32)
        pos = ((n - 1 - r1) if reverse else r1).astype(F32)
        streams.append((dmask, dist, pos) + refs)
    for h in range(heads):
        for d, (dmask, dist, pos, q_ref, k_ref, v_ref, o_ref, s_ref) in enumerate(streams):
            lg = jnp.full((1, 1), lg_ref[d * heads + h], F32)
            decay = jnp.where(dmask, jnp.exp(dist * lg), 0.0)
            q_dec = jnp.exp((pos + 1.0) * lg)
            k_dec = jnp.exp((n - 1.0 - pos) * lg)
            c_dec = jnp.exp(float(n) * lg)
            q = q_ref[:, h * dk:(h + 1) * dk]
            k = k_ref[:, h * dk:(h + 1) * dk]
            v = v_ref[:, h * dv:(h + 1) * dv]
            s_prev = s_ref[h]
            scores = (_dot_nt(q, k) * decay).astype(BF16)
            inner = jnp.dot(scores, v, preferred_element_type=F32)
            cross = jnp.dot(q, s_prev.astype(BF16), preferred_element_type=F32)
            o_ref[:, h * dv:(h + 1) * dv] = (inner + q_dec * cross).astype(o_ref.dtype)
            kd = (k.astype(F32) * k_dec).astype(BF16)
            s_ref[h] = c_dec * s_prev + _dot_tn(kd, v)


def _retention_scan(log_gamma, q, k, vg, n_ctx, heads, dk, dv):
    t = q.shape[1]
    nc, cc = t // CHUNK, n_ctx // CHUNK
    cm = lambda c: _chunk_index(c, nc, cc, True)
    qk_spec = lambda d, idx: pl.BlockSpec((None, CHUNK, heads * dk), lambda c: (d, idx(c), 0))
    v_spec = lambda idx: pl.BlockSpec((CHUNK, heads * dv), lambda c: (idx(c), 0))
    fwd = lambda c: c
    out = jax.ShapeDtypeStruct((t, heads * dv), BF16)
    return pl.pallas_call(
        functools.partial(_retention_kernel, heads=heads, dk=dk, dv=dv),
        out_shape=(out, out),
        grid=(nc,),
        in_specs=[pl.BlockSpec(memory_space=pltpu.SMEM),
                  qk_spec(0, fwd), qk_spec(0, fwd), v_spec(fwd),
                  qk_spec(1, cm), qk_spec(1, cm), v_spec(cm)],
        out_specs=(v_spec(fwd), v_spec(cm)),
        scratch_shapes=[pltpu.VMEM((heads, dk, dv), F32), pltpu.VMEM((heads, dk, dv), F32)],
        compiler_params=_cparams(("arbitrary",)),
        name="retention_scan",
    )(log_gamma.reshape(-1), q, k, vg, q, k, vg)


def _ret_post_kernel(yf_ref, yb_ref, g_ref, w_ref, b_ref, o_ref, *, heads, dv):
    for h in range(heads):
        sl = slice(h * dv, (h + 1) * dv)
        y = yf_ref[:, sl].astype(F32) + yb_ref[:, sl].astype(F32)
        mu = jnp.mean(y, axis=-1, keepdims=True)
        yc = y - mu
        var = jnp.mean(yc * yc, axis=-1, keepdims=True)
        yn = yc * lax.rsqrt(var + NORM_EPS)
        g = g_ref[:, sl].astype(F32)
        silu = g * (1.0 / (1.0 + jnp.exp(-g)))
        o_ref[:, sl] = ((yn * w_ref[:, sl] + b_ref[:, sl]) * silu).astype(o_ref.dtype)


def _ret_post(yf, yb, vg, gn_w, gn_b, heads, dv):
    t, vw = yf.shape
    tm = ROW_TILE
    return pl.pallas_call(
        functools.partial(_ret_post_kernel, heads=heads, dv=dv),
        out_shape=jax.ShapeDtypeStruct((t, vw), BF16),
        grid=(t // tm,),
        in_specs=[pl.BlockSpec((tm, vw), lambda i: (i, 0)),
                  pl.BlockSpec((tm, vw), lambda i: (i, 0)),
                  pl.BlockSpec((tm, vw), lambda i: (i, 1)),
                  pl.BlockSpec((1, vw), lambda i: (0, 0)),
                  pl.BlockSpec((1, vw), lambda i: (0, 0))],
        out_specs=pl.BlockSpec((tm, vw), lambda i: (i, 0)),
        compiler_params=_cparams(("arbitrary",)),
        name="retention_post",
    )(yf, yb, vg, gn_w.reshape(1, vw), gn_b.reshape(1, vw))


def _conv_silu_kernel(x_ref, prev_ref, next_ref, w_ref, o_ref, *, n_ctx, t_total, q_cols, q_scale):
    i = pl.program_id(0)
    tm = x_ref.shape[0]
    x = x_ref[...]
    row = lax.broadcasted_iota(I32, (tm, 1), 0)
    grow = i * tm + row
    halo = prev_ref.shape[0]
    x_prev = jnp.where(row == 0, prev_ref[halo - 1:halo, :], pltpu.roll(x, 1, axis=0))
    x_prev = jnp.where((grow == 0) | (grow == n_ctx), 0.0, x_prev)
    x_next = jnp.where(row == tm - 1, next_ref[0:1, :], pltpu.roll(x, tm - 1, axis=0))
    x_next = jnp.where((grow == n_ctx - 1) | (grow == t_total - 1), 0.0, x_next)
    y = x_prev * w_ref[0:1, :] + x * w_ref[1:2, :] + x_next * w_ref[2:3, :]
    y = y * (1.0 / (1.0 + jnp.exp(-y)))
    o_ref[:, :q_cols] = (y[:, :q_cols] * q_scale).astype(o_ref.dtype)
    o_ref[:, q_cols:] = y[:, q_cols:].astype(o_ref.dtype)


def _conv_silu(x, conv_w, n_ctx, q_cols, q_scale):
    t, c = x.shape
    tm = ROW_TILE
    halo = 8
    r = tm // halo
    last = t // halo - 1
    w8 = jnp.zeros((8, c), F32).at[:3].set(conv_w)
    return pl.pallas_call(
        functools.partial(_conv_silu_kernel, n_ctx=n_ctx, t_total=t, q_cols=q_cols, q_scale=q_scale),
        out_shape=jax.ShapeDtypeStruct((t, c), BF16),
        grid=(t // tm,),
        in_specs=[pl.BlockSpec((tm, c), lambda i: (i, 0)),
                  pl.BlockSpec((halo, c), lambda i: (jnp.maximum(i * r - 1, 0), 0)),
                  pl.BlockSpec((halo, c), lambda i: (jnp.minimum((i + 1) * r, last), 0)),
                  pl.BlockSpec((8, c), lambda i: (0, 0))],
        out_specs=pl.BlockSpec((tm, c), lambda i: (i, 0)),
        compiler_params=_cparams(("arbitrary",)),
        name="mlstm_conv_silu",
    )(x, x, x, w8)


def _mlstm_kernel(qk_ref, v_ref, gt_ref, bias_ref, o_ref, c_ref, n_ref, m_ref, *, reverse, heads, dk, dv, d_idx):
    c = pl.program_id(0)

    @pl.when(c == 0)
    def _():
        c_ref[...] = jnp.zeros_like(c_ref)
        n_ref[...] = jnp.zeros_like(n_ref)
        m_ref[...] = jnp.zeros_like(m_ref)

    n = qk_ref.shape[0]
    gates = gt_ref[...] + bias_ref[...]
    i_all = GATE_SOFTCAP * jnp.tanh(gates * (1.0 / GATE_SOFTCAP))
    f_all = -(jnp.maximum(-gates, 0.0) + jnp.log1p(jnp.exp(-jnp.abs(gates))))
    ri = lax.broadcasted_iota(I32, (n, n), 0)
    ci = lax.broadcasted_iota(I32, (n, n), 1)
    mask = (ci >= ri) if reverse else (ci <= ri)
    tri = jnp.where(mask, 1.0, 0.0).astype(F32)
    b_all = jnp.dot(tri, f_all, preferred_element_type=F32, precision=lax.Precision.HIGHEST)
    b_all_t = b_all.T
    i_all_t = i_all.T
    last = 0 if reverse else n - 1
    qk_w = heads * dk
    for h in range(heads):
        ic = d_idx * 2 * heads + h
        fc = ic + heads
        b_col = b_all[:, fc:fc + 1]
        b_row = b_all_t[fc:fc + 1, :]
        i_col = i_all[:, ic:ic + 1]
        i_row = i_all_t[ic:ic + 1, :]
        m_prev = m_ref[h][:, 0:1]
        q = qk_ref[:, h * dk:(h + 1) * dk]
        k = qk_ref[:, qk_w + h * dk:qk_w + (h + 1) * dk]
        v = v_ref[:, h * dv:(h + 1) * dv]
        c_prev = c_ref[h]
        n_prev = n_ref[h]
        d_log = jnp.where(mask, b_col - b_row + i_row, NEG_INF)
        m_t = jnp.maximum(b_col + m_prev, jnp.max(d_log, axis=-1, keepdims=True))
        w = jnp.exp(d_log - m_t)
        s = _dot_nt(q, k) * w
        inter = jnp.exp(b_col + m_prev - m_t)
        num = jnp.dot(s.astype(BF16), v, preferred_element_type=F32) + inter * jnp.dot(
            q, c_prev.astype(BF16), preferred_element_type=F32)
        den = jnp.sum(s, axis=-1, keepdims=True) + inter * jnp.sum(q.astype(F32) * n_prev, axis=-1, keepdims=True)
        o_ref[:, h * dv:(h + 1) * dv] = (num / jnp.maximum(jnp.abs(den), jnp.exp(-m_t))).astype(o_ref.dtype)
        b_last = b_col[last:last + 1, :]
        g = b_last - b_col + i_col
        m_new = jnp.maximum(b_last + m_prev, jnp.max(g, axis=0, keepdims=True))
        wk = jnp.exp(g - m_new) * k.astype(F32)
        dec = jnp.exp(b_last + m_prev - m_new)
        c_ref[h] = dec * c_prev + _dot_tn(wk.astype(BF16), v)
        n_ref[h] = dec * n_prev + jnp.sum(wk, axis=0, keepdims=True)
        m_ref[h] = jnp.broadcast_to(m_new, m_ref.shape[1:])


def _mlstm_scan(qk, vo, gates, bias, n_ctx, heads, dk, dv, reverse):
    t = qk.shape[0]
    nc, cc = t // CHUNK, n_ctx // CHUNK
    cm = lambda c: _chunk_index(c, nc, cc, reverse)
    return pl.pallas_call(
        functools.partial(_mlstm_kernel, reverse=reverse, heads=heads, dk=dk, dv=dv, d_idx=1 if reverse else 0),
        out_shape=jax.ShapeDtypeStruct((t, heads * dv), BF16),
        grid=(nc,),
        in_specs=[pl.BlockSpec((CHUNK, 2 * heads * dk), lambda c: (cm(c), 0)),
                  pl.BlockSpec((CHUNK, heads * dv), lambda c: (cm(c), 0)),
                  pl.BlockSpec((CHUNK, LANES), lambda c: (cm(c), 0)),
                  pl.BlockSpec((1, LANES), lambda c: (0, 0))],
        out_specs=pl.BlockSpec((CHUNK, heads * dv), lambda c: (cm(c), 0)),
        scratch_shapes=[pltpu.VMEM((heads, dk, dv), F32),
                        pltpu.VMEM((heads, 1, dk), F32),
                        pltpu.VMEM((heads, 1, LANES), F32)],
        compiler_params=_cparams(("arbitrary",)),
        name="mlstm_scan_rev" if reverse else "mlstm_scan_fwd",
    )(qk, vo, gates, bias)


def _mlstm_post_kernel(yf_ref, yb_ref, o_ref_in, w_ref, o_ref, *, heads, dv):
    for h in range(heads):
        sl = slice(h * dv, (h + 1) * dv)
        y = _rms(yf_ref[:, sl].astype(F32) + yb_ref[:, sl].astype(F32))
        o = o_ref_in[:, sl].astype(F32)
        o_ref[:, sl] = (y * w_ref[:, sl] * (1.0 / (1.0 + jnp.exp(-o)))).astype(o_ref.dtype)


def _mlstm_post(yf, yb, vo, norm_w, heads, dv):
    t, vw = yf.shape
    tm = ROW_TILE
    return pl.pallas_call(
        functools.partial(_mlstm_post_kernel, heads=heads, dv=dv),
        out_shape=jax.ShapeDtypeStruct((t, vw), BF16),
        grid=(t // tm,),
        in_specs=[pl.BlockSpec((tm, vw), lambda i: (i, 0)),
                  pl.BlockSpec((tm, vw), lambda i: (i, 0)),
                  pl.BlockSpec((tm, vw), lambda i: (i, 1)),
                  pl.BlockSpec((1, vw), lambda i: (0, 0))],
        out_specs=pl.BlockSpec((tm, vw), lambda i: (i, 0)),
        compiler_params=_cparams(("arbitrary",)),
        name="mlstm_post",
    )(yf, yb, vo, norm_w.reshape(1, vw))


def _qk_norm_rope_kernel(x_ref, cos_ref, sin_ref, qn_ref, kn_ref, q_ref, k_ref, *, q_heads, k_heads, hd):
    cos, sin = cos_ref[...], sin_ref[...]
    lane = lax.broadcasted_iota(I32, (1, hd), 1)
    first = (lane % (hd // 2)) < (hd // 4)
    for h in range(q_heads + k_heads):
        x = x_ref[:, h * hd:(h + 1) * hd]
        w = qn_ref[...] if h < q_heads else kn_ref[...]
        xh = _rms(x) * w
        partner = jnp.where(first, pltpu.roll(xh, hd - hd // 4, axis=1), pltpu.roll(xh, hd // 4, axis=1))
        y = (xh * cos + partner * sin).astype(q_ref.dtype)
        if h < q_heads:
            q_ref[:, h * hd:(h + 1) * hd] = y
        else:
            k_ref[:, (h - q_heads) * hd:(h - q_heads + 1) * hd] = y


def _qk_norm_rope(qk, cos, sin, q_norm, k_norm, q_heads, k_heads, hd):
    t = qk.shape[0]
    tm = ROW_TILE
    return pl.pallas_call(
        functools.partial(_qk_norm_rope_kernel, q_heads=q_heads, k_heads=k_heads, hd=hd),
        out_shape=(jax.ShapeDtypeStruct((t, q_heads * hd), BF16), jax.ShapeDtypeStruct((t, k_heads * hd), BF16)),
        grid=(t // tm,),
        in_specs=[pl.BlockSpec((tm, (q_heads + k_heads) * hd), lambda i: (i, 0)),
                  pl.BlockSpec((tm, hd), lambda i: (i, 0)),
                  pl.BlockSpec((tm, hd), lambda i: (i, 0)),
                  pl.BlockSpec((1, hd), lambda i: (0, 0)),
                  pl.BlockSpec((1, hd), lambda i: (0, 0))],
        out_specs=(pl.BlockSpec((tm, q_heads * hd), lambda i: (i, 0)),
                   pl.BlockSpec((tm, k_heads * hd), lambda i: (i, 0))),
        compiler_params=_cparams(("arbitrary",)),
        name="attn_qk_norm_rope",
    )(qk, cos, sin, q_norm.reshape(1, hd), k_norm.reshape(1, hd))


def _flash_kernel(q_ref, k_ref, v_ref, o_ref, m_ref, l_ref, acc_ref, alpha_ref, sa_ref, sb_ref, p_ref,
                  *, groups, hd, tk, n_ctx):
    tq = q_ref.shape[0]
    n_kv = k_ref.shape[0]
    n = n_kv // tk
    q = jnp.concatenate([q_ref[:, g * hd:(g + 1) * hd] for g in range(groups)], axis=0)
    m_ref[...] = jnp.full_like(m_ref, NEG_INF)
    l_ref[...] = jnp.zeros_like(l_ref)
    acc_ref[...] = jnp.zeros_like(acc_ref)

    def scores(off, size):
        return _dot_nt(q, k_ref[pl.ds(off, size), :])

    def absorb(s_ref, off, size):
        for band in range(groups * tq // FLASH_BAND):
            rows = slice(band * FLASH_BAND, (band + 1) * FLASH_BAND)
            s = s_ref[rows, :size]
            m_prev = m_ref[rows, :]
            m_new = jnp.maximum(m_prev, jnp.max(s, axis=-1, keepdims=True))
            alpha = jnp.exp2(m_prev - m_new)
            p = jnp.exp2(s - jnp.concatenate([m_new] * (size // LANES), axis=1))
            l_ref[rows, :] = alpha * l_ref[rows, :] + jnp.sum(p, axis=-1, keepdims=True)
            m_ref[rows, :] = m_new
            alpha_ref[rows, :] = alpha
            p_ref[rows, :size] = p.astype(BF16)
        acc_ref[...] = alpha_ref[...] * acc_ref[...] + jnp.dot(p_ref[:, :size], v_ref[pl.ds(off, size), :],
                                                                preferred_element_type=F32)

    is_ctx = pl.program_id(1) < n_ctx // tq

    @pl.when(is_ctx)
    def _():
        sa_ref[:, :n_ctx] = scores(0, n_ctx)
        absorb(sa_ref, 0, n_ctx)

    @pl.when(jnp.logical_not(is_ctx))
    def _():
        bufs = (sa_ref, sb_ref)
        bufs[0][...] = scores(0, tk)
        for j in range(n):
            if j + 1 < n:
                bufs[(j + 1) % 2][...] = scores((j + 1) * tk, tk)
            absorb(bufs[j % 2], j * tk, tk)

    out = acc_ref[...] / l_ref[...]
    for g in range(groups):
        o_ref[:, g * hd:(g + 1) * hd] = out[g * tq:(g + 1) * tq].astype(o_ref.dtype)


def _flash_attention(q, k, v, n_ctx, kv_heads, groups, hd):
    t = q.shape[0]
    tq = _pick_tile(math.gcd(t, n_ctx), (256, 128))
    tk = _pick_tile(t, (768, 512, 256, 128))
    assert hd == LANES and n_ctx <= tk and (groups * tq) % FLASH_BAND == 0
    gw = groups * hd
    return pl.pallas_call(
        functools.partial(_flash_kernel, groups=groups, hd=hd, tk=tk, n_ctx=n_ctx),
        out_shape=jax.ShapeDtypeStruct((t, kv_heads * gw), BF16),
        grid=(kv_heads, t // tq),
        in_specs=[pl.BlockSpec((tq, gw), lambda g, i: (i, g)),
                  pl.BlockSpec((t, hd), lambda g, i: (0, g)),
                  pl.BlockSpec((t, hd), lambda g, i: (0, g))],
        out_specs=pl.BlockSpec((tq, gw), lambda g, i: (i, g)),
        scratch_shapes=[pltpu.VMEM((groups * tq, LANES), F32),
                        pltpu.VMEM((groups * tq, LANES), F32),
                        pltpu.VMEM((groups * tq, hd), F32),
                        pltpu.VMEM((groups * tq, LANES), F32),
                        pltpu.VMEM((groups * tq, tk), F32),
                        pltpu.VMEM((groups * tq, tk), F32),
                        pltpu.VMEM((groups * tq, tk), BF16)],
        compiler_params=_cparams(("arbitrary", "arbitrary")),
        name="flash_attention",
    )(q, k, v)


def _router_kernel(x_ref, g_ref, sh_ref, sc_ref, wr_ref, br_ref, h_ref, ids_ref, wts_ref, cnt_ref, run_ref,
                   *, n_ctx, row0):
    i = pl.program_id(0)
    tm, d = x_ref.shape

    @pl.when(i == 0)
    def _():
        run_ref[...] = jnp.zeros_like(run_ref)

    ib = i + row0 // tm
    h = _rms(x_ref[...]) * g_ref[...]
    h = h * (1.0 + _mod_row(sc_ref, ib, tm, n_ctx)) + _mod_row(sh_ref, ib, tm, n_ctx)
    h_ref[...] = h
    h_hi = h.astype(BF16)
    h_lo = (h - h_hi.astype(F32)).astype(BF16)
    hw = jnp.dot(h_hi, wr_ref[...], preferred_element_type=F32)
    lw = jnp.dot(h_lo, wr_ref[:, :LANES], preferred_element_type=F32)
    logits = hw[:, :LANES] + (hw[:, LANES:] + lw) + br_ref[...]
    lane = lax.broadcasted_iota(I32, (tm, LANES), 1)
    big = jnp.int32(LANES)
    gl = jnp.where((lane >= MOE_EXPERTS) & (lane < MOE_EXPERTS + MOE_GROUPS), logits, NEG_INF)
    g_max = jnp.max(gl, axis=-1, keepdims=True)
    g_idx = jnp.min(jnp.where(gl == g_max, lane, big), axis=-1, keepdims=True) - MOE_EXPERTS
    g_w = 1.0 / jnp.sum(jnp.exp(gl - g_max), axis=-1, keepdims=True)
    el = jnp.where((lane < MOE_EXPERTS) & ((lane >> 3) == g_idx), logits, NEG_INF)
    m1 = jnp.max(el, axis=-1, keepdims=True)
    i1 = jnp.min(jnp.where(el == m1, lane, big), axis=-1, keepdims=True)
    el2 = jnp.where(lane == i1, NEG_INF, el)
    m2 = jnp.max(el2, axis=-1, keepdims=True)
    i2 = jnp.min(jnp.where(el2 == m2, lane, big), axis=-1, keepdims=True)
    e_sum = jnp.sum(jnp.exp(el - m1), axis=-1, keepdims=True)
    p1 = 1.0 / e_sum
    p2 = jnp.exp(m2 - m1) / e_sum
    w1 = g_w * (p1 / (p1 + p2))
    w2 = g_w * (p2 / (p1 + p2))
    onehot = jnp.where((lane == i1) | (lane == i2), 1.0, 0.0)
    ri = lax.broadcasted_iota(I32, (tm, tm), 0)
    ci = lax.broadcasted_iota(I32, (tm, tm), 1)
    strict = jnp.where(ci < ri, 1.0, 0.0).astype(BF16)
    before = jnp.dot(strict, onehot.astype(BF16), preferred_element_type=F32) + run_ref[0:1, :]
    r1 = jnp.sum(jnp.where(lane == i1, before, 0.0), axis=-1, keepdims=True).astype(I32)
    r2 = jnp.sum(jnp.where(lane == i2, before, 0.0), axis=-1, keepdims=True).astype(I32)
    ids_ref[...] = jnp.where(lane == 0, i1, jnp.where(lane == 1, i2, jnp.where(lane == 2, r1, jnp.where(lane == 3, r2, 0))))
    wts_ref[...] = jnp.where(lane == 0, w1, jnp.where(lane == 1, w2, 0.0))
    total = run_ref[0:1, :] + jnp.sum(onehot, axis=0, keepdims=True)
    run_ref[...] = jnp.broadcast_to(total, run_ref.shape)
    cnt_ref[...] = jnp.broadcast_to(total, cnt_ref.shape)


def _router(x, g, mod, k_shift, k_scale, w_router, b_router, n_ctx, row0):
    t, d = x.shape
    tm = ROW_TILE
    n = t - row0
    off = row0 // tm
    return pl.pallas_call(
        functools.partial(_router_kernel, n_ctx=n_ctx, row0=row0),
        out_shape=(jax.ShapeDtypeStruct((n, d), F32),
                   jax.ShapeDtypeStruct((n, LANES), I32),
                   jax.ShapeDtypeStruct((n, LANES), F32),
                   jax.ShapeDtypeStruct((8, LANES), F32)),
        grid=(n // tm,),
        in_specs=[pl.BlockSpec((tm, d), lambda i: (i + off, 0)),
                  pl.BlockSpec((1, d), lambda i: (0, 0)),
                  pl.BlockSpec((8, d), lambda i: (0, k_shift)),
                  pl.BlockSpec((8, d), lambda i: (0, k_scale)),
                  pl.BlockSpec((d, 2 * LANES), lambda i: (0, 0)),
                  pl.BlockSpec((1, LANES), lambda i: (0, 0))],
        out_specs=(pl.BlockSpec((tm, d), lambda i: (i, 0)),
                   pl.BlockSpec((tm, LANES), lambda i: (i, 0)),
                   pl.BlockSpec((tm, LANES), lambda i: (i, 0)),
                   pl.BlockSpec((8, LANES), lambda i: (0, 0))),
        scratch_shapes=[pltpu.VMEM((8, LANES), F32)],
        compiler_params=_cparams(("arbitrary",)),
        name="moe_router",
    )(x, g.reshape(1, d), mod, mod, w_router, b_router)


def _slot_index_kernel(ids_ref, seg_ref, o_ref):
    ids = ids_ref[...]
    lane = lax.broadcasted_iota(I32, ids.shape, 1)
    seg = seg_ref[...]
    slots = []
    for k in range(MOE_TOPK):
        start = jnp.sum(jnp.where(lane == ids[:, k:k + 1], seg, 0.0), axis=-1, keepdims=True)
        slots.append(start.astype(I32) + ids[:, MOE_TOPK + k:MOE_TOPK + k + 1])
    o_ref[...] = jnp.where(lane == 0, slots[0], jnp.where(lane == 1, slots[1], 0))


def _slot_index(ids, seg_start):
    n = ids.shape[0]
    tm = ROW_TILE
    seg = jnp.zeros((1, LANES), F32).at[0, :MOE_EXPERTS].set(seg_start.astype(F32))
    out = pl.pallas_call(
        _slot_index_kernel,
        out_shape=jax.ShapeDtypeStruct((n, LANES), I32),
        grid=(n // tm,),
        in_specs=[pl.BlockSpec((tm, LANES), lambda i: (i, 0)),
                  pl.BlockSpec((1, LANES), lambda i: (0, 0))],
        out_specs=pl.BlockSpec((tm, LANES), lambda i: (i, 0)),
        compiler_params=_cparams(("arbitrary",)),
        name="moe_slot_index",
    )(ids, seg)
    return out[:, :MOE_TOPK].reshape(-1)


def _dispatch_kernel(dest_ref, h_ref, xs_in_ref, xs_ref, sem):
    del xs_in_ref
    i = pl.program_id(0)
    tm = h_ref.shape[0]

    def row_copy(r, k):
        slot = dest_ref[(i * tm + r) * MOE_TOPK + k]
        return pltpu.make_async_copy(h_ref.at[pl.ds(r, 1)], xs_ref.at[pl.ds(slot, 1)], sem)

    def start(r, carry):
        for k in range(MOE_TOPK):
            row_copy(r, k).start()
        return carry

    def wait(r, carry):
        for k in range(MOE_TOPK):
            row_copy(r, k).wait()
        return carry

    lax.fori_loop(0, tm, start, 0, unroll=8)
    lax.fori_loop(0, tm, wait, 0, unroll=8)


def _dispatch(dest, h, xs_init):
    n, w = h.shape
    tm = ROW_TILE
    return pl.pallas_call(
        _dispatch_kernel,
        out_shape=jax.ShapeDtypeStruct(xs_init.shape, xs_init.dtype),
        grid_spec=pltpu.PrefetchScalarGridSpec(
            num_scalar_prefetch=1,
            grid=(n // tm,),
            in_specs=[pl.BlockSpec((tm, w), lambda i, dest: (i, 0)),
                      pl.BlockSpec(memory_space=pl.ANY)],
            out_specs=pl.BlockSpec(memory_space=pl.ANY),
            scratch_shapes=[pltpu.SemaphoreType.DMA(())]),
        input_output_aliases={2: 0},
        compiler_params=_cparams(("arbitrary",)),
        name="moe_dispatch",
    )(dest, h, xs_init)


def _expert_kernel(be_ref, first_ref, next_ref, nb_ref, xs_ref, wg_hbm, wu_hbm, wd_hbm, ys_ref,
                   wg_st, wu_st, wd_st, wg_bf, wu_bf, wd_bf, sem, *, layer):
    b = pl.program_id(0)
    used = b < nb_ref[0]

    weights = ((wg_hbm, wg_st, wg_bf), (wu_hbm, wu_st, wu_bf), (wd_hbm, wd_st, wd_bf))

    def fetch(e, idx):
        hbm, st, _ = weights[idx]
        return pltpu.make_async_copy(hbm.at[layer, e], st, sem.at[idx])

    @pl.when(b == 0)
    def _():
        for idx in range(len(weights)):
            fetch(be_ref[0], idx).start()

    @pl.when(used & (first_ref[b] == 1))
    def _():
        has_next = next_ref[b] >= 0
        for idx, (_, st, bf) in enumerate(weights):
            fetch(be_ref[b], idx).wait()
            bf[...] = st[...].astype(BF16)

            @pl.when(has_next)
            def _():
                fetch(next_ref[b], idx).start()

    @pl.when(used)
    def _():
        x = xs_ref[...].astype(BF16)
        gate = jnp.dot(x, wg_bf[...], preferred_element_type=F32)
        up = jnp.dot(x, wu_bf[...], preferred_element_type=F32)
        act = (gate * (1.0 / (1.0 + jnp.exp(-gate))) * up).astype(BF16)
        ys_ref[...] = jnp.dot(act, wd_bf[...], preferred_element_type=F32)

    @pl.when(jnp.logical_not(used))
    def _():
        ys_ref[...] = jnp.zeros_like(ys_ref)


def _experts(block_expert, first, next_expert, n_used, xs, w_gate, w_up, w_down, layer):
    n_slots, d = xs.shape
    ff = w_gate.shape[3]
    nb = n_slots // MOE_BLOCK
    row_spec = pl.BlockSpec((MOE_BLOCK, d), lambda b, *_: (b, 0))
    hbm = pl.BlockSpec(memory_space=pl.ANY)
    return pl.pallas_call(
        functools.partial(_expert_kernel, layer=layer),
        out_shape=jax.ShapeDtypeStruct((n_slots, d), F32),
        grid_spec=pltpu.PrefetchScalarGridSpec(
            num_scalar_prefetch=4,
            grid=(nb,),
            in_specs=[row_spec, hbm, hbm, hbm],
            out_specs=row_spec,
            scratch_shapes=[pltpu.VMEM((d, ff), F32), pltpu.VMEM((d, ff), F32), pltpu.VMEM((ff, d), F32),
                            pltpu.VMEM((d, ff), BF16), pltpu.VMEM((d, ff), BF16), pltpu.VMEM((ff, d), BF16),
                            pltpu.SemaphoreType.DMA((3,))]),
        compiler_params=_cparams(("arbitrary",)),
        name="moe_experts",
    )(block_expert, first, next_expert, n_used, xs, w_gate, w_up, w_down)


def _combine_kernel(dest_ref, x_ref, wts_ref, mod_ref, ys_ref, g_ref, sh_ref, sc_ref, *rest, n_ctx, row0, last):
    if last:
        o_ref, buf, sem = rest
    else:
        o_ref, h_ref, buf, sem = rest
    i = pl.program_id(0)
    tm = x_ref.shape[0]
    cur = i % 2

    def row_copy(block, half, r, k):
        slot = dest_ref[(block * tm + r) * MOE_TOPK + k]
        return pltpu.make_async_copy(ys_ref.at[pl.ds(slot, 1)], buf.at[half, k, pl.ds(r, 1)], sem.at[half])

    def for_rows(fn):
        def body(r, carry):
            for k in range(MOE_TOPK):
                fn(r, k)
            return carry
        lax.fori_loop(0, tm, body, 0, unroll=8)

    @pl.when(i == 0)
    def _():
        for_rows(lambda r, k: row_copy(0, 0, r, k).start())

    @pl.when(i + 1 < pl.num_programs(0))
    def _():
        for_rows(lambda r, k: row_copy(i + 1, 1 - cur, r, k).start())

    for_rows(lambda r, k: row_copy(i, cur, r, k).wait())
    w = wts_ref[...]
    y = w[:, 0:1] * buf[cur, 0] + w[:, 1:2] * buf[cur, 1]
    ib = i + row0 // tm
    x_new = x_ref[...] + _mod_row(mod_ref, ib, tm, n_ctx) * y
    hn = _rms(x_new) * g_ref[...]
    if last:
        o_ref[...] = hn
    else:
        o_ref[...] = x_new
        hn = hn * (1.0 + _mod_row(sc_ref, ib, tm, n_ctx)) + _mod_row(sh_ref, ib, tm, n_ctx)
        h_ref[...] = hn.astype(h_ref.dtype)


def _combine(dest, x, wts, mod, k_gate, ys, n_ctx, row0, g_next, mod_next, last):
    t, d = x.shape
    tm = ROW_TILE
    n = t - row0
    off = row0 // tm
    row_spec = pl.BlockSpec((tm, d), lambda i, dest: (i, 0))
    out_shape = jax.ShapeDtypeStruct((n, d), F32)
    out_specs = row_spec
    if not last:
        out_shape = (out_shape, jax.ShapeDtypeStruct((n, d), BF16))
        out_specs = (row_spec, row_spec)
    return pl.pallas_call(
        functools.partial(_combine_kernel, n_ctx=n_ctx, row0=row0, last=last),
        out_shape=out_shape,
        grid_spec=pltpu.PrefetchScalarGridSpec(
            num_scalar_prefetch=1,
            grid=(n // tm,),
            in_specs=[pl.BlockSpec((tm, d), lambda i, dest: (i + off, 0)),
                      pl.BlockSpec((tm, LANES), lambda i, dest: (i, 0)),
                      pl.BlockSpec((8, d), lambda i, dest: (0, k_gate)),
                      pl.BlockSpec(memory_space=pl.ANY),
                      pl.BlockSpec((1, d), lambda i, dest: (0, 0)),
                      pl.BlockSpec((8, d), lambda i, dest: (0, 0)),
                      pl.BlockSpec((8, d), lambda i, dest: (0, 1))],
            out_specs=out_specs,
            scratch_shapes=[pltpu.VMEM((2, MOE_TOPK, tm, d), F32), pltpu.SemaphoreType.DMA((2,))]),
        compiler_params=_cparams(("arbitrary",)),
        name="moe_combine",
    )(dest, x, wts, mod, ys, g_next.reshape(1, d), mod_next, mod_next)


def _hier_moe(x, g, mod, w_rg, b_rg, w_re, b_re, w_gate, w_up, w_down, layer, n_ctx, row0, xs_buf,
              g_next, mod_next, last):
    t, d = x.shape
    n = t - row0
    pad = LANES - MOE_EXPERTS - MOE_GROUPS
    w_router = jnp.concatenate([w_re, w_rg, jnp.zeros((d, pad), F32)], axis=1)
    w_hi = w_router.astype(BF16)
    w_router = jnp.concatenate([w_hi, (w_router - w_hi.astype(F32)).astype(BF16)], axis=1)
    b_router = jnp.concatenate([b_re, b_rg, jnp.zeros((pad,), F32)]).reshape(1, LANES)
    h, ids, wts, counts = _router(x, g, mod, 3, 4, w_router, b_router, n_ctx, row0)
    counts = counts[0, :MOE_EXPERTS].astype(I32)
    padded = (counts + MOE_BLOCK - 1) // MOE_BLOCK * MOE_BLOCK
    seg_end = jnp.cumsum(padded)
    seg_start = seg_end - padded
    dest = _slot_index(ids, seg_start)
    n_blocks = (t * MOE_TOPK + MOE_EXPERTS * (MOE_BLOCK - 1)) // MOE_BLOCK + 1
    block_start = jnp.arange(n_blocks, dtype=I32) * MOE_BLOCK
    block_expert = jnp.minimum(jnp.sum((seg_end[None, :] <= block_start[:, None]).astype(I32), axis=1),
                               MOE_EXPERTS - 1).astype(I32)
    n_used = (seg_end[-1:] // MOE_BLOCK).astype(I32)
    block_id = jnp.arange(n_blocks, dtype=I32)
    prev_expert = jnp.concatenate([jnp.full((1,), -1, I32), block_expert[:-1]])
    first = ((block_id < n_used[0]) & (block_expert != prev_expert)).astype(I32)
    eid = jnp.arange(MOE_EXPERTS, dtype=I32)
    owner = jnp.where(padded > 0, eid, MOE_EXPERTS)
    later = jnp.concatenate([lax.cummin(owner[::-1])[::-1][1:], jnp.full((1,), MOE_EXPERTS, I32)])
    next_expert = jnp.where(later < MOE_EXPERTS, later, -1)[block_expert].astype(I32)
    if xs_buf is None:
        xs_buf = jnp.zeros((n_blocks * MOE_BLOCK, d), F32)
    xs = _dispatch(dest, h, xs_buf)
    ys = _experts(block_expert, first, next_expert, n_used, xs, w_gate, w_up, w_down, layer)
    return _combine(dest, x, wts, mod, 5, ys, n_ctx, row0, g_next, mod_next, last), xs


def _rope_table(pos, dim, base):
    inv = jnp.power(base, -jnp.arange(0, dim, 2, dtype=F32) / dim)
    ang = pos.astype(F32)[:, None] * inv[None, :]
    return jnp.cos(ang), jnp.sin(ang)


def _flipped_positions(t, n_ctx):
    idx = jnp.arange(t, dtype=I32)
    return jnp.where(idx < n_ctx, n_ctx - 1 - idx, n_ctx + (t - 1 - idx))


def _axial_tables(n_ctx, n_lat, hd):
    rows = n_lat // GRID_W
    row = jnp.repeat(jnp.arange(rows, dtype=I32), GRID_W)
    col = jnp.tile(jnp.arange(GRID_W, dtype=I32), rows)
    half = hd // 2
    cr, sr = _rope_table(row, half, ROPE_BASE)
    cc, sc = _rope_table(col, half, ROPE_BASE)
    cos = jnp.concatenate([cr, cr, cc, cc], axis=1)
    sin = jnp.concatenate([-sr, sr, -sc, sc], axis=1)
    cos = jnp.concatenate([jnp.ones((n_ctx, hd), F32), cos], axis=0)
    sin = jnp.concatenate([jnp.zeros((n_ctx, hd), F32), sin], axis=0)
    return cos, sin


def _retention_layer(xa, h, mod, w_in, logit_gamma, gn_w, gn_b, w_out, j, n_ctx):
    t, d = xa.shape
    heads = RET_HEADS
    dk = d // heads
    dv = 2 * dk
    qk_w, v_w = heads * dk, heads * dv
    pos_f = jnp.arange(t, dtype=I32)
    cf, sf = _rope_table(pos_f, dk, RET_ROPE_BASE)
    cb, sb = _rope_table(_flipped_positions(t, n_ctx), dk, RET_ROPE_BASE)
    tables = (cf, sf, cb, sb)
    q = _matmul_rope(h, w_in, j, 0, qk_w, tables, dk ** -0.5, dk)
    k = _matmul_rope(h, w_in, j, qk_w, qk_w, tables, 1.0, dk)
    vg = _matmul(h, w_in, j, 2 * qk_w, 2 * v_w, BF16)
    log_gamma = jax.nn.log_sigmoid(logit_gamma.astype(F32))
    yf, yb = _retention_scan(log_gamma, q, k, vg, n_ctx, heads, dk, dv)
    yn = _ret_post(yf, yb, vg, gn_w, gn_b, heads, dv)
    return _matmul_residual(yn, w_out, j, xa, mod, 2, n_ctx)


def _mlstm_layer(xa, h, mod, w_in, conv_w, b_gate, norm_w, w_out, j, n_ctx):
    t, d = xa.shape
    heads = MLSTM_HEADS
    dk = d // (2 * heads)
    dv = d // heads
    qk_w, v_w = heads * dk, heads * dv
    n_gates = 4 * heads
    qk_pre = _matmul(h, w_in, j, 0, 2 * qk_w, F32)
    vo = _matmul(h, w_in, j, 2 * qk_w, 2 * v_w, BF16)
    w_gates = jnp.pad(w_in[j, :, 2 * qk_w + 2 * v_w:], ((0, 0), (0, LANES - n_gates)))
    gates = _matmul(h, w_gates[None], 0, 0, LANES, F32)
    bias = jnp.pad(b_gate.astype(F32).reshape(1, n_gates), ((0, 0), (0, LANES - n_gates)))
    qk = _conv_silu(qk_pre, conv_w, n_ctx, qk_w, dk ** -0.5)
    yf = _mlstm_scan(qk, vo, gates, bias, n_ctx, heads, dk, dv, reverse=False)
    yb = _mlstm_scan(qk, vo, gates, bias, n_ctx, heads, dk, dv, reverse=True)
    yn = _mlstm_post(yf, yb, vo, norm_w, heads, dv)
    return _matmul_residual(yn, w_out, j, xa, mod, 2, n_ctx)


def _attention_layer(xa, h, mod, w_in, q_norm, k_norm, w_out, j, n_ctx):
    t, d = xa.shape
    hd = d // ATTN_HEADS
    groups = ATTN_HEADS // ATTN_KV_HEADS
    q_w, kv_w = ATTN_HEADS * hd, ATTN_KV_HEADS * hd
    n_lat = t - n_ctx
    qk_pre = _matmul(h, w_in, j, 0, q_w + kv_w, F32)
    v = _matmul(h, w_in, j, q_w + kv_w, kv_w, BF16)
    cos, sin = _axial_tables(n_ctx, n_lat, hd)
    q_gain = q_norm * (hd ** -0.5 * math.log2(math.e))
    q, k = _qk_norm_rope(qk_pre, cos, sin, q_gain, k_norm, ATTN_HEADS, ATTN_KV_HEADS, hd)
    o = _flash_attention(q, k, v, n_ctx, ATTN_KV_HEADS, groups, hd)
    return _matmul_residual(o, w_out, j, xa, mod, 2, n_ctx)


def kernel(x, c, ctx, c_ctx, w_ada, b_ada, norm_g, ret_w_in, ret_logit_gamma, ret_gn_w, ret_gn_b, ret_w_out, mlstm_w_in, mlstm_conv_w, mlstm_b_gate, mlstm_norm_w, mlstm_w_out, attn_w_in, attn_q_norm, attn_k_norm, attn_w_out, moe_w_router_group, moe_b_router_group, moe_w_router_expert, moe_b_router_expert, moe_w_gate, moe_w_up, moe_w_down, final_norm_g):
    bsz, n_lat, d = x.shape
    n_ctx = ctx.shape[1]
    depth = w_ada.shape[0]
    assert bsz == 1 and n_ctx % ROW_TILE == 0 and n_lat % ROW_TILE == 0 and n_lat % GRID_W == 0
    xa = jnp.concatenate([ctx[0], x[0]], axis=0)
    s = jnp.stack([jax.nn.silu(c[0]), jax.nn.silu(c_ctx)])
    s8 = jnp.zeros((8, d), F32).at[:2].set(s).astype(BF16)
    mods = _ada_modulation(s8, w_ada, b_ada)
    xs_buf = None
    h = _norm_mod(xa, norm_g[0, 0], mods[0], 0, 1, n_ctx)
    for i in range(depth):
        kind, j = i % 3, i // 3
        mod = mods[i]
        last = i == depth - 1
        if kind == 0:
            xa = _retention_layer(xa, h, mod, ret_w_in, ret_logit_gamma[j], ret_gn_w[j], ret_gn_b[j],
                                  ret_w_out, j, n_ctx)
        elif kind == 1:
            xa = _mlstm_layer(xa, h, mod, mlstm_w_in, mlstm_conv_w[j], mlstm_b_gate[j],
                              mlstm_norm_w[j], mlstm_w_out, j, n_ctx)
        else:
            xa = _attention_layer(xa, h, mod, attn_w_in, attn_q_norm[j], attn_k_norm[j],
                                  attn_w_out, j, n_ctx)
        row0 = n_ctx if last else 0
        g_next = final_norm_g if last else norm_g[i + 1, 0]
        mod_next = mod if last else mods[i + 1]
        out, xs_buf = _hier_moe(xa, norm_g[i, 1], mod, moe_w_router_group[i], moe_b_router_group[i],
                                moe_w_router_expert[i], moe_b_router_expert[i], moe_w_gate, moe_w_up, moe_w_down, i,
                                n_ctx, row0, xs_buf, g_next, mod_next, last)
        if last:
            return out[None]
        xa, h = out
```

```python
import functools
import math

import jax
import jax.numpy as jnp
import numpy as np
from jax import lax
from jax.experimental import pallas as pl
from jax.experimental.pallas import tpu as pltpu

F32 = jnp.float32
BF16 = jnp.bfloat16
I32 = jnp.int32
U32 = jnp.uint32

LANES = 128
CHUNK = 128
NORM_EPS = 1e-6
GRID_W = 64
RET_HEADS = 8
RET_ROPE_BASE = 10000.0
MLSTM_HEADS = 8
GATE_SOFTCAP = 15.0
ATTN_HEADS = 16
ATTN_KV_HEADS = 8
ROPE_BASE = 10000.0
MOE_GROUPS = 4
MOE_PER_GROUP = 8
MOE_EXPERTS = MOE_GROUPS * MOE_PER_GROUP
MOE_TOPK = 2
MOE_BLOCK = 256
ROW_TILE = 256
FLASH_BAND = 32
VMEM_LIMIT = 56 * 1024 * 1024
NEG_INF = float("-inf")


def _cparams(sem, vmem=VMEM_LIMIT):
    return pltpu.CompilerParams(dimension_semantics=sem, vmem_limit_bytes=vmem)


def _pick_tile(total, candidates):
    for c in candidates:
        if total % c == 0:
            return c
    raise ValueError(f"no tile for {total} in {candidates}")


def _mod_row(mod_ref, i, tile, n_ctx):
    sel = jnp.where(i < n_ctx // tile, 1, 0)
    return mod_ref[pl.ds(sel, 1), :]


def _ada_kernel(s_ref, w_ref, b_ref, o_ref):
    w = w_ref[0].astype(BF16)
    o_ref[0] = jnp.dot(s_ref[...], w, preferred_element_type=F32) + b_ref[0]


def _ada_modulation(s8, w_ada, b_ada):
    depth, d, n = w_ada.shape
    tn = _pick_tile(n, (1024, 512, 256, 128))
    return pl.pallas_call(
        _ada_kernel,
        out_shape=jax.ShapeDtypeStruct((depth, 8, n), F32),
        grid=(depth, n // tn),
        in_specs=[pl.BlockSpec((8, d), lambda l, j: (0, 0)),
                  pl.BlockSpec((1, d, tn), lambda l, j: (l, 0, j)),
                  pl.BlockSpec((1, 1, tn), lambda l, j: (l, 0, j))],
        out_specs=pl.BlockSpec((1, 8, tn), lambda l, j: (l, 0, j)),
        compiler_params=_cparams(("arbitrary", "arbitrary")),
        name="ada_modulation",
    )(s8, w_ada, b_ada.reshape(depth, 1, n))


def _rms(x):
    return x * lax.rsqrt(jnp.mean(x * x, axis=-1, keepdims=True) + NORM_EPS)


def _norm_mod_kernel(x_ref, g_ref, sh_ref, sc_ref, o_ref, *, n_ctx):
    i = pl.program_id(0)
    tm = x_ref.shape[0]
    y = _rms(x_ref[...]) * g_ref[...]
    y = y * (1.0 + _mod_row(sc_ref, i, tm, n_ctx)) + _mod_row(sh_ref, i, tm, n_ctx)
    o_ref[...] = y.astype(o_ref.dtype)


def _norm_mod(x, g, mod, k_shift, k_scale, n_ctx):
    t, d = x.shape
    tm = ROW_TILE
    return pl.pallas_call(
        functools.partial(_norm_mod_kernel, n_ctx=n_ctx),
        out_shape=jax.ShapeDtypeStruct((t, d), BF16),
        grid=(t // tm,),
        in_specs=[pl.BlockSpec((tm, d), lambda i: (i, 0)),
                  pl.BlockSpec((1, d), lambda i: (0, 0)),
                  pl.BlockSpec((8, d), lambda i: (0, k_shift)),
                  pl.BlockSpec((8, d), lambda i: (0, k_scale))],
        out_specs=pl.BlockSpec((tm, d), lambda i: (i, 0)),
        compiler_params=_cparams(("arbitrary",)),
        name="norm_mod",
    )(x, g.reshape(1, d), mod, mod)


def _cast_weight_once(w_ref, wbf_ref):
    @pl.when(pl.program_id(1) == 0)
    def _():
        wbf_ref[...] = w_ref[...].astype(BF16)


def _mm_plain_kernel(a_ref, w_ref, o_ref, wbf_ref, *, scale):
    _cast_weight_once(w_ref, wbf_ref)
    acc = jnp.dot(a_ref[...], wbf_ref[...], preferred_element_type=F32)
    if scale != 1.0:
        acc = acc * scale
    o_ref[...] = acc.astype(o_ref.dtype)


def _mm_residual_kernel(a_ref, w_ref, x_ref, mod_ref, o_ref, wbf_ref, *, n_ctx):
    _cast_weight_once(w_ref, wbf_ref)
    tm = a_ref.shape[0]
    acc = jnp.dot(a_ref[...], wbf_ref[...], preferred_element_type=F32)
    row = pl.program_id(1) * tm + lax.broadcasted_iota(I32, (tm, 1), 0)
    gate = jnp.where(row < n_ctx, mod_ref[1:2, :], mod_ref[0:1, :])
    o_ref[...] = x_ref[...] + gate * acc


def _mm_rope_kernel(a_ref, w_ref, cf_ref, sf_ref, cb_ref, sb_ref, o_ref, wbf_ref, *, scale, head_dim):
    _cast_weight_once(w_ref, wbf_ref)
    acc = jnp.dot(a_ref[...], wbf_ref[...], preferred_element_type=F32)
    if scale != 1.0:
        acc = acc * scale
    half = head_dim // 2
    for d, (c_ref, s_ref) in enumerate(((cf_ref, sf_ref), (cb_ref, sb_ref))):
        c, s = c_ref[...], s_ref[...]
        for h in range(acc.shape[1] // head_dim):
            x1 = acc[:, h * head_dim:h * head_dim + half]
            x2 = acc[:, h * head_dim + half:(h + 1) * head_dim]
            o_ref[d, :, h * head_dim:h * head_dim + half] = (x1 * c - x2 * s).astype(o_ref.dtype)
            o_ref[d, :, h * head_dim + half:(h + 1) * head_dim] = (x1 * s + x2 * c).astype(o_ref.dtype)


def _mm_tiles(m, k, n):
    tm = _pick_tile(m, (768, 512, 384, 256, 128))
    tn_cap = 1024 if k <= 2048 else 512
    tn = _pick_tile(n, tuple(c for c in (1024, 512, 256, 128) if c <= tn_cap))
    return tm, tn


def _matmul(a, w, layer, col0, n, out_dtype, scale=1.0):
    m, k = a.shape
    tm, tn = _mm_tiles(m, k, n)
    assert col0 % tn == 0
    c0 = col0 // tn
    return pl.pallas_call(
        functools.partial(_mm_plain_kernel, scale=scale),
        out_shape=jax.ShapeDtypeStruct((m, n), out_dtype),
        grid=(n // tn, m // tm),
        in_specs=[pl.BlockSpec((tm, k), lambda j, i: (i, 0)),
                  pl.BlockSpec((None, k, tn), lambda j, i: (layer, 0, j + c0))],
        out_specs=pl.BlockSpec((tm, tn), lambda j, i: (i, j)),
        scratch_shapes=[pltpu.VMEM((k, tn), BF16)],
        compiler_params=_cparams(("arbitrary", "arbitrary")),
        name="matmul",
    )(a, w)


def _matmul_residual(a, w, layer, x, mod, k_gate, n_ctx):
    m, k = a.shape
    n = w.shape[2]
    tm, tn = _mm_tiles(m, k, n)
    kb = k_gate * (n // tn)
    return pl.pallas_call(
        functools.partial(_mm_residual_kernel, n_ctx=n_ctx),
        out_shape=jax.ShapeDtypeStruct((m, n), F32),
        grid=(n // tn, m // tm),
        in_specs=[pl.BlockSpec((tm, k), lambda j, i: (i, 0)),
                  pl.BlockSpec((None, k, tn), lambda j, i: (layer, 0, j)),
                  pl.BlockSpec((tm, tn), lambda j, i: (i, j)),
                  pl.BlockSpec((8, tn), lambda j, i: (0, kb + j))],
        out_specs=pl.BlockSpec((tm, tn), lambda j, i: (i, j)),
        scratch_shapes=[pltpu.VMEM((k, tn), BF16)],
        compiler_params=_cparams(("arbitrary", "arbitrary")),
        name="matmul_residual",
    )(a, w, x, mod)


def _matmul_rope(a, w, layer, col0, n, tables, scale, head_dim):
    m, k = a.shape
    tm, tn = _mm_tiles(m, k, n)
    assert col0 % tn == 0 and tn % head_dim == 0
    c0 = col0 // tn
    half = head_dim // 2
    tab_spec = pl.BlockSpec((tm, half), lambda j, i: (i, 0))
    return pl.pallas_call(
        functools.partial(_mm_rope_kernel, scale=scale, head_dim=head_dim),
        out_shape=jax.ShapeDtypeStruct((2, m, n), BF16),
        grid=(n // tn, m // tm),
        in_specs=[pl.BlockSpec((tm, k), lambda j, i: (i, 0)),
                  pl.BlockSpec((None, k, tn), lambda j, i: (layer, 0, j + c0)),
                  tab_spec, tab_spec, tab_spec, tab_spec],
        out_specs=pl.BlockSpec((2, tm, tn), lambda j, i: (0, i, j)),
        scratch_shapes=[pltpu.VMEM((k, tn), BF16)],
        compiler_params=_cparams(("arbitrary", "arbitrary")),
        name="matmul_rope",
    )(a, w, *tables)


def _chunk_index(c, n_chunks, ctx_chunks, reverse):
    if not reverse:
        return c
    return jnp.where(c < ctx_chunks, ctx_chunks - 1 - c, n_chunks - 1 - (c - ctx_chunks))


def _dot_nt(a, b):
    return lax.dot_general(a, b, (((1,), (1,)), ((), ())), preferred_element_type=F32)


def _dot_tn(a, b):
    return lax.dot_general(a, b, (((0,), (0,)), ((), ())), preferred_element_type=F32)


def _retention_kernel(lg_ref, qf_ref, kf_ref, vf_ref, qb_ref, kb_ref, vb_ref, of_ref, ob_ref, sf_ref, sb_ref,
                      *, heads, dk, dv):
    c = pl.program_id(0)

    @pl.when(c == 0)
    def _():
        sf_ref[...] = jnp.zeros_like(sf_ref)
        sb_ref[...] = jnp.zeros_like(sb_ref)

    n = qf_ref.shape[0]
    ri = lax.broadcasted_iota(I32, (n, n), 0)
    ci = lax.broadcasted_iota(I32, (n, n), 1)
    r1 = lax.broadcasted_iota(I32, (n, 1), 0)
    streams = []
    for reverse, refs in ((False, (qf_ref, kf_ref, vf_ref, of_ref, sf_ref)),
                          (True, (qb_ref, kb_ref, vb_ref, ob_ref, sb_ref))):
        diff = (ci - ri) if reverse else (ri - ci)
        dmask = diff >= 0
        dist = jnp.where(dmask, diff, 0).astype(F32)
        pos = ((n - 1 - r1) if reverse else r1).astype(F32)
        streams.append((dmask, dist, pos) + refs)
    for h in range(heads):
        for d, (dmask, dist, pos, q_ref, k_ref, v_ref, o_ref, s_ref) in enumerate(streams):
            lg = jnp.full((1, 1), lg_ref[d * heads + h], F32)
            decay = jnp.where(dmask, jnp.exp(dist * lg), 0.0)
            q_dec = jnp.exp((pos + 1.0) * lg)
            k_dec = jnp.exp((n - 1.0 - pos) * lg)
            c_dec = jnp.exp(float(n) * lg)
            q = q_ref[:, h * dk:(h + 1) * dk]
            k = k_ref[:, h * dk:(h + 1) * dk]
            v = v_ref[:, h * dv:(h + 1) * dv]
            s_prev = s_ref[h]
            scores = (_dot_nt(q, k) * decay).astype(BF16)
            inner = jnp.dot(scores, v, preferred_element_type=F32)
            cross = jnp.dot(q, s_prev.astype(BF16), preferred_element_type=F32)
            o_ref[:, h * dv:(h + 1) * dv] = (inner + q_dec * cross).astype(o_ref.dtype)
            kd = (k.astype(F32) * k_dec).astype(BF16)
            s_ref[h] = c_dec * s_prev + _dot_tn(kd, v)


def _retention_scan(log_gamma, q, k, vg, n_ctx, heads, dk, dv):
    t = q.shape[1]
    nc, cc = t // CHUNK, n_ctx // CHUNK
    cm = lambda c: _chunk_index(c, nc, cc, True)
    qk_spec = lambda d, idx: pl.BlockSpec((None, CHUNK, heads * dk), lambda c: (d, idx(c), 0))
    v_spec = lambda idx: pl.BlockSpec((CHUNK, heads * dv), lambda c: (idx(c), 0))
    fwd = lambda c: c
    out = jax.ShapeDtypeStruct((t, heads * dv), BF16)
    return pl.pallas_call(
        functools.partial(_retention_kernel, heads=heads, dk=dk, dv=dv),
        out_shape=(out, out),
        grid=(nc,),
        in_specs=[pl.BlockSpec(memory_space=pltpu.SMEM),
                  qk_spec(0, fwd), qk_spec(0, fwd), v_spec(fwd),
                  qk_spec(1, cm), qk_spec(1, cm), v_spec(cm)],
        out_specs=(v_spec(fwd), v_spec(cm)),
        scratch_shapes=[pltpu.VMEM((heads, dk, dv), F32), pltpu.VMEM((heads, dk, dv), F32)],
        compiler_params=_cparams(("arbitrary",)),
        name="retention_scan",
    )(log_gamma.reshape(-1), q, k, vg, q, k, vg)


def _ret_post_kernel(yf_ref, yb_ref, g_ref, w_ref, b_ref, o_ref, *, heads, dv):
    for h in range(heads):
        sl = slice(h * dv, (h + 1) * dv)
        y = yf_ref[:, sl].astype(F32) + yb_ref[:, sl].astype(F32)
        mu = jnp.mean(y, axis=-1, keepdims=True)
        yc = y - mu
        var = jnp.mean(yc * yc, axis=-1, keepdims=True)
        yn = yc * lax.rsqrt(var + NORM_EPS)
        g = g_ref[:, sl].astype(F32)
        silu = g * (1.0 / (1.0 + jnp.exp(-g)))
        o_ref[:, sl] = ((yn * w_ref[:, sl] + b_ref[:, sl]) * silu).astype(o_ref.dtype)


def _ret_post(yf, yb, vg, gn_w, gn_b, heads, dv):
    t, vw = yf.shape
    tm = ROW_TILE
    return pl.pallas_call(
        functools.partial(_ret_post_kernel, heads=heads, dv=dv),
        out_shape=jax.ShapeDtypeStruct((t, vw), BF16),
        grid=(t // tm,),
        in_specs=[pl.BlockSpec((tm, vw), lambda i: (i, 0)),
                  pl.BlockSpec((tm, vw), lambda i: (i, 0)),
                  pl.BlockSpec((tm, vw), lambda i: (i, 1)),
                  pl.BlockSpec((1, vw), lambda i: (0, 0)),
                  pl.BlockSpec((1, vw), lambda i: (0, 0))],
        out_specs=pl.BlockSpec((tm, vw), lambda i: (i, 0)),
        compiler_params=_cparams(("arbitrary",)),
        name="retention_post",
    )(yf, yb, vg, gn_w.reshape(1, vw), gn_b.reshape(1, vw))


def _conv_silu_kernel(x_ref, prev_ref, next_ref, w_ref, o_ref, *, n_ctx, t_total, q_cols, q_scale):
    i = pl.program_id(0)
    tm = x_ref.shape[0]
    x = x_ref[...]
    row = lax.broadcasted_iota(I32, (tm, 1), 0)
    grow = i * tm + row
    halo = prev_ref.shape[0]
    x_prev = jnp.where(row == 0, prev_ref[halo - 1:halo, :], pltpu.roll(x, 1, axis=0))
    x_prev = jnp.where((grow == 0) | (grow == n_ctx), 0.0, x_prev)
    x_next = jnp.where(row == tm - 1, next_ref[0:1, :], pltpu.roll(x, tm - 1, axis=0))
    x_next = jnp.where((grow == n_ctx - 1) | (grow == t_total - 1), 0.0, x_next)
    y = x_prev * w_ref[0:1, :] + x * w_ref[1:2, :] + x_next * w_ref[2:3, :]
    y = y * (1.0 / (1.0 + jnp.exp(-y)))
    o_ref[:, :q_cols] = (y[:, :q_cols] * q_scale).astype(o_ref.dtype)
    o_ref[:, q_cols:] = y[:, q_cols:].astype(o_ref.dtype)


def _conv_silu(x, conv_w, n_ctx, q_cols, q_scale):
    t, c = x.shape
    tm = ROW_TILE
    halo = 8
    r = tm // halo
    last = t // halo - 1
    w8 = jnp.zeros((8, c), F32).at[:3].set(conv_w)
    return pl.pallas_call(
        functools.partial(_conv_silu_kernel, n_ctx=n_ctx, t_total=t, q_cols=q_cols, q_scale=q_scale),
        out_shape=jax.ShapeDtypeStruct((t, c), BF16),
        grid=(t // tm,),
        in_specs=[pl.BlockSpec((tm, c), lambda i: (i, 0)),
                  pl.BlockSpec((halo, c), lambda i: (jnp.maximum(i * r - 1, 0), 0)),
                  pl.BlockSpec((halo, c), lambda i: (jnp.minimum((i + 1) * r, last), 0)),
                  pl.BlockSpec((8, c), lambda i: (0, 0))],
        out_specs=pl.BlockSpec((tm, c), lambda i: (i, 0)),
        compiler_params=_cparams(("arbitrary",)),
        name="mlstm_conv_silu",
    )(x, x, x, w8)


def _mlstm_kernel(qk_ref, v_ref, gt_ref, bias_ref, o_ref, c_ref, n_ref, m_ref, *, reverse, heads, dk, dv, d_idx):
    c = pl.program_id(0)

    @pl.when(c == 0)
    def _():
        c_ref[...] = jnp.zeros_like(c_ref)
        n_ref[...] = jnp.zeros_like(n_ref)
        m_ref[...] = jnp.zeros_like(m_ref)

    n = qk_ref.shape[0]
    gates = gt_ref[...] + bias_ref[...]
    i_all = GATE_SOFTCAP * jnp.tanh(gates * (1.0 / GATE_SOFTCAP))
    f_all = -(jnp.maximum(-gates, 0.0) + jnp.log1p(jnp.exp(-jnp.abs(gates))))
    ri = lax.broadcasted_iota(I32, (n, n), 0)
    ci = lax.broadcasted_iota(I32, (n, n), 1)
    mask = (ci >= ri) if reverse else (ci <= ri)
    tri = jnp.where(mask, 1.0, 0.0).astype(F32)
    b_all = jnp.dot(tri, f_all, preferred_element_type=F32, precision=lax.Precision.HIGHEST)
    b_all_t = b_all.T
    i_all_t = i_all.T
    last = 0 if reverse else n - 1
    qk_w = heads * dk
    for h in range(heads):
        ic = d_idx * 2 * heads + h
        fc = ic + heads
        b_col = b_all[:, fc:fc + 1]
        b_row = b_all_t[fc:fc + 1, :]
        i_col = i_all[:, ic:ic + 1]
        i_row = i_all_t[ic:ic + 1, :]
        m_prev = m_ref[h][:, 0:1]
        q = qk_ref[:, h * dk:(h + 1) * dk]
        k = qk_ref[:, qk_w + h * dk:qk_w + (h + 1) * dk]
        v = v_ref[:, h * dv:(h + 1) * dv]
        c_prev = c_ref[h]
        n_prev = n_ref[h]
        d_log = jnp.where(mask, b_col - b_row + i_row, NEG_INF)
        m_t = jnp.maximum(b_col + m_prev, jnp.max(d_log, axis=-1, keepdims=True))
        w = jnp.exp(d_log - m_t)
        s = _dot_nt(q, k) * w
        inter = jnp.exp(b_col + m_prev - m_t)
        num = jnp.dot(s.astype(BF16), v, preferred_element_type=F32) + inter * jnp.dot(
            q, c_prev.astype(BF16), preferred_element_type=F32)
        den = jnp.sum(s, axis=-1, keepdims=True) + inter * jnp.sum(q.astype(F32) * n_prev, axis=-1, keepdims=True)
        o_ref[:, h * dv:(h + 1) * dv] = (num / jnp.maximum(jnp.abs(den), jnp.exp(-m_t))).astype(o_ref.dtype)
        b_last = b_col[last:last + 1, :]
        g = b_last - b_col + i_col
        m_new = jnp.maximum(b_last + m_prev, jnp.max(g, axis=0, keepdims=True))
        wk = jnp.exp(g - m_new) * k.astype(F32)
        dec = jnp.exp(b_last + m_prev - m_new)
        c_ref[h] = dec * c_prev + _dot_tn(wk.astype(BF16), v)
        n_ref[h] = dec * n_prev + jnp.sum(wk, axis=0, keepdims=True)
        m_ref[h] = jnp.broadcast_to(m_new, m_ref.shape[1:])


def _mlstm_scan(qk, vo, gates, bias, n_ctx, heads, dk, dv, reverse):
    t = qk.shape[0]
    nc, cc = t // CHUNK, n_ctx // CHUNK
    cm = lambda c: _chunk_index(c, nc, cc, reverse)
    return pl.pallas_call(
        functools.partial(_mlstm_kernel, reverse=reverse, heads=heads, dk=dk, dv=dv, d_idx=1 if reverse else 0),
        out_shape=jax.ShapeDtypeStruct((t, heads * dv), BF16),
        grid=(nc,),
        in_specs=[pl.BlockSpec((CHUNK, 2 * heads * dk), lambda c: (cm(c), 0)),
                  pl.BlockSpec((CHUNK, heads * dv), lambda c: (cm(c), 0)),
                  pl.BlockSpec((CHUNK, LANES), lambda c: (cm(c), 0)),
                  pl.BlockSpec((1, LANES), lambda c: (0, 0))],
        out_specs=pl.BlockSpec((CHUNK, heads * dv), lambda c: (cm(c), 0)),
        scratch_shapes=[pltpu.VMEM((heads, dk, dv), F32),
                        pltpu.VMEM((heads, 1, dk), F32),
                        pltpu.VMEM((heads, 1, LANES), F32)],
        compiler_params=_cparams(("arbitrary",)),
        name="mlstm_scan_rev" if reverse else "mlstm_scan_fwd",
    )(qk, vo, gates, bias)


def _mlstm_post_kernel(yf_ref, yb_ref, o_ref_in, w_ref, o_ref, *, heads, dv):
    for h in range(heads):
        sl = slice(h * dv, (h + 1) * dv)
        y = _rms(yf_ref[:, sl].astype(F32) + yb_ref[:, sl].astype(F32))
        o = o_ref_in[:, sl].astype(F32)
        o_ref[:, sl] = (y * w_ref[:, sl] * (1.0 / (1.0 + jnp.exp(-o)))).astype(o_ref.dtype)


def _mlstm_post(yf, yb, vo, norm_w, heads, dv):
    t, vw = yf.shape
    tm = ROW_TILE
    return pl.pallas_call(
        functools.partial(_mlstm_post_kernel, heads=heads, dv=dv),
        out_shape=jax.ShapeDtypeStruct((t, vw), BF16),
        grid=(t // tm,),
        in_specs=[pl.BlockSpec((tm, vw), lambda i: (i, 0)),
                  pl.BlockSpec((tm, vw), lambda i: (i, 0)),
                  pl.BlockSpec((tm, vw), lambda i: (i, 1)),
                  pl.BlockSpec((1, vw), lambda i: (0, 0))],
        out_specs=pl.BlockSpec((tm, vw), lambda i: (i, 0)),
        compiler_params=_cparams(("arbitrary",)),
        name="mlstm_post",
    )(yf, yb, vo, norm_w.reshape(1, vw))


def _qk_norm_rope_kernel(x_ref, cos_ref, sin_ref, qn_ref, kn_ref, q_ref, k_ref, *, q_heads, k_heads, hd):
    cos, sin = cos_ref[...], sin_ref[...]
    lane = lax.broadcasted_iota(I32, (1, hd), 1)
    first = (lane % (hd // 2)) < (hd // 4)
    for h in range(q_heads + k_heads):
        x = x_ref[:, h * hd:(h + 1) * hd]
        w = qn_ref[...] if h < q_heads else kn_ref[...]
        xh = _rms(x) * w
        partner = jnp.where(first, pltpu.roll(xh, hd - hd // 4, axis=1), pltpu.roll(xh, hd // 4, axis=1))
        y = (xh * cos + partner * sin).astype(q_ref.dtype)
        if h < q_heads:
            q_ref[:, h * hd:(h + 1) * hd] = y
        else:
            k_ref[:, (h - q_heads) * hd:(h - q_heads + 1) * hd] = y


def _qk_norm_rope(qk, cos, sin, q_norm, k_norm, q_heads, k_heads, hd):
    t = qk.shape[0]
    tm = ROW_TILE
    return pl.pallas_call(
        functools.partial(_qk_norm_rope_kernel, q_heads=q_heads, k_heads=k_heads, hd=hd),
        out_shape=(jax.ShapeDtypeStruct((t, q_heads * hd), BF16), jax.ShapeDtypeStruct((t, k_heads * hd), BF16)),
        grid=(t // tm,),
        in_specs=[pl.BlockSpec((tm, (q_heads + k_heads) * hd), lambda i: (i, 0)),
                  pl.BlockSpec((tm, hd), lambda i: (i, 0)),
                  pl.BlockSpec((tm, hd), lambda i: (i, 0)),
                  pl.BlockSpec((1, hd), lambda i: (0, 0)),
                  pl.BlockSpec((1, hd), lambda i: (0, 0))],
        out_specs=(pl.BlockSpec((tm, q_heads * hd), lambda i: (i, 0)),
                   pl.BlockSpec((tm, k_heads * hd), lambda i: (i, 0))),
        compiler_params=_cparams(("arbitrary",)),
        name="attn_qk_norm_rope",
    )(qk, cos, sin, q_norm.reshape(1, hd), k_norm.reshape(1, hd))


def _flash_kernel(q_ref, k_ref, v_ref, o_ref, m_ref, l_ref, acc_ref, alpha_ref, sa_ref, sb_ref, p_ref,
                  *, groups, hd, tk, n_ctx):
    tq = q_ref.shape[0]
    n_kv = k_ref.shape[0]
    n = n_kv // tk
    q = jnp.concatenate([q_ref[:, g * hd:(g + 1) * hd] for g in range(groups)], axis=0)
    m_ref[...] = jnp.full_like(m_ref, NEG_INF)
    l_ref[...] = jnp.zeros_like(l_ref)
    acc_ref[...] = jnp.zeros_like(acc_ref)

    def scores(off, size):
        return _dot_nt(q, k_ref[pl.ds(off, size), :])

    def absorb(s_ref, off, size):
        for band in range(groups * tq // FLASH_BAND):
            rows = slice(band * FLASH_BAND, (band + 1) * FLASH_BAND)
            s = s_ref[rows, :size]
            m_prev = m_ref[rows, :]
            m_new = jnp.maximum(m_prev, jnp.max(s, axis=-1, keepdims=True))
            alpha = jnp.exp2(m_prev - m_new)
            p = jnp.exp2(s - jnp.concatenate([m_new] * (size // LANES), axis=1))
            l_ref[rows, :] = alpha * l_ref[rows, :] + jnp.sum(p, axis=-1, keepdims=True)
            m_ref[rows, :] = m_new
            alpha_ref[rows, :] = alpha
            p_ref[rows, :size] = p.astype(BF16)
        acc_ref[...] = alpha_ref[...] * acc_ref[...] + jnp.dot(p_ref[:, :size], v_ref[pl.ds(off, size), :],
                                                                preferred_element_type=F32)

    is_ctx = pl.program_id(1) < n_ctx // tq

    @pl.when(is_ctx)
    def _():
        sa_ref[:, :n_ctx] = scores(0, n_ctx)
        absorb(sa_ref, 0, n_ctx)

    @pl.when(jnp.logical_not(is_ctx))
    def _():
        bufs = (sa_ref, sb_ref)
        bufs[0][...] = scores(0, tk)
        for j in range(n):
            if j + 1 < n:
                bufs[(j + 1) % 2][...] = scores((j + 1) * tk, tk)
            absorb(bufs[j % 2], j * tk, tk)

    out = acc_ref[...] / l_ref[...]
    for g in range(groups):
        o_ref[:, g * hd:(g + 1) * hd] = out[g * tq:(g + 1) * tq].astype(o_ref.dtype)


def _flash_attention(q, k, v, n_ctx, kv_heads, groups, hd):
    t = q.shape[0]
    tq = _pick_tile(math.gcd(t, n_ctx), (256, 128))
    tk = _pick_tile(t, (768, 512, 256, 128))
    assert hd == LANES and n_ctx <= tk and (groups * tq) % FLASH_BAND == 0
    gw = groups * hd
    return pl.pallas_call(
        functools.partial(_flash_kernel, groups=groups, hd=hd, tk=tk, n_ctx=n_ctx),
        out_shape=jax.ShapeDtypeStruct((t, kv_heads * gw), BF16),
        grid=(kv_heads, t // tq),
        in_specs=[pl.BlockSpec((tq, gw), lambda g, i: (i, g)),
                  pl.BlockSpec((t, hd), lambda g, i: (0, g)),
                  pl.BlockSpec((t, hd), lambda g, i: (0, g))],
        out_specs=pl.BlockSpec((tq, gw), lambda g, i: (i, g)),
        scratch_shapes=[pltpu.VMEM((groups * tq, LANES), F32),
                        pltpu.VMEM((groups * tq, LANES), F32),
                        pltpu.VMEM((groups * tq, hd), F32),
                        pltpu.VMEM((groups * tq, LANES), F32),
                        pltpu.VMEM((groups * tq, tk), F32),
                        pltpu.VMEM((groups * tq, tk), F32),
                        pltpu.VMEM((groups * tq, tk), BF16)],
        compiler_params=_cparams(("arbitrary", "arbitrary")),
        name="flash_attention",
    )(q, k, v)


def _router_kernel(x_ref, g_ref, sh_ref, sc_ref, wr_ref, br_ref, h_ref, ids_ref, wts_ref, cnt_ref, run_ref,
                   *, n_ctx, row0):
    i = pl.program_id(0)
    tm, d = x_ref.shape

    @pl.when(i == 0)
    def _():
        run_ref[...] = jnp.zeros_like(run_ref)

    ib = i + row0 // tm
    h = _rms(x_ref[...]) * g_ref[...]
    h = h * (1.0 + _mod_row(sc_ref, ib, tm, n_ctx)) + _mod_row(sh_ref, ib, tm, n_ctx)
    h_ref[...] = h
    h_hi = h.astype(BF16)
    h_lo = (h - h_hi.astype(F32)).astype(BF16)
    hw = jnp.dot(h_hi, wr_ref[...], preferred_element_type=F32)
    lw = jnp.dot(h_lo, wr_ref[:, :LANES], preferred_element_type=F32)
    logits = hw[:, :LANES] + (hw[:, LANES:] + lw) + br_ref[...]
    lane = lax.broadcasted_iota(I32, (tm, LANES), 1)
    big = jnp.int32(LANES)
    gl = jnp.where((lane >= MOE_EXPERTS) & (lane < MOE_EXPERTS + MOE_GROUPS), logits, NEG_INF)
    g_max = jnp.max(gl, axis=-1, keepdims=True)
    g_idx = jnp.min(jnp.where(gl == g_max, lane, big), axis=-1, keepdims=True) - MOE_EXPERTS
    g_w = 1.0 / jnp.sum(jnp.exp(gl - g_max), axis=-1, keepdims=True)
    el = jnp.where((lane < MOE_EXPERTS) & ((lane >> 3) == g_idx), logits, NEG_INF)
    m1 = jnp.max(el, axis=-1, keepdims=True)
    i1 = jnp.min(jnp.where(el == m1, lane, big), axis=-1, keepdims=True)
    el2 = jnp.where(lane == i1, NEG_INF, el)
    m2 = jnp.max(el2, axis=-1, keepdims=True)
    i2 = jnp.min(jnp.where(el2 == m2, lane, big), axis=-1, keepdims=True)
    e_sum = jnp.sum(jnp.exp(el - m1), axis=-1, keepdims=True)
    p1 = 1.0 / e_sum
    p2 = jnp.exp(m2 - m1) / e_sum
    w1 = g_w * (p1 / (p1 + p2))
    w2 = g_w * (p2 / (p1 + p2))
    onehot = jnp.where((lane == i1) | (lane == i2), 1.0, 0.0)
    ri = lax.broadcasted_iota(I32, (tm, tm), 0)
    ci = lax.broadcasted_iota(I32, (tm, tm), 1)
    strict = jnp.where(ci < ri, 1.0, 0.0).astype(BF16)
    before = jnp.dot(strict, onehot.astype(BF16), preferred_element_type=F32) + run_ref[0:1, :]
    r1 = jnp.sum(jnp.where(lane == i1, before, 0.0), axis=-1, keepdims=True).astype(I32)
    r2 = jnp.sum(jnp.where(lane == i2, before, 0.0), axis=-1, keepdims=True).astype(I32)
    ids_ref[...] = jnp.where(lane == 0, i1, jnp.where(lane == 1, i2, jnp.where(lane == 2, r1, jnp.where(lane == 3, r2, 0))))
    wts_ref[...] = jnp.where(lane == 0, w1, jnp.where(lane == 1, w2, 0.0))
    total = run_ref[0:1, :] + jnp.sum(onehot, axis=0, keepdims=True)
    run_ref[...] = jnp.broadcast_to(total, run_ref.shape)
    cnt_ref[...] = jnp.broadcast_to(total, cnt_ref.shape)


def _router(x, g, mod, k_shift, k_scale, w_router, b_router, n_ctx, row0):
    t, d = x.shape
    tm = ROW_TILE
    n = t - row0
    off = row0 // tm
    return pl.pallas_call(
        functools.partial(_router_kernel, n_ctx=n_ctx, row0=row0),
        out_shape=(jax.ShapeDtypeStruct((n, d), F32),
                   jax.ShapeDtypeStruct((n, LANES), I32),
                   jax.ShapeDtypeStruct((n, LANES), F32),
                   jax.ShapeDtypeStruct((8, LANES), F32)),
        grid=(n // tm,),
        in_specs=[pl.BlockSpec((tm, d), lambda i: (i + off, 0)),
                  pl.BlockSpec((1, d), lambda i: (0, 0)),
                  pl.BlockSpec((8, d), lambda i: (0, k_shift)),
                  pl.BlockSpec((8, d), lambda i: (0, k_scale)),
                  pl.BlockSpec((d, 2 * LANES), lambda i: (0, 0)),
                  pl.BlockSpec((1, LANES), lambda i: (0, 0))],
        out_specs=(pl.BlockSpec((tm, d), lambda i: (i, 0)),
                   pl.BlockSpec((tm, LANES), lambda i: (i, 0)),
                   pl.BlockSpec((tm, LANES), lambda i: (i, 0)),
                   pl.BlockSpec((8, LANES), lambda i: (0, 0))),
        scratch_shapes=[pltpu.VMEM((8, LANES), F32)],
        compiler_params=_cparams(("arbitrary",)),
        name="moe_router",
    )(x, g.reshape(1, d), mod, mod, w_router, b_router)


def _slot_index_kernel(ids_ref, seg_ref, o_ref):
    ids = ids_ref[...]
    lane = lax.broadcasted_iota(I32, ids.shape, 1)
    seg = seg_ref[...]
    slots = []
    for k in range(MOE_TOPK):
        start = jnp.sum(jnp.where(lane == ids[:, k:k + 1], seg, 0.0), axis=-1, keepdims=True)
        slots.append(start.astype(I32) + ids[:, MOE_TOPK + k:MOE_TOPK + k + 1])
    o_ref[...] = jnp.where(lane == 0, slots[0], jnp.where(lane == 1, slots[1], 0))


def _slot_index(ids, seg_start):
    n = ids.shape[0]
    tm = ROW_TILE
    seg = jnp.zeros((1, LANES), F32).at[0, :MOE_EXPERTS].set(seg_start.astype(F32))
    out = pl.pallas_call(
        _slot_index_kernel,
        out_shape=jax.ShapeDtypeStruct((n, LANES), I32),
        grid=(n // tm,),
        in_specs=[pl.BlockSpec((tm, LANES), lambda i: (i, 0)),
                  pl.BlockSpec((1, LANES), lambda i: (0, 0))],
        out_specs=pl.BlockSpec((tm, LANES), lambda i: (i, 0)),
        compiler_params=_cparams(("arbitrary",)),
        name="moe_slot_index",
    )(ids, seg)
    return out[:, :MOE_TOPK].reshape(-1)


def _dispatch_kernel(dest_ref, h_hbm, xs_in_ref, xs_ref, hbuf, sem_in, sem_out):
    del xs_in_ref
    i = pl.program_id(0)
    n_steps = pl.num_programs(0)
    tm = hbuf.shape[1]
    cur = i % 2

    def load(block, half):
        return pltpu.make_async_copy(h_hbm.at[pl.ds(block * tm, tm)], hbuf.at[half], sem_in.at[half])

    def row_copy(block, half, r, k):
        slot = dest_ref[(block * tm + r) * MOE_TOPK + k]
        return pltpu.make_async_copy(hbuf.at[half, pl.ds(r, 1)], xs_ref.at[pl.ds(slot, 1)], sem_out.at[half])

    def for_rows(fn):
        def body(r, carry):
            for k in range(MOE_TOPK):
                fn(r, k)
            return carry
        lax.fori_loop(0, tm, body, 0, unroll=8)

    @pl.when(i == 0)
    def _():
        load(0, 0).start()

    @pl.when(i >= 1)
    def _():
        for_rows(lambda r, k: row_copy(i - 1, 1 - cur, r, k).wait())

    @pl.when(i + 1 < n_steps)
    def _():
        load(i + 1, 1 - cur).start()

    load(i, cur).wait()
    for_rows(lambda r, k: row_copy(i, cur, r, k).start())

    @pl.when(i == n_steps - 1)
    def _():
        for_rows(lambda r, k: row_copy(i, cur, r, k).wait())


def _dispatch(dest, h, xs_init):
    n, w = h.shape
    tm = ROW_TILE
    return pl.pallas_call(
        _dispatch_kernel,
        out_shape=jax.ShapeDtypeStruct(xs_init.shape, xs_init.dtype),
        grid_spec=pltpu.PrefetchScalarGridSpec(
            num_scalar_prefetch=1,
            grid=(n // tm,),
            in_specs=[pl.BlockSpec(memory_space=pl.ANY),
                      pl.BlockSpec(memory_space=pl.ANY)],
            out_specs=pl.BlockSpec(memory_space=pl.ANY),
            scratch_shapes=[pltpu.VMEM((2, tm, w), h.dtype), pltpu.SemaphoreType.DMA((2,)),
                            pltpu.SemaphoreType.DMA((2,))]),
        input_output_aliases={2: 0},
        compiler_params=_cparams(("arbitrary",)),
        name="moe_dispatch",
    )(dest, h, xs_init)


def _expert_kernel(be_ref, first_ref, next_ref, nb_ref, xs_ref, wg_hbm, wu_hbm, wd_hbm, ys_ref,
                   wg_st, wu_st, wd_st, wg_bf, wu_bf, wd_bf, sem, *, layer):
    b = pl.program_id(0)
    used = b < nb_ref[0]

    weights = ((wg_hbm, wg_st, wg_bf), (wu_hbm, wu_st, wu_bf), (wd_hbm, wd_st, wd_bf))

    def fetch(e, idx):
        hbm, st, _ = weights[idx]
        return pltpu.make_async_copy(hbm.at[layer, e], st, sem.at[idx])

    @pl.when(b == 0)
    def _():
        for idx in range(len(weights)):
            fetch(be_ref[0], idx).start()

    @pl.when(used & (first_ref[b] == 1))
    def _():
        has_next = next_ref[b] >= 0
        for idx, (_, st, bf) in enumerate(weights):
            fetch(be_ref[b], idx).wait()
            bf[...] = st[...].astype(BF16)

            @pl.when(has_next)
            def _():
                fetch(next_ref[b], idx).start()

    @pl.when(used)
    def _():
        x = xs_ref[...].astype(BF16)
        gate = jnp.dot(x, wg_bf[...], preferred_element_type=F32)
        up = jnp.dot(x, wu_bf[...], preferred_element_type=F32)
        act = (gate * (1.0 / (1.0 + jnp.exp(-gate))) * up).astype(BF16)
        ys_ref[...] = jnp.dot(act, wd_bf[...], preferred_element_type=F32)

    @pl.when(jnp.logical_not(used))
    def _():
        ys_ref[...] = jnp.zeros_like(ys_ref)


def _experts(block_expert, first, next_expert, n_used, xs, w_gate, w_up, w_down, layer):
    n_slots, d = xs.shape
    ff = w_gate.shape[3]
    nb = n_slots // MOE_BLOCK
    row_spec = pl.BlockSpec((MOE_BLOCK, d), lambda b, *_: (b, 0))
    hbm = pl.BlockSpec(memory_space=pl.ANY)
    return pl.pallas_call(
        functools.partial(_expert_kernel, layer=layer),
        out_shape=jax.ShapeDtypeStruct((n_slots, d), F32),
        grid_spec=pltpu.PrefetchScalarGridSpec(
            num_scalar_prefetch=4,
            grid=(nb,),
            in_specs=[row_spec, hbm, hbm, hbm],
            out_specs=row_spec,
            scratch_shapes=[pltpu.VMEM((d, ff), F32), pltpu.VMEM((d, ff), F32), pltpu.VMEM((ff, d), F32),
                            pltpu.VMEM((d, ff), BF16), pltpu.VMEM((d, ff), BF16), pltpu.VMEM((ff, d), BF16),
                            pltpu.SemaphoreType.DMA((3,))]),
        compiler_params=_cparams(("arbitrary",)),
        name="moe_experts",
    )(block_expert, first, next_expert, n_used, xs, w_gate, w_up, w_down)


def _combine_kernel(dest_ref, x_ref, wts_ref, mod_ref, ys_ref, g_ref, sh_ref, sc_ref, *rest, n_ctx, row0, last):
    if last:
        o_ref, buf, sem = rest
    else:
        o_ref, h_ref, buf, sem = rest
    i = pl.program_id(0)
    tm = x_ref.shape[0]
    cur = i % 2

    def row_copy(block, half, r, k):
        slot = dest_ref[(block * tm + r) * MOE_TOPK + k]
        return pltpu.make_async_copy(ys_ref.at[pl.ds(slot, 1)], buf.at[half, k, pl.ds(r, 1)], sem.at[half])

    def for_rows(fn):
        def body(r, carry):
            for k in range(MOE_TOPK):
                fn(r, k)
            return carry
        lax.fori_loop(0, tm, body, 0, unroll=8)

    @pl.when(i == 0)
    def _():
        for_rows(lambda r, k: row_copy(0, 0, r, k).start())

    @pl.when(i + 1 < pl.num_programs(0))
    def _():
        for_rows(lambda r, k: row_copy(i + 1, 1 - cur, r, k).start())

    for_rows(lambda r, k: row_copy(i, cur, r, k).wait())
    w = wts_ref[...]
    y = w[:, 0:1] * buf[cur, 0] + w[:, 1:2] * buf[cur, 1]
    ib = i + row0 // tm
    x_new = x_ref[...] + _mod_row(mod_ref, ib, tm, n_ctx) * y
    hn = _rms(x_new) * g_ref[...]
    if last:
        o_ref[...] = hn
    else:
        o_ref[...] = x_new
        hn = hn * (1.0 + _mod_row(sc_ref, ib, tm, n_ctx)) + _mod_row(sh_ref, ib, tm, n_ctx)
        h_ref[...] = hn.astype(h_ref.dtype)


def _combine(dest, x, wts, mod, k_gate, ys, n_ctx, row0, g_next, mod_next, last):
    t, d = x.shape
    tm = ROW_TILE
    n = t - row0
    off = row0 // tm
    row_spec = pl.BlockSpec((tm, d), lambda i, dest: (i, 0))
    out_shape = jax.ShapeDtypeStruct((n, d), F32)
    out_specs = row_spec
    if not last:
        out_shape = (out_shape, jax.ShapeDtypeStruct((n, d), BF16))
        out_specs = (row_spec, row_spec)
    return pl.pallas_call(
        functools.partial(_combine_kernel, n_ctx=n_ctx, row0=row0, last=last),
        out_shape=out_shape,
        grid_spec=pltpu.PrefetchScalarGridSpec(
            num_scalar_prefetch=1,
            grid=(n // tm,),
            in_specs=[pl.BlockSpec((tm, d), lambda i, dest: (i + off, 0)),
                      pl.BlockSpec((tm, LANES), lambda i, dest: (i, 0)),
                      pl.BlockSpec((8, d), lambda i, dest: (0, k_gate)),
                      pl.BlockSpec(memory_space=pl.ANY),
                      pl.BlockSpec((1, d), lambda i, dest: (0, 0)),
                      pl.BlockSpec((8, d), lambda i, dest: (0, 0)),
                      pl.BlockSpec((8, d), lambda i, dest: (0, 1))],
            out_specs=out_specs,
            scratch_shapes=[pltpu.VMEM((2, MOE_TOPK, tm, d), F32), pltpu.SemaphoreType.DMA((2,))]),
        compiler_params=_cparams(("arbitrary",)),
        name="moe_combine",
    )(dest, x, wts, mod, ys, g_next.reshape(1, d), mod_next, mod_next)


def _hier_moe(x, g, mod, w_rg, b_rg, w_re, b_re, w_gate, w_up, w_down, layer, n_ctx, row0, xs_buf,
              g_next, mod_next, last):
    t, d = x.shape
    n = t - row0
    pad = LANES - MOE_EXPERTS - MOE_GROUPS
    w_router = jnp.concatenate([w_re, w_rg, jnp.zeros((d, pad), F32)], axis=1)
    w_hi = w_router.astype(BF16)
    w_router = jnp.concatenate([w_hi, (w_router - w_hi.astype(F32)).astype(BF16)], axis=1)
    b_router = jnp.concatenate([b_re, b_rg, jnp.zeros((pad,), F32)]).reshape(1, LANES)
    h, ids, wts, counts = _router(x, g, mod, 3, 4, w_router, b_router, n_ctx, row0)
    counts = counts[0, :MOE_EXPERTS].astype(I32)
    padded = (counts + MOE_BLOCK - 1) // MOE_BLOCK * MOE_BLOCK
    seg_end = jnp.cumsum(padded)
    seg_start = seg_end - padded
    dest = _slot_index(ids, seg_start)
    n_blocks = (t * MOE_TOPK + MOE_EXPERTS * (MOE_BLOCK - 1)) // MOE_BLOCK + 1
    block_start = jnp.arange(n_blocks, dtype=I32) * MOE_BLOCK
    block_expert = jnp.minimum(jnp.sum((seg_end[None, :] <= block_start[:, None]).astype(I32), axis=1),
                               MOE_EXPERTS - 1).astype(I32)
    n_used = (seg_end[-1:] // MOE_BLOCK).astype(I32)
    block_id = jnp.arange(n_blocks, dtype=I32)
    prev_expert = jnp.concatenate([jnp.full((1,), -1, I32), block_expert[:-1]])
    first = ((block_id < n_used[0]) & (block_expert != prev_expert)).astype(I32)
    eid = jnp.arange(MOE_EXPERTS, dtype=I32)
    owner = jnp.where(padded > 0, eid, MOE_EXPERTS)
    later = jnp.concatenate([lax.cummin(owner[::-1])[::-1][1:], jnp.full((1,), MOE_EXPERTS, I32)])
    next_expert = jnp.where(later < MOE_EXPERTS, later, -1)[block_expert].astype(I32)
    if xs_buf is None:
        xs_buf = jnp.zeros((n_blocks * MOE_BLOCK, d), F32)
    xs = _dispatch(dest, h, xs_buf)
    ys = _experts(block_expert, first, next_expert, n_used, xs, w_gate, w_up, w_down, layer)
    return _combine(dest, x, wts, mod, 5, ys, n_ctx, row0, g_next, mod_next, last), xs


def _rope_table(pos, dim, base):
    inv = jnp.power(base, -jnp.arange(0, dim, 2, dtype=F32) / dim)
    ang = pos.astype(F32)[:, None] * inv[None, :]
    return jnp.cos(ang), jnp.sin(ang)


def _flipped_positions(t, n_ctx):
    idx = jnp.arange(t, dtype=I32)
    return jnp.where(idx < n_ctx, n_ctx - 1 - idx, n_ctx + (t - 1 - idx))


def _axial_tables(n_ctx, n_lat, hd):
    rows = n_lat // GRID_W
    row = jnp.repeat(jnp.arange(rows, dtype=I32), GRID_W)
    col = jnp.tile(jnp.arange(GRID_W, dtype=I32), rows)
    half = hd // 2
    cr, sr = _rope_table(row, half, ROPE_BASE)
    cc, sc = _rope_table(col, half, ROPE_BASE)
    cos = jnp.concatenate([cr, cr, cc, cc], axis=1)
    sin = jnp.concatenate([-sr, sr, -sc, sc], axis=1)
    cos = jnp.concatenate([jnp.ones((n_ctx, hd), F32), cos], axis=0)
    sin = jnp.concatenate([jnp.zeros((n_ctx, hd), F32), sin], axis=0)
    return cos, sin


def _retention_layer(xa, h, mod, w_in, logit_gamma, gn_w, gn_b, w_out, j, n_ctx):
    t, d = xa.shape
    heads = RET_HEADS
    dk = d // heads
    dv = 2 * dk
    qk_w, v_w = heads * dk, heads * dv
    pos_f = jnp.arange(t, dtype=I32)
    cf, sf = _rope_table(pos_f, dk, RET_ROPE_BASE)
    cb, sb = _rope_table(_flipped_positions(t, n_ctx), dk, RET_ROPE_BASE)
    tables = (cf, sf, cb, sb)
    q = _matmul_rope(h, w_in, j, 0, qk_w, tables, dk ** -0.5, dk)
    k = _matmul_rope(h, w_in, j, qk_w, qk_w, tables, 1.0, dk)
    vg = _matmul(h, w_in, j, 2 * qk_w, 2 * v_w, BF16)
    log_gamma = jax.nn.log_sigmoid(logit_gamma.astype(F32))
    yf, yb = _retention_scan(log_gamma, q, k, vg, n_ctx, heads, dk, dv)
    yn = _ret_post(yf, yb, vg, gn_w, gn_b, heads, dv)
    return _matmul_residual(yn, w_out, j, xa, mod, 2, n_ctx)


def _mlstm_layer(xa, h, mod, w_in, conv_w, b_gate, norm_w, w_out, j, n_ctx):
    t, d = xa.shape
    heads = MLSTM_HEADS
    dk = d // (2 * heads)
    dv = d // heads
    qk_w, v_w = heads * dk, heads * dv
    n_gates = 4 * heads
    qk_pre = _matmul(h, w_in, j, 0, 2 * qk_w, F32)
    vo = _matmul(h, w_in, j, 2 * qk_w, 2 * v_w, BF16)
    w_gates = jnp.pad(w_in[j, :, 2 * qk_w + 2 * v_w:], ((0, 0), (0, LANES - n_gates)))
    gates = _matmul(h, w_gates[None], 0, 0, LANES, F32)
    bias = jnp.pad(b_gate.astype(F32).reshape(1, n_gates), ((0, 0), (0, LANES - n_gates)))
    qk = _conv_silu(qk_pre, conv_w, n_ctx, qk_w, dk ** -0.5)
    yf = _mlstm_scan(qk, vo, gates, bias, n_ctx, heads, dk, dv, reverse=False)
    yb = _mlstm_scan(qk, vo, gates, bias, n_ctx, heads, dk, dv, reverse=True)
    yn = _mlstm_post(yf, yb, vo, norm_w, heads, dv)
    return _matmul_residual(yn, w_out, j, xa, mod, 2, n_ctx)


def _attention_layer(xa, h, mod, w_in, q_norm, k_norm, w_out, j, n_ctx):
    t, d = xa.shape
    hd = d // ATTN_HEADS
    groups = ATTN_HEADS // ATTN_KV_HEADS
    q_w, kv_w = ATTN_HEADS * hd, ATTN_KV_HEADS * hd
    n_lat = t - n_ctx
    qk_pre = _matmul(h, w_in, j, 0, q_w + kv_w, F32)
    v = _matmul(h, w_in, j, q_w + kv_w, kv_w, BF16)
    cos, sin = _axial_tables(n_ctx, n_lat, hd)
    q_gain = q_norm * (hd ** -0.5 * math.log2(math.e))
    q, k = _qk_norm_rope(qk_pre, cos, sin, q_gain, k_norm, ATTN_HEADS, ATTN_KV_HEADS, hd)
    o = _flash_attention(q, k, v, n_ctx, ATTN_KV_HEADS, groups, hd)
    return _matmul_residual(o, w_out, j, xa, mod, 2, n_ctx)


def kernel(x, c, ctx, c_ctx, w_ada, b_ada, norm_g, ret_w_in, ret_logit_gamma, ret_gn_w, ret_gn_b, ret_w_out, mlstm_w_in, mlstm_conv_w, mlstm_b_gate, mlstm_norm_w, mlstm_w_out, attn_w_in, attn_q_norm, attn_k_norm, attn_w_out, moe_w_router_group, moe_b_router_group, moe_w_router_expert, moe_b_router_expert, moe_w_gate, moe_w_up, moe_w_down, final_norm_g):
    bsz, n_lat, d = x.shape
    n_ctx = ctx.shape[1]
    depth = w_ada.shape[0]
    assert bsz == 1 and n_ctx % ROW_TILE == 0 and n_lat % ROW_TILE == 0 and n_lat % GRID_W == 0
    xa = jnp.concatenate([ctx[0], x[0]], axis=0)
    s = jnp.stack([jax.nn.silu(c[0]), jax.nn.silu(c_ctx)])
    s8 = jnp.zeros((8, d), F32).at[:2].set(s).astype(BF16)
    mods = _ada_modulation(s8, w_ada, b_ada)
    xs_buf = None
    h = _norm_mod(xa, norm_g[0, 0], mods[0], 0, 1, n_ctx)
    for i in range(depth):
        kind, j = i % 3, i // 3
        mod = mods[i]
        last = i == depth - 1
        if kind == 0:
            xa = _retention_layer(xa, h, mod, ret_w_in, ret_logit_gamma[j], ret_gn_w[j], ret_gn_b[j],
                                  ret_w_out, j, n_ctx)
        elif kind == 1:
            xa = _mlstm_layer(xa, h, mod, mlstm_w_in, mlstm_conv_w[j], mlstm_b_gate[j],
                              mlstm_norm_w[j], mlstm_w_out, j, n_ctx)
        else:
            xa = _attention_layer(xa, h, mod, attn_w_in, attn_q_norm[j], attn_k_norm[j],
                                  attn_w_out, j, n_ctx)
        row0 = n_ctx if last else 0
        g_next = final_norm_g if last else norm_g[i + 1, 0]
        mod_next = mod if last else mods[i + 1]
        out, xs_buf = _hier_moe(xa, norm_g[i, 1], mod, moe_w_router_group[i], moe_b_router_group[i],
                                moe_w_router_expert[i], moe_b_router_expert[i], moe_w_gate, moe_w_up, moe_w_down, i,
                                n_ctx, row0, xs_buf, g_next, mod_next, last)
        if last:
            return out[None]
        xa, h = out
```

```python
import functools
import math

import jax
import jax.numpy as jnp
import numpy as np
from jax import lax
from jax.experimental import pallas as pl
from jax.experimental.pallas import tpu as pltpu

F32 = jnp.float32
BF16 = jnp.bfloat16
I32 = jnp.int32
U32 = jnp.uint32

LANES = 128
CHUNK = 128
NORM_EPS = 1e-6
GRID_W = 64
RET_HEADS = 8
RET_ROPE_BASE = 10000.0
MLSTM_HEADS = 8
GATE_SOFTCAP = 15.0
ATTN_HEADS = 16
ATTN_KV_HEADS = 8
ROPE_BASE = 10000.0
MOE_GROUPS = 4
MOE_PER_GROUP = 8
MOE_EXPERTS = MOE_GROUPS * MOE_PER_GROUP
MOE_TOPK = 2
MOE_BLOCK = 256
ROW_TILE = 256
FLASH_BAND = 32
VMEM_LIMIT = 56 * 1024 * 1024
NEG_INF = float("-inf")


def _cparams(sem, vmem=VMEM_LIMIT):
    return pltpu.CompilerParams(dimension_semantics=sem, vmem_limit_bytes=vmem)


def _pick_tile(total, candidates):
    for c in candidates:
        if total % c == 0:
            return c
    raise ValueError(f"no tile for {total} in {candidates}")


def _mod_row(mod_ref, i, tile, n_ctx):
    sel = jnp.where(i < n_ctx // tile, 1, 0)
    return mod_ref[pl.ds(sel, 1), :]


def _ada_kernel(s_ref, w_ref, b_ref, o_ref):
    w = w_ref[0].astype(BF16)
    o_ref[0] = jnp.dot(s_ref[...], w, preferred_element_type=F32) + b_ref[0]


def _ada_modulation(s8, w_ada, b_ada):
    depth, d, n = w_ada.shape
    tn = _pick_tile(n, (1024, 512, 256, 128))
    return pl.pallas_call(
        _ada_kernel,
        out_shape=jax.ShapeDtypeStruct((depth, 8, n), F32),
        grid=(depth, n // tn),
        in_specs=[pl.BlockSpec((8, d), lambda l, j: (0, 0)),
                  pl.BlockSpec((1, d, tn), lambda l, j: (l, 0, j)),
                  pl.BlockSpec((1, 1, tn), lambda l, j: (l, 0, j))],
        out_specs=pl.BlockSpec((1, 8, tn), lambda l, j: (l, 0, j)),
        compiler_params=_cparams(("arbitrary", "arbitrary")),
        name="ada_modulation",
    )(s8, w_ada, b_ada.reshape(depth, 1, n))


def _rms(x):
    return x * lax.rsqrt(jnp.mean(x * x, axis=-1, keepdims=True) + NORM_EPS)


def _norm_mod_kernel(x_ref, g_ref, sh_ref, sc_ref, o_ref, *, n_ctx):
    i = pl.program_id(0)
    tm = x_ref.shape[0]
    y = _rms(x_ref[...]) * g_ref[...]
    y = y * (1.0 + _mod_row(sc_ref, i, tm, n_ctx)) + _mod_row(sh_ref, i, tm, n_ctx)
    o_ref[...] = y.astype(o_ref.dtype)


def _norm_mod(x, g, mod, k_shift, k_scale, n_ctx):
    t, d = x.shape
    tm = ROW_TILE
    return pl.pallas_call(
        functools.partial(_norm_mod_kernel, n_ctx=n_ctx),
        out_shape=jax.ShapeDtypeStruct((t, d), BF16),
        grid=(t // tm,),
        in_specs=[pl.BlockSpec((tm, d), lambda i: (i, 0)),
                  pl.BlockSpec((1, d), lambda i: (0, 0)),
                  pl.BlockSpec((8, d), lambda i: (0, k_shift)),
                  pl.BlockSpec((8, d), lambda i: (0, k_scale))],
        out_specs=pl.BlockSpec((tm, d), lambda i: (i, 0)),
        compiler_params=_cparams(("arbitrary",)),
        name="norm_mod",
    )(x, g.reshape(1, d), mod, mod)


def _cast_weight_once(w_ref, wbf_ref):
    @pl.when(pl.program_id(1) == 0)
    def _():
        wbf_ref[...] = w_ref[...].astype(BF16)


def _mm_plain_kernel(a_ref, w_ref, o_ref, wbf_ref, *, scale):
    _cast_weight_once(w_ref, wbf_ref)
    acc = jnp.dot(a_ref[...], wbf_ref[...], preferred_element_type=F32)
    if scale != 1.0:
        acc = acc * scale
    o_ref[...] = acc.astype(o_ref.dtype)


def _mm_residual_kernel(a_ref, w_ref, x_ref, mod_ref, o_ref, wbf_ref, *, n_ctx):
    _cast_weight_once(w_ref, wbf_ref)
    tm = a_ref.shape[0]
    acc = jnp.dot(a_ref[...], wbf_ref[...], preferred_element_type=F32)
    row = pl.program_id(1) * tm + lax.broadcasted_iota(I32, (tm, 1), 0)
    gate = jnp.where(row < n_ctx, mod_ref[1:2, :], mod_ref[0:1, :])
    o_ref[...] = x_ref[...] + gate * acc


def _mm_rope_kernel(a_ref, w_ref, cf_ref, sf_ref, cb_ref, sb_ref, o_ref, wbf_ref, *, scale, head_dim):
    _cast_weight_once(w_ref, wbf_ref)
    acc = jnp.dot(a_ref[...], wbf_ref[...], preferred_element_type=F32)
    if scale != 1.0:
        acc = acc * scale
    half = head_dim // 2
    for d, (c_ref, s_ref) in enumerate(((cf_ref, sf_ref), (cb_ref, sb_ref))):
        c, s = c_ref[...], s_ref[...]
        for h in range(acc.shape[1] // head_dim):
            x1 = acc[:, h * head_dim:h * head_dim + half]
            x2 = acc[:, h * head_dim + half:(h + 1) * head_dim]
            o_ref[d, :, h * head_dim:h * head_dim + half] = (x1 * c - x2 * s).astype(o_ref.dtype)
            o_ref[d, :, h * head_dim + half:(h + 1) * head_dim] = (x1 * s + x2 * c).astype(o_ref.dtype)


def _mm_tiles(m, k, n):
    tm = _pick_tile(m, (768, 512, 384, 256, 128))
    tn_cap = 1024 if k <= 2048 else 512
    tn = _pick_tile(n, tuple(c for c in (1024, 512, 256, 128) if c <= tn_cap))
    return tm, tn


def _matmul(a, w, layer, col0, n, out_dtype, scale=1.0):
    m, k = a.shape
    tm, tn = _mm_tiles(m, k, n)
    assert col0 % tn == 0
    c0 = col0 // tn
    return pl.pallas_call(
        functools.partial(_mm_plain_kernel, scale=scale),
        out_shape=jax.ShapeDtypeStruct((m, n), out_dtype),
        grid=(n // tn, m // tm),
        in_specs=[pl.BlockSpec((tm, k), lambda j, i: (i, 0)),
                  pl.BlockSpec((None, k, tn), lambda j, i: (layer, 0, j + c0))],
        out_specs=pl.BlockSpec((tm, tn), lambda j, i: (i, j)),
        scratch_shapes=[pltpu.VMEM((k, tn), BF16)],
        compiler_params=_cparams(("arbitrary", "arbitrary")),
        name="matmul",
    )(a, w)


def _matmul_residual(a, w, layer, x, mod, k_gate, n_ctx):
    m, k = a.shape
    n = w.shape[2]
    tm, tn = _mm_tiles(m, k, n)
    kb = k_gate * (n // tn)
    return pl.pallas_call(
        functools.partial(_mm_residual_kernel, n_ctx=n_ctx),
        out_shape=jax.ShapeDtypeStruct((m, n), F32),
        grid=(n // tn, m // tm),
        in_specs=[pl.BlockSpec((tm, k), lambda j, i: (i, 0)),
                  pl.BlockSpec((None, k, tn), lambda j, i: (layer, 0, j)),
                  pl.BlockSpec((tm, tn), lambda j, i: (i, j)),
                  pl.BlockSpec((8, tn), lambda j, i: (0, kb + j))],
        out_specs=pl.BlockSpec((tm, tn), lambda j, i: (i, j)),
        scratch_shapes=[pltpu.VMEM((k, tn), BF16)],
        compiler_params=_cparams(("arbitrary", "arbitrary")),
        name="matmul_residual",
    )(a, w, x, mod)


def _matmul_rope(a, w, layer, col0, n, tables, scale, head_dim):
    m, k = a.shape
    tm, tn = _mm_tiles(m, k, n)
    assert col0 % tn == 0 and tn % head_dim == 0
    c0 = col0 // tn
    half = head_dim // 2
    tab_spec = pl.BlockSpec((tm, half), lambda j, i: (i, 0))
    return pl.pallas_call(
        functools.partial(_mm_rope_kernel, scale=scale, head_dim=head_dim),
        out_shape=jax.ShapeDtypeStruct((2, m, n), BF16),
        grid=(n // tn, m // tm),
        in_specs=[pl.BlockSpec((tm, k), lambda j, i: (i, 0)),
                  pl.BlockSpec((None, k, tn), lambda j, i: (layer, 0, j + c0)),
                  tab_spec, tab_spec, tab_spec, tab_spec],
        out_specs=pl.BlockSpec((2, tm, tn), lambda j, i: (0, i, j)),
        scratch_shapes=[pltpu.VMEM((k, tn), BF16)],
        compiler_params=_cparams(("arbitrary", "arbitrary")),
        name="matmul_rope",
    )(a, w, *tables)


def _chunk_index(c, n_chunks, ctx_chunks, reverse):
    if not reverse:
        return c
    return jnp.where(c < ctx_chunks, ctx_chunks - 1 - c, n_chunks - 1 - (c - ctx_chunks))


def _dot_nt(a, b):
    return lax.dot_general(a, b, (((1,), (1,)), ((), ())), preferred_element_type=F32)


def _dot_tn(a, b):
    return lax.dot_general(a, b, (((0,), (0,)), ((), ())), preferred_element_type=F32)


def _retention_kernel(lg_ref, qf_ref, kf_ref, vf_ref, qb_ref, kb_ref, vb_ref, of_ref, ob_ref, sf_ref, sb_ref,
                      *, heads, dk, dv):
    c = pl.program_id(0)

    @pl.when(c == 0)
    def _():
        sf_ref[...] = jnp.zeros_like(sf_ref)
        sb_ref[...] = jnp.zeros_like(sb_ref)

    n = qf_ref.shape[0]
    ri = lax.broadcasted_iota(I32, (n, n), 0)
    ci = lax.broadcasted_iota(I32, (n, n), 1)
    r1 = lax.broadcasted_iota(I32, (n, 1), 0)
    streams = []
    for reverse, refs in ((False, (qf_ref, kf_ref, vf_ref, of_ref, sf_ref)),
                          (True, (qb_ref, kb_ref, vb_ref, ob_ref, sb_ref))):
        diff = (ci - ri) if reverse else (ri - ci)
        dmask = diff >= 0
        dist = jnp.where(dmask, diff, 0).astype(F32)
        pos = ((n - 1 - r1) if reverse else r1).astype(F32)
        streams.append((dmask, dist, pos) + refs)
    for h in range(heads):
        for d, (dmask, dist, pos, q_ref, k_ref, v_ref, o_ref, s_ref) in enumerate(streams):
            lg = jnp.full((1, 1), lg_ref[d * heads + h], F32)
            decay = jnp.where(dmask, jnp.exp(dist * lg), 0.0)
            q_dec = jnp.exp((pos + 1.0) * lg)
            k_dec = jnp.exp((n - 1.0 - pos) * lg)
            c_dec = jnp.exp(float(n) * lg)
            q = q_ref[:, h * dk:(h + 1) * dk]
            k = k_ref[:, h * dk:(h + 1) * dk]
            v = v_ref[:, h * dv:(h + 1) * dv]
            s_prev = s_ref[h]
            scores = (_dot_nt(q, k) * decay).astype(BF16)
            inner = jnp.dot(scores, v, preferred_element_type=F32)
            cross = jnp.dot(q, s_prev.astype(BF16), preferred_element_type=F32)
            o_ref[:, h * dv:(h + 1) * dv] = (inner + q_dec * cross).astype(o_ref.dtype)
            kd = (k.astype(F32) * k_dec).astype(BF16)
            s_ref[h] = c_dec * s_prev + _dot_tn(kd, v)


def _retention_scan(log_gamma, q, k, vg, n_ctx, heads, dk, dv):
    t = q.shape[1]
    nc, cc = t // CHUNK, n_ctx // CHUNK
    cm = lambda c: _chunk_index(c, nc, cc, True)
    qk_spec = lambda d, idx: pl.BlockSpec((None, CHUNK, heads * dk), lambda c: (d, idx(c), 0))
    v_spec = lambda idx: pl.BlockSpec((CHUNK, heads * dv), lambda c: (idx(c), 0))
    fwd = lambda c: c
    out = jax.ShapeDtypeStruct((t, heads * dv), BF16)
    return pl.pallas_call(
        functools.partial(_retention_kernel, heads=heads, dk=dk, dv=dv),
        out_shape=(out, out),
        grid=(nc,),
        in_specs=[pl.BlockSpec(memory_space=pltpu.SMEM),
                  qk_spec(0, fwd), qk_spec(0, fwd), v_spec(fwd),
                  qk_spec(1, cm), qk_spec(1, cm), v_spec(cm)],
        out_specs=(v_spec(fwd), v_spec(cm)),
        scratch_shapes=[pltpu.VMEM((heads, dk, dv), F32), pltpu.VMEM((heads, dk, dv), F32)],
        compiler_params=_cparams(("arbitrary",)),
        name="retention_scan",
    )(log_gamma.reshape(-1), q, k, vg, q, k, vg)


def _ret_post_kernel(yf_ref, yb_ref, g_ref, w_ref, b_ref, o_ref, *, heads, dv):
    for h in range(heads):
        sl = slice(h * dv, (h + 1) * dv)
        y = yf_ref[:, sl].astype(F32) + yb_ref[:, sl].astype(F32)
        mu = jnp.mean(y, axis=-1, keepdims=True)
        yc = y - mu
        var = jnp.mean(yc * yc, axis=-1, keepdims=True)
        yn = yc * lax.rsqrt(var + NORM_EPS)
        g = g_ref[:, sl].astype(F32)
        silu = g * (1.0 / (1.0 + jnp.exp(-g)))
        o_ref[:, sl] = ((yn * w_ref[:, sl] + b_ref[:, sl]) * silu).astype(o_ref.dtype)


def _ret_post(yf, yb, vg, gn_w, gn_b, heads, dv):
    t, vw = yf.shape
    tm = ROW_TILE
    return pl.pallas_call(
        functools.partial(_ret_post_kernel, heads=heads, dv=dv),
        out_shape=jax.ShapeDtypeStruct((t, vw), BF16),
        grid=(t // tm,),
        in_specs=[pl.BlockSpec((tm, vw), lambda i: (i, 0)),
                  pl.BlockSpec((tm, vw), lambda i: (i, 0)),
                  pl.BlockSpec((tm, vw), lambda i: (i, 1)),
                  pl.BlockSpec((1, vw), lambda i: (0, 0)),
                  pl.BlockSpec((1, vw), lambda i: (0, 0))],
        out_specs=pl.BlockSpec((tm, vw), lambda i: (i, 0)),
        compiler_params=_cparams(("arbitrary",)),
        name="retention_post",
    )(yf, yb, vg, gn_w.reshape(1, vw), gn_b.reshape(1, vw))


def _conv_silu_kernel(x_ref, prev_ref, next_ref, w_ref, o_ref, *, n_ctx, t_total, q_cols, q_scale):
    i = pl.program_id(0)
    tm = x_ref.shape[0]
    x = x_ref[...]
    row = lax.broadcasted_iota(I32, (tm, 1), 0)
    grow = i * tm + row
    halo = prev_ref.shape[0]
    x_prev = jnp.where(row == 0, prev_ref[halo - 1:halo, :], pltpu.roll(x, 1, axis=0))
    x_prev = jnp.where((grow == 0) | (grow == n_ctx), 0.0, x_prev)
    x_next = jnp.where(row == tm - 1, next_ref[0:1, :], pltpu.roll(x, tm - 1, axis=0))
    x_next = jnp.where((grow == n_ctx - 1) | (grow == t_total - 1), 0.0, x_next)
    y = x_prev * w_ref[0:1, :] + x * w_ref[1:2, :] + x_next * w_ref[2:3, :]
    y = y * (1.0 / (1.0 + jnp.exp(-y)))
    o_ref[:, :q_cols] = (y[:, :q_cols] * q_scale).astype(o_ref.dtype)
    o_ref[:, q_cols:] = y[:, q_cols:].astype(o_ref.dtype)


def _conv_silu(x, conv_w, n_ctx, q_cols, q_scale):
    t, c = x.shape
    tm = ROW_TILE
    halo = 8
    r = tm // halo
    last = t // halo - 1
    w8 = jnp.zeros((8, c), F32).at[:3].set(conv_w)
    return pl.pallas_call(
        functools.partial(_conv_silu_kernel, n_ctx=n_ctx, t_total=t, q_cols=q_cols, q_scale=q_scale),
        out_shape=jax.ShapeDtypeStruct((t, c), BF16),
        grid=(t // tm,),
        in_specs=[pl.BlockSpec((tm, c), lambda i: (i, 0)),
                  pl.BlockSpec((halo, c), lambda i: (jnp.maximum(i * r - 1, 0), 0)),
                  pl.BlockSpec((halo, c), lambda i: (jnp.minimum((i + 1) * r, last), 0)),
                  pl.BlockSpec((8, c), lambda i: (0, 0))],
        out_specs=pl.BlockSpec((tm, c), lambda i: (i, 0)),
        compiler_params=_cparams(("arbitrary",)),
        name="mlstm_conv_silu",
    )(x, x, x, w8)


def _mlstm_kernel(qk_ref, v_ref, gt_ref, bias_ref, o_ref, c_ref, n_ref, m_ref, *, reverse, heads, dk, dv, d_idx):
    c = pl.program_id(0)

    @pl.when(c == 0)
    def _():
        c_ref[...] = jnp.zeros_like(c_ref)
        n_ref[...] = jnp.zeros_like(n_ref)
        m_ref[...] = jnp.zeros_like(m_ref)

    n = qk_ref.shape[0]
    gates = gt_ref[...] + bias_ref[...]
    i_all = GATE_SOFTCAP * jnp.tanh(gates * (1.0 / GATE_SOFTCAP))
    f_all = -(jnp.maximum(-gates, 0.0) + jnp.log1p(jnp.exp(-jnp.abs(gates))))
    ri = lax.broadcasted_iota(I32, (n, n), 0)
    ci = lax.broadcasted_iota(I32, (n, n), 1)
    mask = (ci >= ri) if reverse else (ci <= ri)
    tri = jnp.where(mask, 1.0, 0.0).astype(F32)
    b_all = jnp.dot(tri, f_all, preferred_element_type=F32, precision=lax.Precision.HIGHEST)
    b_all_t = b_all.T
    i_all_t = i_all.T
    last = 0 if reverse else n - 1
    qk_w = heads * dk
    for h in range(heads):
        ic = d_idx * 2 * heads + h
        fc = ic + heads
        b_col = b_all[:, fc:fc + 1]
        b_row = b_all_t[fc:fc + 1, :]
        i_col = i_all[:, ic:ic + 1]
        i_row = i_all_t[ic:ic + 1, :]
        m_prev = m_ref[h][:, 0:1]
        q = qk_ref[:, h * dk:(h + 1) * dk]
        k = qk_ref[:, qk_w + h * dk:qk_w + (h + 1) * dk]
        v = v_ref[:, h * dv:(h + 1) * dv]
        c_prev = c_ref[h]
        n_prev = n_ref[h]
        d_log = jnp.where(mask, b_col - b_row + i_row, NEG_INF)
        m_t = jnp.maximum(b_col + m_prev, jnp.max(d_log, axis=-1, keepdims=True))
        w = jnp.exp(d_log - m_t)
        s = _dot_nt(q, k) * w
        inter = jnp.exp(b_col + m_prev - m_t)
        num = jnp.dot(s.astype(BF16), v, preferred_element_type=F32) + inter * jnp.dot(
            q, c_prev.astype(BF16), preferred_element_type=F32)
        den = jnp.sum(s, axis=-1, keepdims=True) + inter * jnp.sum(q.astype(F32) * n_prev, axis=-1, keepdims=True)
        o_ref[:, h * dv:(h + 1) * dv] = (num / jnp.maximum(jnp.abs(den), jnp.exp(-m_t))).astype(o_ref.dtype)
        b_last = b_col[last:last + 1, :]
        g = b_last - b_col + i_col
        m_new = jnp.maximum(b_last + m_prev, jnp.max(g, axis=0, keepdims=True))
        wk = jnp.exp(g - m_new) * k.astype(F32)
        dec = jnp.exp(b_last + m_prev - m_new)
        c_ref[h] = dec * c_prev + _dot_tn(wk.astype(BF16), v)
        n_ref[h] = dec * n_prev + jnp.sum(wk, axis=0, keepdims=True)
        m_ref[h] = jnp.broadcast_to(m_new, m_ref.shape[1:])


def _mlstm_scan(qk, vo, gates, bias, n_ctx, heads, dk, dv, reverse):
    t = qk.shape[0]
    nc, cc = t // CHUNK, n_ctx // CHUNK
    cm = lambda c: _chunk_index(c, nc, cc, reverse)
    return pl.pallas_call(
        functools.partial(_mlstm_kernel, reverse=reverse, heads=heads, dk=dk, dv=dv, d_idx=1 if reverse else 0),
        out_shape=jax.ShapeDtypeStruct((t, heads * dv), BF16),
        grid=(nc,),
        in_specs=[pl.BlockSpec((CHUNK, 2 * heads * dk), lambda c: (cm(c), 0)),
                  pl.BlockSpec((CHUNK, heads * dv), lambda c: (cm(c), 0)),
                  pl.BlockSpec((CHUNK, LANES), lambda c: (cm(c), 0)),
                  pl.BlockSpec((1, LANES), lambda c: (0, 0))],
        out_specs=pl.BlockSpec((CHUNK, heads * dv), lambda c: (cm(c), 0)),
        scratch_shapes=[pltpu.VMEM((heads, dk, dv), F32),
                        pltpu.VMEM((heads, 1, dk), F32),
                        pltpu.VMEM((heads, 1, LANES), F32)],
        compiler_params=_cparams(("arbitrary",)),
        name="mlstm_scan_rev" if reverse else "mlstm_scan_fwd",
    )(qk, vo, gates, bias)


def _mlstm_post_kernel(yf_ref, yb_ref, o_ref_in, w_ref, o_ref, *, heads, dv):
    for h in range(heads):
        sl = slice(h * dv, (h + 1) * dv)
        y = _rms(yf_ref[:, sl].astype(F32) + yb_ref[:, sl].astype(F32))
        o = o_ref_in[:, sl].astype(F32)
        o_ref[:, sl] = (y * w_ref[:, sl] * (1.0 / (1.0 + jnp.exp(-o)))).astype(o_ref.dtype)


def _mlstm_post(yf, yb, vo, norm_w, heads, dv):
    t, vw = yf.shape
    tm = ROW_TILE
    return pl.pallas_call(
        functools.partial(_mlstm_post_kernel, heads=heads, dv=dv),
        out_shape=jax.ShapeDtypeStruct((t, vw), BF16),
        grid=(t // tm,),
        in_specs=[pl.BlockSpec((tm, vw), lambda i: (i, 0)),
                  pl.BlockSpec((tm, vw), lambda i: (i, 0)),
                  pl.BlockSpec((tm, vw), lambda i: (i, 1)),
                  pl.BlockSpec((1, vw), lambda i: (0, 0))],
        out_specs=pl.BlockSpec((tm, vw), lambda i: (i, 0)),
        compiler_params=_cparams(("arbitrary",)),
        name="mlstm_post",
    )(yf, yb, vo, norm_w.reshape(1, vw))


def _qk_norm_rope_kernel(x_ref, cos_ref, sin_ref, qn_ref, kn_ref, q_ref, k_ref, *, q_heads, k_heads, hd):
    cos, sin = cos_ref[...], sin_ref[...]
    lane = lax.broadcasted_iota(I32, (1, hd), 1)
    first = (lane % (hd // 2)) < (hd // 4)
    for h in range(q_heads + k_heads):
        x = x_ref[:, h * hd:(h + 1) * hd]
        w = qn_ref[...] if h < q_heads else kn_ref[...]
        xh = _rms(x) * w
        partner = jnp.where(first, pltpu.roll(xh, hd - hd // 4, axis=1), pltpu.roll(xh, hd // 4, axis=1))
        y = (xh * cos + partner * sin).astype(q_ref.dtype)
        if h < q_heads:
            q_ref[:, h * hd:(h + 1) * hd] = y
        else:
            k_ref[:, (h - q_heads) * hd:(h - q_heads + 1) * hd] = y


def _qk_norm_rope(qk, cos, sin, q_norm, k_norm, q_heads, k_heads, hd):
    t = qk.shape[0]
    tm = ROW_TILE
    return pl.pallas_call(
        functools.partial(_qk_norm_rope_kernel, q_heads=q_heads, k_heads=k_heads, hd=hd),
        out_shape=(jax.ShapeDtypeStruct((t, q_heads * hd), BF16), jax.ShapeDtypeStruct((t, k_heads * hd), BF16)),
        grid=(t // tm,),
        in_specs=[pl.BlockSpec((tm, (q_heads + k_heads) * hd), lambda i: (i, 0)),
                  pl.BlockSpec((tm, hd), lambda i: (i, 0)),
                  pl.BlockSpec((tm, hd), lambda i: (i, 0)),
                  pl.BlockSpec((1, hd), lambda i: (0, 0)),
                  pl.BlockSpec((1, hd), lambda i: (0, 0))],
        out_specs=(pl.BlockSpec((tm, q_heads * hd), lambda i: (i, 0)),
                   pl.BlockSpec((tm, k_heads * hd), lambda i: (i, 0))),
        compiler_params=_cparams(("arbitrary",)),
        name="attn_qk_norm_rope",
    )(qk, cos, sin, q_norm.reshape(1, hd), k_norm.reshape(1, hd))


def _flash_kernel(q_ref, k_ref, v_ref, o_ref, m_ref, l_ref, acc_ref, alpha_ref, sa_ref, sb_ref, p_ref,
                  *, groups, hd, tk, n_ctx):
    tq = q_ref.shape[0]
    n_kv = k_ref.shape[0]
    n = n_kv // tk
    q = jnp.concatenate([q_ref[:, g * hd:(g + 1) * hd] for g in range(groups)], axis=0)
    m_ref[...] = jnp.full_like(m_ref, NEG_INF)
    l_ref[...] = jnp.zeros_like(l_ref)
    acc_ref[...] = jnp.zeros_like(acc_ref)

    def scores(off, size):
        return _dot_nt(q, k_ref[pl.ds(off, size), :])

    def absorb(s_ref, off, size):
        for band in range(groups * tq // FLASH_BAND):
            rows = slice(band * FLASH_BAND, (band + 1) * FLASH_BAND)
            s = s_ref[rows, :size]
            m_prev = m_ref[rows, :]
            m_new = jnp.maximum(m_prev, jnp.max(s, axis=-1, keepdims=True))
            alpha = jnp.exp2(m_prev - m_new)
            p = jnp.exp2(s - jnp.concatenate([m_new] * (size // LANES), axis=1))
            l_ref[rows, :] = alpha * l_ref[rows, :] + jnp.sum(p, axis=-1, keepdims=True)
            m_ref[rows, :] = m_new
            alpha_ref[rows, :] = alpha
            p_ref[rows, :size] = p.astype(BF16)
        acc_ref[...] = alpha_ref[...] * acc_ref[...] + jnp.dot(p_ref[:, :size], v_ref[pl.ds(off, size), :],
                                                                preferred_element_type=F32)

    is_ctx = pl.program_id(1) < n_ctx // tq

    @pl.when(is_ctx)
    def _():
        sa_ref[:, :n_ctx] = scores(0, n_ctx)
        absorb(sa_ref, 0, n_ctx)

    @pl.when(jnp.logical_not(is_ctx))
    def _():
        bufs = (sa_ref, sb_ref)
        bufs[0][...] = scores(0, tk)
        for j in range(n):
            if j + 1 < n:
                bufs[(j + 1) % 2][...] = scores((j + 1) * tk, tk)
            absorb(bufs[j % 2], j * tk, tk)

    out = acc_ref[...] / l_ref[...]
    for g in range(groups):
        o_ref[:, g * hd:(g + 1) * hd] = out[g * tq:(g + 1) * tq].astype(o_ref.dtype)


def _flash_attention(q, k, v, n_ctx, kv_heads, groups, hd):
    t = q.shape[0]
    tq = _pick_tile(math.gcd(t, n_ctx), (256, 128))
    tk = _pick_tile(t, (768, 512, 256, 128))
    assert hd == LANES and n_ctx <= tk and (groups * tq) % FLASH_BAND == 0
    gw = groups * hd
    return pl.pallas_call(
        functools.partial(_flash_kernel, groups=groups, hd=hd, tk=tk, n_ctx=n_ctx),
        out_shape=jax.ShapeDtypeStruct((t, kv_heads * gw), BF16),
        grid=(kv_heads, t // tq),
        in_specs=[pl.BlockSpec((tq, gw), lambda g, i: (i, g)),
                  pl.BlockSpec((t, hd), lambda g, i: (0, g)),
                  pl.BlockSpec((t, hd), lambda g, i: (0, g))],
        out_specs=pl.BlockSpec((tq, gw), lambda g, i: (i, g)),
        scratch_shapes=[pltpu.VMEM((groups * tq, LANES), F32),
                        pltpu.VMEM((groups * tq, LANES), F32),
                        pltpu.VMEM((groups * tq, hd), F32),
                        pltpu.VMEM((groups * tq, LANES), F32),
                        pltpu.VMEM((groups * tq, tk), F32),
                        pltpu.VMEM((groups * tq, tk), F32),
                        pltpu.VMEM((groups * tq, tk), BF16)],
        compiler_params=_cparams(("arbitrary", "arbitrary")),
        name="flash_attention",
    )(q, k, v)


def _router_kernel(x_ref, g_ref, sh_ref, sc_ref, wr_ref, br_ref, h_ref, ids_ref, wts_ref, cnt_ref, run_ref,
                   *, n_ctx, row0):
    i = pl.program_id(0)
    tm, d = x_ref.shape

    @pl.when(i == 0)
    def _():
        run_ref[...] = jnp.zeros_like(run_ref)

    ib = i + row0 // tm
    h = _rms(x_ref[...]) * g_ref[...]
    h = h * (1.0 + _mod_row(sc_ref, ib, tm, n_ctx)) + _mod_row(sh_ref, ib, tm, n_ctx)
    h_ref[...] = h
    h_hi = h.astype(BF16)
    h_lo = (h - h_hi.astype(F32)).astype(BF16)
    hw = jnp.dot(h_hi, wr_ref[...], preferred_element_type=F32)
    lw = jnp.dot(h_lo, wr_ref[:, :LANES], preferred_element_type=F32)
    logits = hw[:, :LANES] + (hw[:, LANES:] + lw) + br_ref[...]
    lane = lax.broadcasted_iota(I32, (tm, LANES), 1)
    big = jnp.int32(LANES)
    gl = jnp.where((lane >= MOE_EXPERTS) & (lane < MOE_EXPERTS + MOE_GROUPS), logits, NEG_INF)
    g_max = jnp.max(gl, axis=-1, keepdims=True)
    g_idx = jnp.min(jnp.where(gl == g_max, lane, big), axis=-1, keepdims=True) - MOE_EXPERTS
    g_w = 1.0 / jnp.sum(jnp.exp(gl - g_max), axis=-1, keepdims=True)
    el = jnp.where((lane < MOE_EXPERTS) & ((lane >> 3) == g_idx), logits, NEG_INF)
    m1 = jnp.max(el, axis=-1, keepdims=True)
    i1 = jnp.min(jnp.where(el == m1, lane, big), axis=-1, keepdims=True)
    el2 = jnp.where(lane == i1, NEG_INF, el)
    m2 = jnp.max(el2, axis=-1, keepdims=True)
    i2 = jnp.min(jnp.where(el2 == m2, lane, big), axis=-1, keepdims=True)
    e_sum = jnp.sum(jnp.exp(el - m1), axis=-1, keepdims=True)
    p1 = 1.0 / e_sum
    p2 = jnp.exp(m2 - m1) / e_sum
    w1 = g_w * (p1 / (p1 + p2))
    w2 = g_w * (p2 / (p1 + p2))
    onehot = jnp.where((lane == i1) | (lane == i2), 1.0, 0.0)
    ri = lax.broadcasted_iota(I32, (tm, tm), 0)
    ci = lax.broadcasted_iota(I32, (tm, tm), 1)
    strict = jnp.where(ci < ri, 1.0, 0.0).astype(BF16)
    before = jnp.dot(strict, onehot.astype(BF16), preferred_element_type=F32) + run_ref[0:1, :]
    r1 = jnp.sum(jnp.where(lane == i1, before, 0.0), axis=-1, keepdims=True).astype(I32)
    r2 = jnp.sum(jnp.where(lane == i2, before, 0.0), axis=-1, keepdims=True).astype(I32)
    ids_ref[...] = jnp.where(lane == 0, i1, jnp.where(lane == 1, i2, jnp.where(lane == 2, r1, jnp.where(lane == 3, r2, 0))))
    wts_ref[...] = jnp.where(lane == 0, w1, jnp.where(lane == 1, w2, 0.0))
    total = run_ref[0:1, :] + jnp.sum(onehot, axis=0, keepdims=True)
    run_ref[...] = jnp.broadcast_to(total, run_ref.shape)
    cnt_ref[...] = jnp.broadcast_to(total, cnt_ref.shape)


def _router(x, g, mod, k_shift, k_scale, w_router, b_router, n_ctx, row0):
    t, d = x.shape
    tm = ROW_TILE
    n = t - row0
    off = row0 // tm
    return pl.pallas_call(
        functools.partial(_router_kernel, n_ctx=n_ctx, row0=row0),
        out_shape=(jax.ShapeDtypeStruct((n, d), F32),
                   jax.ShapeDtypeStruct((n, LANES), I32),
                   jax.ShapeDtypeStruct((n, LANES), F32),
                   jax.ShapeDtypeStruct((8, LANES), F32)),
        grid=(n // tm,),
        in_specs=[pl.BlockSpec((tm, d), lambda i: (i + off, 0)),
                  pl.BlockSpec((1, d), lambda i: (0, 0)),
                  pl.BlockSpec((8, d), lambda i: (0, k_shift)),
                  pl.BlockSpec((8, d), lambda i: (0, k_scale)),
                  pl.BlockSpec((d, 2 * LANES), lambda i: (0, 0)),
                  pl.BlockSpec((1, LANES), lambda i: (0, 0))],
        out_specs=(pl.BlockSpec((tm, d), lambda i: (i, 0)),
                   pl.BlockSpec((tm, LANES), lambda i: (i, 0)),
                   pl.BlockSpec((tm, LANES), lambda i: (i, 0)),
                   pl.BlockSpec((8, LANES), lambda i: (0, 0))),
        scratch_shapes=[pltpu.VMEM((8, LANES), F32)],
        compiler_params=_cparams(("arbitrary",)),
        name="moe_router",
    )(x, g.reshape(1, d), mod, mod, w_router, b_router)


def _slot_index_kernel(ids_ref, seg_ref, o_ref):
    ids = ids_ref[...]
    lane = lax.broadcasted_iota(I32, ids.shape, 1)
    seg = seg_ref[...]
    slots = []
    for k in range(MOE_TOPK):
        start = jnp.sum(jnp.where(lane == ids[:, k:k + 1], seg, 0.0), axis=-1, keepdims=True)
        slots.append(start.astype(I32) + ids[:, MOE_TOPK + k:MOE_TOPK + k + 1])
    o_ref[...] = jnp.where(lane == 0, slots[0], jnp.where(lane == 1, slots[1], 0))


def _slot_index(ids, seg_start):
    n = ids.shape[0]
    tm = ROW_TILE
    seg = jnp.zeros((1, LANES), F32).at[0, :MOE_EXPERTS].set(seg_start.astype(F32))
    out = pl.pallas_call(
        _slot_index_kernel,
        out_shape=jax.ShapeDtypeStruct((n, LANES), I32),
        grid=(n // tm,),
        in_specs=[pl.BlockSpec((tm, LANES), lambda i: (i, 0)),
                  pl.BlockSpec((1, LANES), lambda i: (0, 0))],
        out_specs=pl.BlockSpec((tm, LANES), lambda i: (i, 0)),
        compiler_params=_cparams(("arbitrary",)),
        name="moe_slot_index",
    )(ids, seg)
    return out[:, :MOE_TOPK].reshape(-1)


def _dispatch_kernel(dest_ref, h_ref, xs_in_ref, xs_ref, sem):
    del xs_in_ref
    i = pl.program_id(0)
    tm = h_ref.shape[0]

    def row_copy(r, k):
        slot = dest_ref[(i * tm + r) * MOE_TOPK + k]
        return pltpu.make_async_copy(h_ref.at[pl.ds(r, 1)], xs_ref.at[pl.ds(slot, 1)], sem)

    def start(r, carry):
        for k in range(MOE_TOPK):
            row_copy(r, k).start(priority=k)
        return carry

    def wait(r, carry):
        for k in range(MOE_TOPK):
            row_copy(r, k).wait()
        return carry

    lax.fori_loop(0, tm, start, 0, unroll=8)
    lax.fori_loop(0, tm, wait, 0, unroll=8)


def _dispatch(dest, h, xs_init):
    n, w = h.shape
    tm = ROW_TILE
    return pl.pallas_call(
        _dispatch_kernel,
        out_shape=jax.ShapeDtypeStruct(xs_init.shape, xs_init.dtype),
        grid_spec=pltpu.PrefetchScalarGridSpec(
            num_scalar_prefetch=1,
            grid=(n // tm,),
            in_specs=[pl.BlockSpec((tm, w), lambda i, dest: (i, 0)),
                      pl.BlockSpec(memory_space=pl.ANY)],
            out_specs=pl.BlockSpec(memory_space=pl.ANY),
            scratch_shapes=[pltpu.SemaphoreType.DMA(())]),
        input_output_aliases={2: 0},
        compiler_params=_cparams(("arbitrary",)),
        name="moe_dispatch",
    )(dest, h, xs_init)


def _expert_kernel(be_ref, first_ref, next_ref, nb_ref, xs_ref, wg_hbm, wu_hbm, wd_hbm, ys_ref,
                   wg_st, wu_st, wd_st, wg_bf, wu_bf, wd_bf, sem, *, layer):
    b = pl.program_id(0)
    used = b < nb_ref[0]

    weights = ((wg_hbm, wg_st, wg_bf), (wu_hbm, wu_st, wu_bf), (wd_hbm, wd_st, wd_bf))

    def fetch(e, idx):
        hbm, st, _ = weights[idx]
        return pltpu.make_async_copy(hbm.at[layer, e], st, sem.at[idx])

    @pl.when(b == 0)
    def _():
        for idx in range(len(weights)):
            fetch(be_ref[0], idx).start()

    @pl.when(used & (first_ref[b] == 1))
    def _():
        has_next = next_ref[b] >= 0
        for idx, (_, st, bf) in enumerate(weights):
            fetch(be_ref[b], idx).wait()
            bf[...] = st[...].astype(BF16)

            @pl.when(has_next)
            def _():
                fetch(next_ref[b], idx).start()

    @pl.when(used)
    def _():
        x = xs_ref[...].astype(BF16)
        gate = jnp.dot(x, wg_bf[...], preferred_element_type=F32)
        up = jnp.dot(x, wu_bf[...], preferred_element_type=F32)
        act = (gate * (1.0 / (1.0 + jnp.exp(-gate))) * up).astype(BF16)
        ys_ref[...] = jnp.dot(act, wd_bf[...], preferred_element_type=F32)

    @pl.when(jnp.logical_not(used))
    def _():
        ys_ref[...] = jnp.zeros_like(ys_ref)


def _experts(block_expert, first, next_expert, n_used, xs, w_gate, w_up, w_down, layer):
    n_slots, d = xs.shape
    ff = w_gate.shape[3]
    nb = n_slots // MOE_BLOCK
    row_spec = pl.BlockSpec((MOE_BLOCK, d), lambda b, *_: (b, 0))
    hbm = pl.BlockSpec(memory_space=pl.ANY)
    return pl.pallas_call(
        functools.partial(_expert_kernel, layer=layer),
        out_shape=jax.ShapeDtypeStruct((n_slots, d), F32),
        grid_spec=pltpu.PrefetchScalarGridSpec(
            num_scalar_prefetch=4,
            grid=(nb,),
            in_specs=[row_spec, hbm, hbm, hbm],
            out_specs=row_spec,
            scratch_shapes=[pltpu.VMEM((d, ff), F32), pltpu.VMEM((d, ff), F32), pltpu.VMEM((ff, d), F32),
                            pltpu.VMEM((d, ff), BF16), pltpu.VMEM((d, ff), BF16), pltpu.VMEM((ff, d), BF16),
                            pltpu.SemaphoreType.DMA((3,))]),
        compiler_params=_cparams(("arbitrary",)),
        name="moe_experts",
    )(block_expert, first, next_expert, n_used, xs, w_gate, w_up, w_down)


def _combine_kernel(dest_ref, x_ref, wts_ref, mod_ref, ys_ref, g_ref, sh_ref, sc_ref, *rest, n_ctx, row0, last):
    if last:
        o_ref, buf, sem = rest
    else:
        o_ref, h_ref, buf, sem = rest
    i = pl.program_id(0)
    tm = x_ref.shape[0]
    cur = i % 2

    def row_copy(block, half, r, k):
        slot = dest_ref[(block * tm + r) * MOE_TOPK + k]
        return pltpu.make_async_copy(ys_ref.at[pl.ds(slot, 1)], buf.at[half, k, pl.ds(r, 1)], sem.at[half])

    def for_rows(fn):
        def body(r, carry):
            for k in range(MOE_TOPK):
                fn(r, k)
            return carry
        lax.fori_loop(0, tm, body, 0, unroll=8)

    @pl.when(i == 0)
    def _():
        for_rows(lambda r, k: row_copy(0, 0, r, k).start(priority=k))

    @pl.when(i + 1 < pl.num_programs(0))
    def _():
        for_rows(lambda r, k: row_copy(i + 1, 1 - cur, r, k).start(priority=k))

    for_rows(lambda r, k: row_copy(i, cur, r, k).wait())
    w = wts_ref[...]
    y = w[:, 0:1] * buf[cur, 0] + w[:, 1:2] * buf[cur, 1]
    ib = i + row0 // tm
    x_new = x_ref[...] + _mod_row(mod_ref, ib, tm, n_ctx) * y
    hn = _rms(x_new) * g_ref[...]
    if last:
        o_ref[...] = hn
    else:
        o_ref[...] = x_new
        hn = hn * (1.0 + _mod_row(sc_ref, ib, tm, n_ctx)) + _mod_row(sh_ref, ib, tm, n_ctx)
        h_ref[...] = hn.astype(h_ref.dtype)


def _combine(dest, x, wts, mod, k_gate, ys, n_ctx, row0, g_next, mod_next, last):
    t, d = x.shape
    tm = ROW_TILE
    n = t - row0
    off = row0 // tm
    row_spec = pl.BlockSpec((tm, d), lambda i, dest: (i, 0))
    out_shape = jax.ShapeDtypeStruct((n, d), F32)
    out_specs = row_spec
    if not last:
        out_shape = (out_shape, jax.ShapeDtypeStruct((n, d), BF16))
        out_specs = (row_spec, row_spec)
    return pl.pallas_call(
        functools.partial(_combine_kernel, n_ctx=n_ctx, row0=row0, last=last),
        out_shape=out_shape,
        grid_spec=pltpu.PrefetchScalarGridSpec(
            num_scalar_prefetch=1,
            grid=(n // tm,),
            in_specs=[pl.BlockSpec((tm, d), lambda i, dest: (i + off, 0)),
                      pl.BlockSpec((tm, LANES), lambda i, dest: (i, 0)),
                      pl.BlockSpec((8, d), lambda i, dest: (0, k_gate)),
                      pl.BlockSpec(memory_space=pl.ANY),
                      pl.BlockSpec((1, d), lambda i, dest: (0, 0)),
                      pl.BlockSpec((8, d), lambda i, dest: (0, 0)),
                      pl.BlockSpec((8, d), lambda i, dest: (0, 1))],
            out_specs=out_specs,
            scratch_shapes=[pltpu.VMEM((2, MOE_TOPK, tm, d), F32), pltpu.SemaphoreType.DMA((2,))]),
        compiler_params=_cparams(("arbitrary",)),
        name="moe_combine",
    )(dest, x, wts, mod, ys, g_next.reshape(1, d), mod_next, mod_next)


def _hier_moe(x, g, mod, w_rg, b_rg, w_re, b_re, w_gate, w_up, w_down, layer, n_ctx, row0, xs_buf,
              g_next, mod_next, last):
    t, d = x.shape
    n = t - row0
    pad = LANES - MOE_EXPERTS - MOE_GROUPS
    w_router = jnp.concatenate([w_re, w_rg, jnp.zeros((d, pad), F32)], axis=1)
    w_hi = w_router.astype(BF16)
    w_router = jnp.concatenate([w_hi, (w_router - w_hi.astype(F32)).astype(BF16)], axis=1)
    b_router = jnp.concatenate([b_re, b_rg, jnp.zeros((pad,), F32)]).reshape(1, LANES)
    h, ids, wts, counts = _router(x, g, mod, 3, 4, w_router, b_router, n_ctx, row0)
    counts = counts[0, :MOE_EXPERTS].astype(I32)
    padded = (counts + MOE_BLOCK - 1) // MOE_BLOCK * MOE_BLOCK
    seg_end = jnp.cumsum(padded)
    seg_start = seg_end - padded
    dest = _slot_index(ids, seg_start)
    n_blocks = (t * MOE_TOPK + MOE_EXPERTS * (MOE_BLOCK - 1)) // MOE_BLOCK + 1
    block_start = jnp.arange(n_blocks, dtype=I32) * MOE_BLOCK
    block_expert = jnp.minimum(jnp.sum((seg_end[None, :] <= block_start[:, None]).astype(I32), axis=1),
                               MOE_EXPERTS - 1).astype(I32)
    n_used = (seg_end[-1:] // MOE_BLOCK).astype(I32)
    block_id = jnp.arange(n_blocks, dtype=I32)
    prev_expert = jnp.concatenate([jnp.full((1,), -1, I32), block_expert[:-1]])
    first = ((block_id < n_used[0]) & (block_expert != prev_expert)).astype(I32)
    eid = jnp.arange(MOE_EXPERTS, dtype=I32)
    owner = jnp.where(padded > 0, eid, MOE_EXPERTS)
    later = jnp.concatenate([lax.cummin(owner[::-1])[::-1][1:], jnp.full((1,), MOE_EXPERTS, I32)])
    next_expert = jnp.where(later < MOE_EXPERTS, later, -1)[block_expert].astype(I32)
    if xs_buf is None:
        xs_buf = jnp.zeros((n_blocks * MOE_BLOCK, d), F32)
    xs = _dispatch(dest, h, xs_buf)
    ys = _experts(block_expert, first, next_expert, n_used, xs, w_gate, w_up, w_down, layer)
    return _combine(dest, x, wts, mod, 5, ys, n_ctx, row0, g_next, mod_next, last), xs


def _rope_table(pos, dim, base):
    inv = jnp.power(base, -jnp.arange(0, dim, 2, dtype=F32) / dim)
    ang = pos.astype(F32)[:, None] * inv[None, :]
    return jnp.cos(ang), jnp.sin(ang)


def _flipped_positions(t, n_ctx):
    idx = jnp.arange(t, dtype=I32)
    return jnp.where(idx < n_ctx, n_ctx - 1 - idx, n_ctx + (t - 1 - idx))


def _axial_tables(n_ctx, n_lat, hd):
    rows = n_lat // GRID_W
    row = jnp.repeat(jnp.arange(rows, dtype=I32), GRID_W)
    col = jnp.tile(jnp.arange(GRID_W, dtype=I32), rows)
    half = hd // 2
    cr, sr = _rope_table(row, half, ROPE_BASE)
    cc, sc = _rope_table(col, half, ROPE_BASE)
    cos = jnp.concatenate([cr, cr, cc, cc], axis=1)
    sin = jnp.concatenate([-sr, sr, -sc, sc], axis=1)
    cos = jnp.concatenate([jnp.ones((n_ctx, hd), F32), cos], axis=0)
    sin = jnp.concatenate([jnp.zeros((n_ctx, hd), F32), sin], axis=0)
    return cos, sin


def _retention_layer(xa, h, mod, w_in, logit_gamma, gn_w, gn_b, w_out, j, n_ctx):
    t, d = xa.shape
    heads = RET_HEADS
    dk = d // heads
    dv = 2 * dk
    qk_w, v_w = heads * dk, heads * dv
    pos_f = jnp.arange(t, dtype=I32)
    cf, sf = _rope_table(pos_f, dk, RET_ROPE_BASE)
    cb, sb = _rope_table(_flipped_positions(t, n_ctx), dk, RET_ROPE_BASE)
    tables = (cf, sf, cb, sb)
    q = _matmul_rope(h, w_in, j, 0, qk_w, tables, dk ** -0.5, dk)
    k = _matmul_rope(h, w_in, j, qk_w, qk_w, tables, 1.0, dk)
    vg = _matmul(h, w_in, j, 2 * qk_w, 2 * v_w, BF16)
    log_gamma = jax.nn.log_sigmoid(logit_gamma.astype(F32))
    yf, yb = _retention_scan(log_gamma, q, k, vg, n_ctx, heads, dk, dv)
    yn = _ret_post(yf, yb, vg, gn_w, gn_b, heads, dv)
    return _matmul_residual(yn, w_out, j, xa, mod, 2, n_ctx)


def _mlstm_layer(xa, h, mod, w_in, conv_w, b_gate, norm_w, w_out, j, n_ctx):
    t, d = xa.shape
    heads = MLSTM_HEADS
    dk = d // (2 * heads)
    dv = d // heads
    qk_w, v_w = heads * dk, heads * dv
    n_gates = 4 * heads
    qk_pre = _matmul(h, w_in, j, 0, 2 * qk_w, F32)
    vo = _matmul(h, w_in, j, 2 * qk_w, 2 * v_w, BF16)
    w_gates = jnp.pad(w_in[j, :, 2 * qk_w + 2 * v_w:], ((0, 0), (0, LANES - n_gates)))
    gates = _matmul(h, w_gates[None], 0, 0, LANES, F32)
    bias = jnp.pad(b_gate.astype(F32).reshape(1, n_gates), ((0, 0), (0, LANES - n_gates)))
    qk = _conv_silu(qk_pre, conv_w, n_ctx, qk_w, dk ** -0.5)
    yf = _mlstm_scan(qk, vo, gates, bias, n_ctx, heads, dk, dv, reverse=False)
    yb = _mlstm_scan(qk, vo, gates, bias, n_ctx, heads, dk, dv, reverse=True)
    yn = _mlstm_post(yf, yb, vo, norm_w, heads, dv)
    return _matmul_residual(yn, w_out, j, xa, mod, 2, n_ctx)


def _attention_layer(xa, h, mod, w_in, q_norm, k_norm, w_out, j, n_ctx):
    t, d = xa.shape
    hd = d // ATTN_HEADS
    groups = ATTN_HEADS // ATTN_KV_HEADS
    q_w, kv_w = ATTN_HEADS * hd, ATTN_KV_HEADS * hd
    n_lat = t - n_ctx
    qk_pre = _matmul(h, w_in, j, 0, q_w + kv_w, F32)
    v = _matmul(h, w_in, j, q_w + kv_w, kv_w, BF16)
    cos, sin = _axial_tables(n_ctx, n_lat, hd)
    q_gain = q_norm * (hd ** -0.5 * math.log2(math.e))
    q, k = _qk_norm_rope(qk_pre, cos, sin, q_gain, k_norm, ATTN_HEADS, ATTN_KV_HEADS, hd)
    o = _flash_attention(q, k, v, n_ctx, ATTN_KV_HEADS, groups, hd)
    return _matmul_residual(o, w_out, j, xa, mod, 2, n_ctx)


def kernel(x, c, ctx, c_ctx, w_ada, b_ada, norm_g, ret_w_in, ret_logit_gamma, ret_gn_w, ret_gn_b, ret_w_out, mlstm_w_in, mlstm_conv_w, mlstm_b_gate, mlstm_norm_w, mlstm_w_out, attn_w_in, attn_q_norm, attn_k_norm, attn_w_out, moe_w_router_group, moe_b_router_group, moe_w_router_expert, moe_b_router_expert, moe_w_gate, moe_w_up, moe_w_down, final_norm_g):
    bsz, n_lat, d = x.shape
    n_ctx = ctx.shape[1]
    depth = w_ada.shape[0]
    assert bsz == 1 and n_ctx % ROW_TILE == 0 and n_lat % ROW_TILE == 0 and n_lat % GRID_W == 0
    xa = jnp.concatenate([ctx[0], x[0]], axis=0)
    s = jnp.stack([jax.nn.silu(c[0]), jax.nn.silu(c_ctx)])
    s8 = jnp.zeros((8, d), F32).at[:2].set(s).astype(BF16)
    mods = _ada_modulation(s8, w_ada, b_ada)
    xs_buf = None
    h = _norm_mod(xa, norm_g[0, 0], mods[0], 0, 1, n_ctx)
    for i in range(depth):
        kind, j = i % 3, i // 3
        mod = mods[i]
        last = i == depth - 1
        if kind == 0:
            xa = _retention_layer(xa, h, mod, ret_w_in, ret_logit_gamma[j], ret_gn_w[j], ret_gn_b[j],
                                  ret_w_out, j, n_ctx)
        elif kind == 1:
            xa = _mlstm_layer(xa, h, mod, mlstm_w_in, mlstm_conv_w[j], mlstm_b_gate[j],
                              mlstm_norm_w[j], mlstm_w_out, j, n_ctx)
        else:
            xa = _attention_layer(xa, h, mod, attn_w_in, attn_q_norm[j], attn_k_norm[j],
                                  attn_w_out, j, n_ctx)
        row0 = n_ctx if last else 0
        g_next = final_norm_g if last else norm_g[i + 1, 0]
        mod_next = mod if last else mods[i + 1]
        out, xs_buf = _hier_moe(xa, norm_g[i, 1], mod, moe_w_router_group[i], moe_b_router_group[i],
                                moe_w_router_expert[i], moe_b_router_expert[i], moe_w_gate, moe_w_up, moe_w_down, i,
                                n_ctx, row0, xs_buf, g_next, mod_next, last)
        if last:
            return out[None]
        xa, h = out
```
